```python
import math
import jax, jax.numpy as jnp
from jax import lax
import numpy as np

D_MODEL = 1024
BATCH = 4
SEQ = 8192
DEPTH = 2
DEC_BATCH = 8
DEC_SEQ = 64
PAST_LEN = 2048

CHUNK = 64
Q_BLOCK = 128
SB_HEADS = 8
SB_HEAD_DIM = D_MODEL // 16
SB_WIDTH = SB_HEADS * SB_HEAD_DIM
GDN_HEADS = 4
GDN_HEAD_DIM = D_MODEL // 8
GDN_WIDTH = GDN_HEADS * GDN_HEAD_DIM
CONV_WIDTH = 4
CONV_DIM = 3 * GDN_WIDTH
N_BRANCH = 2
D_FF = 11 * D_MODEL // 4
NORM_EPS = 1e-6
IN_SIZES = (SB_WIDTH, SB_WIDTH, SB_WIDTH, CONV_DIM, GDN_WIDTH, GDN_HEADS, GDN_HEADS, N_BRANCH * D_MODEL)
IN_DIM = sum(IN_SIZES)

kernel_name = 'stickbreak_gdn_macaron_stream_step'


def rms_norm(x, gain):
    xf = x.astype(jnp.float32)
    y = xf * lax.rsqrt(jnp.mean(xf * xf, axis=-1, keepdims=True) + NORM_EPS)
    return (y * gain.astype(jnp.float32)).astype(x.dtype)


def l2_norm(x):
    xf = x.astype(jnp.float32)
    return xf * lax.rsqrt(jnp.sum(xf * xf, axis=-1, keepdims=True) + NORM_EPS)


def swiglu_ffn(h, w_up, w_down):
    gate, up = jnp.split(h @ w_up, 2, axis=-1)
    return (jax.nn.silu(gate) * up) @ w_down


def causal_conv(u, hist, w):
    L = u.shape[1]
    up = jnp.concatenate([hist.astype(u.dtype), u], axis=1)
    y = up[:, 0:L] * w[0]
    for i in range(1, CONV_WIDTH):
        y = y + up[:, i:i + L] * w[i]
    return y, up[:, L:]


def stick_breaking_block(q, k, v, q_pos, k_pos):
    z = jnp.einsum('bqhd,bkhd->bhqk', q, k, preferred_element_type=jnp.float32) * (SB_HEAD_DIM ** -0.5)
    visible = k_pos[None, :] < q_pos[:, None]
    neg_log_keep = jnp.where(visible, jax.nn.softplus(z), 0.0)
    between = lax.cumsum(neg_log_keep, axis=3, reverse=True) - neg_log_keep
    log_a = jax.nn.log_sigmoid(z) - between
    a = jnp.where(visible, jnp.exp(log_a), 0.0)
    return jnp.einsum('bhqk,bkhd->bqhd', a.astype(v.dtype), v)


def stick_breaking_prompt(q, k, v):
    B, L, H, D = q.shape
    nb = L // Q_BLOCK
    qb = q.reshape(B, nb, Q_BLOCK, H, D).swapaxes(0, 1)
    k_pos = jnp.arange(L, dtype=jnp.int32)

    def one_block(args):
        q_blk, blk = args
        q_pos = blk * Q_BLOCK + jnp.arange(Q_BLOCK, dtype=jnp.int32)
        return stick_breaking_block(q_blk, k, v, q_pos, k_pos)

    o = lax.map(one_block, (qb, jnp.arange(nb, dtype=jnp.int32)))
    return o.swapaxes(0, 1).reshape(B, L, H, D)


def gated_delta_rule(q, k, v, g, beta, s0, chunk):
    B, L, H, _ = q.shape
    DV = v.shape[-1]
    n = L // chunk
    f32 = jnp.float32

    def blocks(t):
        t = t.astype(f32).reshape((B, n, chunk, H) + t.shape[3:])
        return jnp.moveaxis(t, (1, 3), (0, 2))

    qc, kc, vc, gc, bc = blocks(q), blocks(k), blocks(v), blocks(g), blocks(beta)
    gcum = jnp.cumsum(gc, axis=-1)
    idx = jnp.arange(chunk)
    incl = idx[:, None] >= idx[None, :]
    strict = idx[:, None] > idx[None, :]
    gamma = jnp.exp(jnp.where(incl, gcum[..., :, None] - gcum[..., None, :], -jnp.inf))
    kb = kc * bc[..., None]
    a_strict = jnp.where(strict, jnp.einsum('nbhid,nbhjd->nbhij', kb, kc) * gamma, 0.0)
    t_mat = a_strict + jnp.eye(chunk, dtype=f32)
    rhs = jnp.concatenate([vc * bc[..., None], kb * jnp.exp(gcum)[..., None]], axis=-1)
    sol = lax.linalg.triangular_solve(t_mat, rhs, left_side=True, lower=True, unit_diagonal=True)
    u, w = sol[..., :DV], sol[..., DV:]
    qk = jnp.einsum('nbhid,nbhjd->nbhij', qc, kc) * gamma
    q_dec = qc * jnp.exp(gcum)[..., None]
    k_end = kc * jnp.exp(gcum[..., -1:] - gcum)[..., None]
    chunk_decay = jnp.exp(gcum[..., -1])

    def step(s, xs):
        u_c, w_c, qk_c, qd_c, ke_c, cd_c = xs
        v_new = u_c - jnp.einsum('bhck,bhkv->bhcv', w_c, s)
        o = jnp.einsum('bhck,bhkv->bhcv', qd_c, s) + jnp.einsum('bhij,bhjv->bhiv', qk_c, v_new)
        s = s * cd_c[..., None, None] + jnp.einsum('bhck,bhcv->bhkv', ke_c, v_new)
        return s, o

    s_fin, o = lax.scan(step, s0.astype(f32), (u, w, qk, q_dec, k_end, chunk_decay))
    o = jnp.moveaxis(o, (0, 2), (1, 3)).reshape(B, L, H, DV)
    return o, s_fin


def gdn_branch(qkv_raw, z, b, a, conv_hist, s0, conv_w, a_log, dt_bias, head_gain, chunk):
    B, L, _ = qkv_raw.shape
    qkv, conv_new = causal_conv(qkv_raw, conv_hist, conv_w)
    qkv = jax.nn.silu(qkv)
    q, k, v = jnp.split(qkv, 3, axis=-1)
    hs = (B, L, GDN_HEADS, GDN_HEAD_DIM)
    q = l2_norm(q.reshape(hs)) * (GDN_HEAD_DIM ** -0.5)
    k = l2_norm(k.reshape(hs))
    v = v.reshape(hs)
    beta = jax.nn.sigmoid(b.astype(jnp.float32))
    g = -jnp.exp(a_log.astype(jnp.float32)) * jax.nn.softplus(a.astype(jnp.float32) + dt_bias.astype(jnp.float32))
    o, s_fin = gated_delta_rule(q, k, v, g, beta, s0, chunk)
    o = rms_norm(o, head_gain) * jax.nn.silu(z.reshape(hs).astype(jnp.float32))
    return o.reshape(B, L, GDN_WIDTH).astype(qkv_raw.dtype), s_fin, conv_new


def token_mixer(h, w_in, conv_w, a_log, dt_bias, gdn_gain, w_branch_sb, w_branch_gdn, w_out,
                past_k, past_v, conv_hist, s0, chunk):
    B, L, _ = h.shape
    split_points = np.cumsum(IN_SIZES)[:-1].tolist()
    sb_q, sb_k, sb_v, gdn_qkv, gdn_z, gdn_b, gdn_a, gate_logits = jnp.split(h @ w_in, split_points, axis=-1)
    hs = (B, L, SB_HEADS, SB_HEAD_DIM)
    sb_q, sb_k, sb_v = sb_q.reshape(hs), sb_k.reshape(hs), sb_v.reshape(hs)
    if past_k is None:
        o_sb = stick_breaking_prompt(sb_q, sb_k, sb_v)
    else:
        P = past_k.shape[1]
        k_all = jnp.concatenate([past_k.astype(sb_k.dtype), sb_k], axis=1)
        v_all = jnp.concatenate([past_v.astype(sb_v.dtype), sb_v], axis=1)
        q_pos = P + jnp.arange(L, dtype=jnp.int32)
        k_pos = jnp.arange(P + L, dtype=jnp.int32)
        o_sb = stick_breaking_block(sb_q, k_all, v_all, q_pos, k_pos)
    o_gdn, s_fin, conv_new = gdn_branch(gdn_qkv, gdn_z, gdn_b, gdn_a, conv_hist, s0, conv_w,
                                        a_log, dt_bias, gdn_gain, chunk)
    gates = jax.nn.sigmoid(gate_logits.astype(jnp.float32)).reshape(B, L, N_BRANCH, D_MODEL)
    merged = (gates[:, :, 0] * (o_sb.reshape(B, L, SB_WIDTH) @ w_branch_sb)
              + gates[:, :, 1] * (o_gdn @ w_branch_gdn))
    return merged.astype(h.dtype) @ w_out, sb_k, sb_v, s_fin, conv_new


def encoder_layer(x, norm_g, w1_up, w1_down, w_in, conv_w, a_log, dt_bias, gdn_gain,
                  w_branch_sb, w_branch_gdn, w_out, w2_up, w2_down,
                  past_k, past_v, conv_hist, s0, chunk):
    x = x + 0.5 * rms_norm(swiglu_ffn(rms_norm(x, norm_g[0]), w1_up, w1_down), norm_g[1])
    m, k_rows, v_rows, s_fin, conv_new = token_mixer(
        rms_norm(x, norm_g[2]), w_in, conv_w, a_log, dt_bias, gdn_gain, w_branch_sb, w_branch_gdn,
        w_out, past_k, past_v, conv_hist, s0, chunk)
    x = x + rms_norm(m, norm_g[3])
    x = x + 0.5 * rms_norm(swiglu_ffn(rms_norm(x, norm_g[4]), w2_up, w2_down), norm_g[5])
    return x, k_rows, v_rows, s_fin, conv_new


def setup_inputs(seed: int = 0) -> dict:
    key = jax.random.key(seed)
    ks = jax.random.split(key, 20)
    f32 = jnp.float32

    def nrm(k, shape, scale):
        return jax.random.normal(k, shape, f32) * scale

    dt = jnp.exp(jax.random.uniform(ks[13], (DEPTH, GDN_HEADS), f32, math.log(1e-3), math.log(1e-1)))
    return {
        'x_prompt': nrm(ks[0], (BATCH, SEQ, D_MODEL), 1.0),
        'x_sample': nrm(ks[1], (DEC_BATCH, DEC_SEQ, D_MODEL), 1.0),
        'cache_sb_k': nrm(ks[2], (DEPTH, DEC_BATCH, PAST_LEN, SB_HEADS, SB_HEAD_DIM), 1.0),
        'cache_sb_v': nrm(ks[3], (DEPTH, DEC_BATCH, PAST_LEN, SB_HEADS, SB_HEAD_DIM), 1.0),
        'state_gdn': nrm(ks[4], (DEPTH, DEC_BATCH, GDN_HEADS, GDN_HEAD_DIM, GDN_HEAD_DIM), 0.1),
        'state_conv': nrm(ks[5], (DEPTH, DEC_BATCH, CONV_WIDTH - 1, CONV_DIM), 1.0),
        'norm_gains': 1.0 + nrm(ks[6], (DEPTH, 6, D_MODEL), 0.02),
        'w_ffn1_up': nrm(ks[7], (DEPTH, D_MODEL, 2 * D_FF), D_MODEL ** -0.5),
        'w_ffn1_down': nrm(ks[8], (DEPTH, D_FF, D_MODEL), D_FF ** -0.5),
        'w_in': nrm(ks[9], (DEPTH, D_MODEL, IN_DIM), D_MODEL ** -0.5),
        'conv_w': nrm(ks[10], (DEPTH, CONV_WIDTH, CONV_DIM), CONV_WIDTH ** -0.5),
        'gdn_a_log': jnp.log(jax.random.uniform(ks[11], (DEPTH, GDN_HEADS), f32, 1.0, 16.0)),
        'gdn_dt_bias': dt + jnp.log(-jnp.expm1(-dt)),
        'gdn_norm_gain': 1.0 + nrm(ks[12], (DEPTH, GDN_HEAD_DIM), 0.02),
        'w_branch_sb': nrm(ks[14], (DEPTH, SB_WIDTH, D_MODEL), SB_WIDTH ** -0.5),
        'w_branch_gdn': nrm(ks[15], (DEPTH, GDN_WIDTH, D_MODEL), GDN_WIDTH ** -0.5),
        'w_out': nrm(ks[16], (DEPTH, D_MODEL, D_MODEL), D_MODEL ** -0.5),
        'w_ffn2_up': nrm(ks[17], (DEPTH, D_MODEL, 2 * D_FF), D_MODEL ** -0.5),
        'w_ffn2_down': nrm(ks[18], (DEPTH, D_FF, D_MODEL), D_FF ** -0.5),
    }


def reference(x_prompt, x_sample, cache_sb_k, cache_sb_v, state_gdn, state_conv, norm_gains,
              w_ffn1_up, w_ffn1_down, w_in, conv_w, gdn_a_log, gdn_dt_bias, gdn_norm_gain,
              w_branch_sb, w_branch_gdn, w_out, w_ffn2_up, w_ffn2_down):
    def run(x, past_k, past_v, conv_hist, s0, chunk):
        k_list, v_list, s_list, c_list = [], [], [], []
        for l in range(DEPTH):
            pk = None if past_k is None else past_k[l]
            pv = None if past_v is None else past_v[l]
            x, k_rows, v_rows, s_fin, conv_new = encoder_layer(
                x, norm_gains[l], w_ffn1_up[l], w_ffn1_down[l], w_in[l], conv_w[l], gdn_a_log[l],
                gdn_dt_bias[l], gdn_norm_gain[l], w_branch_sb[l], w_branch_gdn[l], w_out[l],
                w_ffn2_up[l], w_ffn2_down[l], pk, pv, conv_hist[l], s0[l], chunk)
            k_list.append(k_rows)
            v_list.append(v_rows)
            s_list.append(s_fin)
            c_list.append(conv_new)
        return (x, jnp.stack(k_list).astype(cache_sb_k.dtype), jnp.stack(v_list).astype(cache_sb_v.dtype),
                jnp.stack(s_list).astype(state_gdn.dtype), jnp.stack(c_list).astype(state_conv.dtype))

    zero_conv = jnp.zeros((DEPTH, x_prompt.shape[0], CONV_WIDTH - 1, CONV_DIM), state_conv.dtype)
    zero_state = jnp.zeros((DEPTH, x_prompt.shape[0], GDN_HEADS, GDN_HEAD_DIM, GDN_HEAD_DIM), state_gdn.dtype)
    y_prompt, pk, pv, ps, pc = run(x_prompt, None, None, zero_conv, zero_state, CHUNK)
    y_sample, sk, sv, ss, sc = run(x_sample, cache_sb_k, cache_sb_v, state_conv, state_gdn, x_sample.shape[1])
    return (y_prompt, y_sample, pk, pv, ps, pc, sk, sv, ss, sc)
```

```python
import functools

import jax
import jax.numpy as jnp
from jax import lax
from jax.experimental import pallas as pl
from jax.experimental.pallas import tpu as pltpu

F32 = jnp.float32
BF16 = jnp.bfloat16

NORM_EPS = 1e-6
SB_HEADS = 8
GDN_HEADS = 4
LANES = 128
VMEM_LIMIT_BYTES = 56 * 1024 * 1024

ROW_TILE = 512
FF_CHUNK = 256
SB_TILE = 256


def _params(*sem):
    return pltpu.CompilerParams(dimension_semantics=sem, vmem_limit_bytes=VMEM_LIMIT_BYTES)


def _const_spec(shape):
    zeros = (0,) * len(shape)
    return pl.BlockSpec(shape, lambda *_: zeros)


def _rms(x, gain):
    ms = jnp.mean(x * x, axis=-1, keepdims=True)
    return x * lax.rsqrt(ms + NORM_EPS) * gain


def _dot(a, b):
    return jnp.dot(a, b, preferred_element_type=F32)


def _dot_nt(a, b):
    return lax.dot_general(a, b, (((1,), (1,)), ((), ())), preferred_element_type=F32)


def _dot_tn(a, b):
    return lax.dot_general(a, b, (((0,), (0,)), ((), ())), preferred_element_type=F32)


def _split2(x):
    hi = x.astype(BF16)
    lo = (x - hi.astype(F32)).astype(BF16)
    return hi, lo


def _split3(x):
    hi = x.astype(BF16)
    r = x - hi.astype(F32)
    mid = r.astype(BF16)
    lo = (r - mid.astype(F32)).astype(BF16)
    return hi, mid, lo


def _dot_f32_exactrhs(a, b_bf16):
    hi, mid, lo = _split3(a)
    return _dot(hi, b_bf16) + _dot(mid, b_bf16) + _dot(lo, b_bf16)


def _dot_exactlhs_f32(a_bf16, b):
    hi, mid, lo = _split3(b)
    return _dot(a_bf16, hi) + _dot(a_bf16, mid) + _dot(a_bf16, lo)


def _dot_x3(a, b):
    ah, al = _split2(a)
    bh, bl = _split2(b)
    return _dot(ah, bh) + _dot(ah, bl) + _dot(al, bh)


def _ffn_kernel(x_ref, gin_ref, gout_ref, wg_ref, wu_ref, wd_ref, o_ref, act_ref):
    x = x_ref[...]
    h = _rms(x, gin_ref[...]).astype(BF16)
    d_ff = wd_ref.shape[0]
    for c in range(d_ff // FF_CHUNK):
        sl = slice(c * FF_CHUNK, (c + 1) * FF_CHUNK)
        g = _dot(h, wg_ref[:, sl])
        u = _dot(h, wu_ref[:, sl])
        act_ref[:, sl] = (g * jax.nn.sigmoid(g) * u).astype(BF16)
    y = _dot(act_ref[...], wd_ref[...])
    o_ref[...] = x + 0.5 * _rms(y, gout_ref[...])


def _ffn(x2, g_in, g_out, wg, wu, wd):
    n, d = x2.shape
    d_ff = wd.shape[0]
    tm = min(ROW_TILE, n)
    row = pl.BlockSpec((tm, d), lambda i: (i, 0))
    return pl.pallas_call(
        _ffn_kernel,
        grid=(n // tm,),
        in_specs=[row, _const_spec((1, d)), _const_spec((1, d)),
                  _const_spec((d, d_ff)), _const_spec((d, d_ff)), _const_spec((d_ff, d))],
        out_specs=row,
        out_shape=jax.ShapeDtypeStruct((n, d), F32),
        scratch_shapes=[pltpu.VMEM((tm, d_ff), BF16)],
        compiler_params=_params("parallel"),
        name="ffn_half_step",
    )(x2, g_in, g_out, wg, wu, wd)


def _inproj_kernel(x_ref, g_ref, wa_ref, wba_ref,
                   q_ref, k_ref, v_ref, kt_ref, vb_ref, u_ref, z_ref, ba_ref, bat_ref,
                   *, sb_w, conv_dim, gdn_w, sb_tile, chunk, q_scale):
    h = _rms(x_ref[0], g_ref[...]).astype(BF16)
    tm = h.shape[0]

    def proj(lo, width):
        return _dot(h, wa_ref[:, lo:lo + width])

    q_ref[0] = (proj(0, sb_w) * q_scale).astype(BF16)
    k = proj(sb_w, sb_w)
    k_ref[0] = k
    kt = k.T.astype(BF16)
    for hp in range(sb_w // LANES):
        for s in range(tm // sb_tile):
            kt_ref[0, hp, s] = kt[hp * LANES:(hp + 1) * LANES, s * sb_tile:(s + 1) * sb_tile]
    v = proj(2 * sb_w, sb_w)
    v_ref[0] = v
    vb_ref[0] = v.astype(BF16)
    u_ref[0] = proj(3 * sb_w, conv_dim)
    z_ref[0] = proj(3 * sb_w + conv_dim, gdn_w)
    ba = _dot(h, wba_ref[...])
    ba_ref[0] = ba
    bat = ba.T[:8]
    for s in range(tm // chunk):
        bat_ref[0, s] = bat[:, s * chunk:(s + 1) * chunk]


def _inproj(x, gain, wa, wba, *, sb_w, conv_dim, gdn_w, sb_tile, chunk):
    b, l, d = x.shape
    tm = min(ROW_TILE, l)
    n_hp = sb_w // LANES
    kern = functools.partial(_inproj_kernel, sb_w=sb_w, conv_dim=conv_dim, gdn_w=gdn_w,
                             sb_tile=sb_tile, chunk=chunk,
                             q_scale=float((sb_w // SB_HEADS) ** -0.5))

    def rows(width):
        return pl.BlockSpec((1, tm, width), lambda bi, ti: (bi, ti, 0))

    out_shape = (
        jax.ShapeDtypeStruct((b, l, sb_w), BF16),
        jax.ShapeDtypeStruct((b, l, sb_w), F32),
        jax.ShapeDtypeStruct((b, l, sb_w), F32),
        jax.ShapeDtypeStruct((b, n_hp, l // sb_tile, LANES, sb_tile), BF16),
        jax.ShapeDtypeStruct((b, l, sb_w), BF16),
        jax.ShapeDtypeStruct((b, l, conv_dim), F32),
        jax.ShapeDtypeStruct((b, l, gdn_w), F32),
        jax.ShapeDtypeStruct((b, l, LANES), F32),
        jax.ShapeDtypeStruct((b, l // chunk, 8, chunk), F32),
    )
    out_specs = (
        rows(sb_w), rows(sb_w), rows(sb_w),
        pl.BlockSpec((1, n_hp, tm // sb_tile, LANES, sb_tile), lambda bi, ti: (bi, 0, ti, 0, 0)),
        rows(sb_w), rows(conv_dim), rows(gdn_w), rows(LANES),
        pl.BlockSpec((1, tm // chunk, 8, chunk), lambda bi, ti: (bi, ti, 0, 0)),
    )
    return pl.pallas_call(
        kern,
        grid=(b, l // tm),
        in_specs=[rows(d), _const_spec((1, d)), _const_spec(wa.shape), _const_spec(wba.shape)],
        out_specs=out_specs,
        out_shape=out_shape,
        compiler_params=_params("parallel", "parallel"),
        name="in_projection",
    )(x, gain, wa, wba)


def _sb_kernel(q_ref, ktd_ref, vd_ref, ktp_ref, vp_ref, o_ref, *, tq, tk, causal_past, n_past_static):
    dh = LANES // 2
    qi = pl.program_id(2)

    def later_key_matrix(n):
        r = lax.broadcasted_iota(jnp.int32, (n, n), 0)
        c = lax.broadcasted_iota(jnp.int32, (n, n), 1)
        return (r > c).astype(BF16)

    def block(q_h, kt_blk, v_blk, u_mat, carry, acc, visible):
        z = _dot(q_h, kt_blk)
        sp = jnp.maximum(z, 0.0) + jnp.log(1.0 + jnp.exp(-jnp.abs(z)))
        nlk = sp if visible is None else jnp.where(visible, sp, 0.0)
        hi, lo = _split2(nlk)
        between = _dot(hi, u_mat) + _dot(lo, u_mat)
        log_a = (z - sp) - (between + carry)
        a = jnp.exp(log_a)
        if visible is not None:
            a = jnp.where(visible, a, 0.0)
        acc = acc + _dot(a.astype(BF16), v_blk)
        carry = carry + jnp.sum(nlk, axis=1, keepdims=True)
        return carry, acc

    u_diag = later_key_matrix(tq)
    rows_i = lax.broadcasted_iota(jnp.int32, (tq, tq), 0)
    cols_i = lax.broadcasted_iota(jnp.int32, (tq, tq), 1)
    visible = cols_i < rows_i

    q2 = q_ref[0]
    state = []
    for hh in range(2):
        lanes = slice(hh * dh, (hh + 1) * dh)
        carry0 = jnp.zeros((tq, 1), F32)
        acc0 = jnp.zeros((tq, dh), F32)
        state.extend(block(q2[:, lanes], ktd_ref[0, 0, 0, lanes, :], vd_ref[0, :, lanes],
                           u_diag, carry0, acc0, visible))

    u_past = u_diag if tk == tq else later_key_matrix(tk)
    n_past = qi if causal_past else n_past_static

    def body(i, st):
        j = n_past - 1 - i
        row0 = pl.multiple_of(j * tk, tk)
        kt2 = ktp_ref[0, 0, j]
        v2 = vp_ref[0, pl.ds(row0, tk), :]
        out = []
        for hh in range(2):
            lanes = slice(hh * dh, (hh + 1) * dh)
            out.extend(block(q2[:, lanes], kt2[lanes, :], v2[:, lanes], u_past,
                             st[2 * hh], st[2 * hh + 1], None))
        return tuple(out)

    st = lax.fori_loop(0, n_past, body, tuple(state))
    o_ref[0] = jnp.concatenate([st[1], st[3]], axis=1).astype(BF16)


def _sb_attention(q, kt_diag, v_diag, kt_past, v_past, *, tq, tk, causal_past):
    b, l, w = q.shape
    n_hp = w // LANES
    n_past_blocks = kt_past.shape[2]
    p = v_past.shape[1]
    kern = functools.partial(_sb_kernel, tq=tq, tk=tk, causal_past=causal_past,
                             n_past_static=n_past_blocks)
    return pl.pallas_call(
        kern,
        grid=(b, n_hp, l // tq),
        in_specs=[
            pl.BlockSpec((1, tq, LANES), lambda bi, hp, qi: (bi, qi, hp)),
            pl.BlockSpec((1, 1, 1, LANES, tq), lambda bi, hp, qi: (bi, hp, qi, 0, 0)),
            pl.BlockSpec((1, tq, LANES), lambda bi, hp, qi: (bi, qi, hp)),
            pl.BlockSpec((1, 1, n_past_blocks, LANES, tk), lambda bi, hp, qi: (bi, hp, 0, 0, 0)),
            pl.BlockSpec((1, p, LANES), lambda bi, hp, qi: (bi, 0, hp)),
        ],
        out_specs=pl.BlockSpec((1, tq, LANES), lambda bi, hp, qi: (bi, qi, hp)),
        out_shape=jax.ShapeDtypeStruct((b, l, w), BF16),
        compiler_params=_params("parallel", "parallel", "arbitrary"),
        name="stick_breaking_attention",
    )(q, kt_diag, v_diag, kt_past, v_past)


def _gdn_kernel(u_ref, z_ref, ba_ref, bat_ref, hist_ref, s0_ref, cw_ref,
                alog_l_ref, dt_l_ref, alog_s_ref, dt_s_ref, hg_ref,
                o_ref, s_ref, cnew_ref, ext_ref, *, chunk, width, n_taps):
    c = pl.program_id(1)
    n_c = pl.num_programs(1)
    dk = width // GDN_HEADS
    pad = 8
    n_hist = n_taps - 1

    @pl.when(c == 0)
    def _():
        ext_ref[0:pad, :] = jnp.zeros((pad, ext_ref.shape[1]), F32)
        ext_ref[pad - n_hist:pad, :] = hist_ref[0]
        s_ref[...] = s0_ref[...]

    ext_ref[pad:pad + chunk, :] = u_ref[0]
    y = ext_ref[pad - n_hist:pad - n_hist + chunk, :] * cw_ref[0:1, :]
    for i in range(1, n_taps):
        y = y + ext_ref[pad - n_hist + i:pad - n_hist + i + chunk, :] * cw_ref[i:i + 1, :]
    new_tail = ext_ref[chunk:chunk + pad, :]
    ext_ref[0:pad, :] = new_tail

    @pl.when(c == n_c - 1)
    def _():
        cnew_ref[0] = new_tail[pad - n_hist:pad, :]

    qkv = y * jax.nn.sigmoid(y)

    r_i = lax.broadcasted_iota(jnp.int32, (chunk, chunk), 0)
    c_i = lax.broadcasted_iota(jnp.int32, (chunk, chunk), 1)
    incl = r_i >= c_i
    strict = r_i > c_i
    lower_incl = incl.astype(BF16)
    upper_incl = (r_i <= c_i).astype(BF16)
    eye = (r_i == c_i).astype(F32)

    def softplus(t):
        return jnp.maximum(t, 0.0) + jnp.log(1.0 + jnp.exp(-jnp.abs(t)))

    ba = ba_ref[0]
    g_cols = -jnp.exp(alog_l_ref[...]) * softplus(ba + dt_l_ref[...])
    gcum_cols = _dot_exactlhs_f32(lower_incl, g_cols)
    beta_cols = jax.nn.sigmoid(ba)
    bat = bat_ref[0, 0]
    g_rows = -jnp.exp(alog_s_ref[...]) * softplus(bat + dt_s_ref[...])
    gcum_rows = _dot_f32_exactrhs(g_rows, upper_incl)

    z = z_ref[0]
    outs = []
    for h in range(GDN_HEADS):
        q = qkv[:, h * dk:(h + 1) * dk]
        k = qkv[:, width + h * dk:width + (h + 1) * dk]
        v = qkv[:, 2 * width + h * dk:2 * width + (h + 1) * dk]
        q = q * lax.rsqrt(jnp.sum(q * q, axis=-1, keepdims=True) + NORM_EPS) * float(dk ** -0.5)
        k = k * lax.rsqrt(jnp.sum(k * k, axis=-1, keepdims=True) + NORM_EPS)
        gc_col = gcum_cols[:, GDN_HEADS + h:GDN_HEADS + h + 1]
        gc_row = gcum_rows[GDN_HEADS + h:GDN_HEADS + h + 1, :]
        gc_last = gc_row[:, chunk - 1:chunk]
        beta = beta_cols[:, h:h + 1]
        gamma = jnp.where(incl, jnp.exp(gc_col - gc_row), 0.0)
        decay_in = jnp.exp(gc_col)
        kb = k * beta
        k_bf = k.astype(BF16)
        a_strict = jnp.where(strict, _dot_nt(kb.astype(BF16), k_bf) * gamma, 0.0)
        n_pow = -a_strict
        t_inv = eye + n_pow
        steps = max(1, (chunk - 1).bit_length()) - 1
        for _ in range(steps):
            n_pow = _dot_x3(n_pow, n_pow)
            t_inv = t_inv + _dot_x3(t_inv, n_pow)
        rhs = jnp.concatenate([v * beta, kb * decay_in], axis=1)
        sol = _dot_x3(t_inv, rhs)
        u_c = sol[:, :dk]
        w_c = sol[:, dk:]
        qk = _dot_nt(q.astype(BF16), k_bf) * gamma
        q_dec = q * decay_in
        k_end = k * jnp.exp(gc_last - gc_col)
        s = s_ref[0, h]
        s_bf = s.astype(BF16)
        v_new = u_c - _dot(w_c.astype(BF16), s_bf)
        v_new_bf = v_new.astype(BF16)
        o_h = _dot(q_dec.astype(BF16), s_bf) + _dot(qk.astype(BF16), v_new_bf)
        s_ref[0, h] = s * jnp.exp(gc_last) + _dot_tn(k_end.astype(BF16), v_new_bf)
        z_h = z[:, h * dk:(h + 1) * dk]
        outs.append(_rms(o_h, hg_ref[...]) * (z_h * jax.nn.sigmoid(z_h)))
    o_ref[0] = jnp.concatenate(outs, axis=1).astype(BF16)


def _gdn(u, z, ba, bat, hist, s0, conv_w, alog_l, dt_l, alog_s, dt_s, head_gain, *, chunk):
    b, l, conv_dim = u.shape
    width = conv_dim // 3
    dk = width // GDN_HEADS
    n_taps = conv_w.shape[0]
    kern = functools.partial(_gdn_kernel, chunk=chunk, width=width, n_taps=n_taps)

    def rows(wd):
        return pl.BlockSpec((1, chunk, wd), lambda bi, ci: (bi, ci, 0))

    state_spec = pl.BlockSpec((1, GDN_HEADS, dk, dk), lambda bi, ci: (bi, 0, 0, 0))
    hist_spec = pl.BlockSpec((1, n_taps - 1, conv_dim), lambda bi, ci: (bi, 0, 0))
    return pl.pallas_call(
        kern,
        grid=(b, l // chunk),
        in_specs=[rows(conv_dim), rows(width), rows(LANES),
                  pl.BlockSpec((1, 1, 8, chunk), lambda bi, ci: (bi, ci, 0, 0)),
                  hist_spec, state_spec, _const_spec(conv_w.shape),
                  _const_spec((1, LANES)), _const_spec((1, LANES)),
                  _const_spec((8, 1)), _const_spec((8, 1)), _const_spec((1, dk))],
        out_specs=(rows(width), state_spec, hist_spec),
        out_shape=(jax.ShapeDtypeStruct((b, l, width), BF16),
                   jax.ShapeDtypeStruct(s0.shape, F32),
                   jax.ShapeDtypeStruct(hist.shape, F32)),
        scratch_shapes=[pltpu.VMEM((chunk + 8, conv_dim), F32)],
        compiler_params=_params("parallel", "arbitrary"),
        name="gated_delta_rule",
    )(u, z, ba, bat, hist, s0, conv_w, alog_l, dt_l, alog_s, dt_s, head_gain)


def _merge_kernel(x_ref, osb_ref, ogdn_ref, gin_ref, gout_ref, wgate_ref, wsb_ref, wgdn_ref, wout_ref, o_ref):
    x = x_ref[...]
    d = x.shape[1]
    h = _rms(x, gin_ref[...]).astype(BF16)
    gate_sb = jax.nn.sigmoid(_dot(h, wgate_ref[:, :d]))
    merged = gate_sb * _dot(osb_ref[...], wsb_ref[...])
    gate_gdn = jax.nn.sigmoid(_dot(h, wgate_ref[:, d:]))
    merged = merged + gate_gdn * _dot(ogdn_ref[...], wgdn_ref[...])
    m = _dot(merged.astype(BF16), wout_ref[...])
    o_ref[...] = x + _rms(m, gout_ref[...])


def _merge(x2, osb2, ogdn2, g_in, g_out, wgate, wsb, wgdn, wout):
    n, d = x2.shape
    tm = min(ROW_TILE, n)

    def rows(wd):
        return pl.BlockSpec((tm, wd), lambda i: (i, 0))

    return pl.pallas_call(
        _merge_kernel,
        grid=(n // tm,),
        in_specs=[rows(d), rows(osb2.shape[1]), rows(ogdn2.shape[1]),
                  _const_spec((1, d)), _const_spec((1, d)),
                  _const_spec(wgate.shape), _const_spec(wsb.shape), _const_spec(wgdn.shape),
                  _const_spec(wout.shape)],
        out_specs=rows(d),
        out_shape=jax.ShapeDtypeStruct((n, d), F32),
        compiler_params=_params("parallel"),
        name="branch_merge",
    )(x2, osb2, ogdn2, g_in, g_out, wgate, wsb, wgdn, wout)


def _layer_weights(l, norm_gains, w_ffn1_up, w_ffn1_down, w_in, conv_w, gdn_a_log, gdn_dt_bias,
                   gdn_norm_gain, w_branch_sb, w_branch_gdn, w_out, w_ffn2_up, w_ffn2_down):
    d = w_in.shape[1]
    d_ff = w_ffn1_down.shape[1]
    sb_w = w_branch_sb.shape[1]
    gdn_w = w_branch_gdn.shape[1]
    conv_dim = conv_w.shape[2]
    n_a = 3 * sb_w + conv_dim + gdn_w
    wi = w_in[l]
    wba = jnp.zeros((d, LANES), F32).at[:, :2 * GDN_HEADS].set(wi[:, n_a:n_a + 2 * GDN_HEADS])

    def lane_vec(p):
        return jnp.zeros((1, LANES), F32).at[0, GDN_HEADS:2 * GDN_HEADS].set(p)

    def sublane_vec(p):
        return jnp.zeros((8, 1), F32).at[GDN_HEADS:2 * GDN_HEADS, 0].set(p)

    return dict(
        gains=[norm_gains[l, i][None, :] for i in range(6)],
        ffn1=(w_ffn1_up[l][:, :d_ff].astype(BF16), w_ffn1_up[l][:, d_ff:].astype(BF16),
              w_ffn1_down[l].astype(BF16)),
        ffn2=(w_ffn2_up[l][:, :d_ff].astype(BF16), w_ffn2_up[l][:, d_ff:].astype(BF16),
              w_ffn2_down[l].astype(BF16)),
        wa=wi[:, :n_a].astype(BF16),
        wba=wba.astype(BF16),
        wgate=wi[:, n_a + 2 * GDN_HEADS:].astype(BF16),
        wsb=w_branch_sb[l].astype(BF16), wgdn=w_branch_gdn[l].astype(BF16), wout=w_out[l].astype(BF16),
        conv_w=conv_w[l],
        alog_l=lane_vec(gdn_a_log[l]), dt_l=lane_vec(gdn_dt_bias[l]),
        alog_s=sublane_vec(gdn_a_log[l]), dt_s=sublane_vec(gdn_dt_bias[l]),
        head_gain=gdn_norm_gain[l][None, :],
        dims=(sb_w, conv_dim, gdn_w),
    )


def _run_group(x, weights, past_k, past_v, conv_hist, s0, chunk):
    b, l, d = x.shape
    n = b * l
    k_list, v_list, s_list, c_list = [], [], [], []
    for li, w in enumerate(weights):
        sb_w, conv_dim, gdn_w = w["dims"]
        dh = sb_w // SB_HEADS
        tq = min(SB_TILE, l)
        g = w["gains"]
        x2 = _ffn(x.reshape(n, d), g[0], g[1], *w["ffn1"])
        q, k_rows, v_rows, kt, vb, u, z, ba, bat = _inproj(
            x2.reshape(b, l, d), g[2], w["wa"], w["wba"],
            sb_w=sb_w, conv_dim=conv_dim, gdn_w=gdn_w, sb_tile=tq, chunk=chunk)
        if past_k is None:
            o_sb = _sb_attention(q, kt, vb, kt, vb, tq=tq, tk=tq, causal_past=True)
        else:
            p = past_k.shape[2]
            tk = min(SB_TILE, p)
            ktp = past_k[li].astype(BF16).reshape(b, p // tk, tk, sb_w // LANES, LANES)
            ktp = ktp.transpose(0, 3, 1, 4, 2)
            vp = past_v[li].astype(BF16).reshape(b, p, sb_w)
            o_sb = _sb_attention(q, kt, vb, ktp, vp, tq=tq, tk=tk, causal_past=False)
        o_gdn, s_fin, conv_new = _gdn(u, z, ba, bat, conv_hist[li], s0[li], w["conv_w"],
                                      w["alog_l"], w["dt_l"], w["alog_s"], w["dt_s"], w["head_gain"],
                                      chunk=chunk)
        x2 = _merge(x2, o_sb.reshape(n, sb_w), o_gdn.reshape(n, gdn_w), g[2], g[3],
                    w["wgate"], w["wsb"], w["wgdn"], w["wout"])
        x2 = _ffn(x2, g[4], g[5], *w["ffn2"])
        x = x2.reshape(b, l, d)
        k_list.append(k_rows.reshape(b, l, SB_HEADS, dh))
        v_list.append(v_rows.reshape(b, l, SB_HEADS, dh))
        s_list.append(s_fin)
        c_list.append(conv_new)
    return x, jnp.stack(k_list), jnp.stack(v_list), jnp.stack(s_list), jnp.stack(c_list)


def kernel(x_prompt, x_sample, cache_sb_k, cache_sb_v, state_gdn, state_conv, norm_gains,
           w_ffn1_up, w_ffn1_down, w_in, conv_w, gdn_a_log, gdn_dt_bias, gdn_norm_gain,
           w_branch_sb, w_branch_gdn, w_out, w_ffn2_up, w_ffn2_down):
    depth = w_in.shape[0]
    weights = [_layer_weights(l, norm_gains, w_ffn1_up, w_ffn1_down, w_in, conv_w, gdn_a_log,
                              gdn_dt_bias, gdn_norm_gain, w_branch_sb, w_branch_gdn, w_out,
                              w_ffn2_up, w_ffn2_down) for l in range(depth)]
    bp = x_prompt.shape[0]
    chunk_prompt = 64
    zero_conv = jnp.zeros((depth, bp) + state_conv.shape[2:], state_conv.dtype)
    zero_state = jnp.zeros((depth, bp) + state_gdn.shape[2:], state_gdn.dtype)
    y_p, pk, pv, ps, pc = _run_group(x_prompt, weights, None, None, zero_conv, zero_state, chunk_prompt)
    y_s, sk, sv, ss, sc = _run_group(x_sample, weights, cache_sb_k, cache_sb_v, state_conv, state_gdn,
                                     x_sample.shape[1])
    return (y_p, y_s, pk, pv, ps, pc, sk, sv, ss, sc)
```

```python
import functools

import jax
import jax.numpy as jnp
from jax import lax
from jax.experimental import pallas as pl
from jax.experimental.pallas import tpu as pltpu

F32 = jnp.float32
BF16 = jnp.bfloat16

NORM_EPS = 1e-6
LOG2E = 1.4426950408889634
SB_HEADS = 8
GDN_HEADS = 4
LANES = 128
VMEM_LIMIT_BYTES = 56 * 1024 * 1024

ROW_TILE = 512
FF_CHUNK = 256
SB_KEY_TILE = 256
SB_QUERY_TILE = 512
SB_GROUP_LANES = 256


def _params(*sem):
    return pltpu.CompilerParams(dimension_semantics=sem, vmem_limit_bytes=VMEM_LIMIT_BYTES)


def _const_spec(shape):
    zeros = (0,) * len(shape)
    return pl.BlockSpec(shape, lambda *_: zeros)


def _rms(x, gain):
    ms = jnp.mean(x * x, axis=-1, keepdims=True)
    return x * lax.rsqrt(ms + NORM_EPS) * gain


def _dot(a, b):
    return jnp.dot(a, b, preferred_element_type=F32)


def _dot_nt(a, b):
    return lax.dot_general(a, b, (((1,), (1,)), ((), ())), preferred_element_type=F32)


def _dot_tn(a, b):
    return lax.dot_general(a, b, (((0,), (0,)), ((), ())), preferred_element_type=F32)


def _split2(x):
    hi = x.astype(BF16)
    lo = (x - hi.astype(F32)).astype(BF16)
    return hi, lo


def _split3(x):
    hi = x.astype(BF16)
    r = x - hi.astype(F32)
    mid = r.astype(BF16)
    lo = (r - mid.astype(F32)).astype(BF16)
    return hi, mid, lo


def _dot_f32_exactrhs(a, b_bf16):
    hi, mid, lo = _split3(a)
    return _dot(hi, b_bf16) + _dot(mid, b_bf16) + _dot(lo, b_bf16)


def _dot_exactlhs_f32(a_bf16, b):
    hi, mid, lo = _split3(b)
    return _dot(a_bf16, hi) + _dot(a_bf16, mid) + _dot(a_bf16, lo)


def _dot_x3(a, b):
    ah, al = _split2(a)
    bh, bl = _split2(b)
    return _dot(ah, bh) + _dot(ah, bl) + _dot(al, bh)


def _ffn_kernel(x_ref, gin_ref, gout_ref, wg_ref, wu_ref, wd_ref, o_ref, act_ref):
    x = x_ref[...]
    h = _rms(x, gin_ref[...]).astype(BF16)
    d_ff = wd_ref.shape[0]
    for c in range(d_ff // FF_CHUNK):
        sl = slice(c * FF_CHUNK, (c + 1) * FF_CHUNK)
        g = _dot(h, wg_ref[:, sl])
        u = _dot(h, wu_ref[:, sl])
        act_ref[:, sl] = (g * jax.nn.sigmoid(g) * u).astype(BF16)
    y = _dot(act_ref[...], wd_ref[...])
    o_ref[...] = x + 0.5 * _rms(y, gout_ref[...])


def _ffn(x2, g_in, g_out, wg, wu, wd):
    n, d = x2.shape
    d_ff = wd.shape[0]
    tm = min(ROW_TILE, n)
    row = pl.BlockSpec((tm, d), lambda i: (i, 0))
    return pl.pallas_call(
        _ffn_kernel,
        grid=(n // tm,),
        in_specs=[row, _const_spec((1, d)), _const_spec((1, d)),
                  _const_spec((d, d_ff)), _const_spec((d, d_ff)), _const_spec((d_ff, d))],
        out_specs=row,
        out_shape=jax.ShapeDtypeStruct((n, d), F32),
        scratch_shapes=[pltpu.VMEM((tm, d_ff), BF16)],
        compiler_params=_params("parallel"),
        name="ffn_half_step",
    )(x2, g_in, g_out, wg, wu, wd)


def _inproj_kernel(x_ref, g_ref, wa_ref, wba_ref,
                   q_ref, k_ref, v_ref, kt_ref, vb_ref, u_ref, z_ref, ba_ref, bat_ref,
                   *, sb_w, conv_dim, gdn_w, sb_tile, chunk, q_scale):
    h = _rms(x_ref[0], g_ref[...]).astype(BF16)
    tm = h.shape[0]

    def proj(lo, width):
        return _dot(h, wa_ref[:, lo:lo + width])

    q_ref[0] = (proj(0, sb_w) * q_scale).astype(BF16)
    k = proj(sb_w, sb_w)
    k_ref[0] = k
    kt = k.T.astype(BF16)
    for hp in range(sb_w // SB_GROUP_LANES):
        for s in range(tm // sb_tile):
            kt_ref[0, hp, s] = kt[hp * SB_GROUP_LANES:(hp + 1) * SB_GROUP_LANES,
                                  s * sb_tile:(s + 1) * sb_tile]
    v = proj(2 * sb_w, sb_w)
    v_ref[0] = v
    vb_ref[0] = v.astype(BF16)
    u_ref[0] = proj(3 * sb_w, conv_dim)
    z_ref[0] = proj(3 * sb_w + conv_dim, gdn_w)
    ba = _dot(h, wba_ref[...])
    ba_ref[0] = ba
    bat = ba.T[:8]
    for s in range(tm // chunk):
        bat_ref[0, s] = bat[:, s * chunk:(s + 1) * chunk]


def _inproj(x, gain, wa, wba, *, sb_w, conv_dim, gdn_w, sb_tile, chunk):
    b, l, d = x.shape
    tm = min(ROW_TILE, l)
    n_hp = sb_w // SB_GROUP_LANES
    kern = functools.partial(_inproj_kernel, sb_w=sb_w, conv_dim=conv_dim, gdn_w=gdn_w,
                             sb_tile=sb_tile, chunk=chunk,
                             q_scale=float((sb_w // SB_HEADS) ** -0.5))

    def rows(width):
        return pl.BlockSpec((1, tm, width), lambda bi, ti: (bi, ti, 0))

    out_shape = (
        jax.ShapeDtypeStruct((b, l, sb_w), BF16),
        jax.ShapeDtypeStruct((b, l, sb_w), F32),
        jax.ShapeDtypeStruct((b, l, sb_w), F32),
        jax.ShapeDtypeStruct((b, n_hp, l // sb_tile, SB_GROUP_LANES, sb_tile), BF16),
        jax.ShapeDtypeStruct((b, l, sb_w), BF16),
        jax.ShapeDtypeStruct((b, l, conv_dim), F32),
        jax.ShapeDtypeStruct((b, l, gdn_w), F32),
        jax.ShapeDtypeStruct((b, l, LANES), F32),
        jax.ShapeDtypeStruct((b, l // chunk, 8, chunk), F32),
    )
    out_specs = (
        rows(sb_w), rows(sb_w), rows(sb_w),
        pl.BlockSpec((1, n_hp, tm // sb_tile, SB_GROUP_LANES, sb_tile), lambda bi, ti: (bi, 0, ti, 0, 0)),
        rows(sb_w), rows(conv_dim), rows(gdn_w), rows(LANES),
        pl.BlockSpec((1, tm // chunk, 8, chunk), lambda bi, ti: (bi, ti, 0, 0)),
    )
    return pl.pallas_call(
        kern,
        grid=(b, l // tm),
        in_specs=[rows(d), _const_spec((1, d)), _const_spec(wa.shape), _const_spec(wba.shape)],
        out_specs=out_specs,
        out_shape=out_shape,
        compiler_params=_params("parallel", "parallel"),
        name="in_projection",
    )(x, gain, wa, wba)


def _sb_kernel(q_ref, ktd_ref, vd_ref, ktp_ref, vp_ref, o_ref, *, tq, td, nd, tk, heads, dh,
               causal_past, n_past_static):
    qi = pl.program_id(2)

    def from_key_matrix(n):
        r = lax.broadcasted_iota(jnp.int32, (n, n), 0)
        c = lax.broadcasted_iota(jnp.int32, (n, n), 1)
        return (r >= c).astype(BF16)

    def sweep(blocks, u_mat, st):
        pairs = [(blk, hh) for blk in blocks for hh in range(heads)]
        scores = []
        for (kt_blk, _, visible), hh in pairs:
            lanes = slice(hh * dh, (hh + 1) * dh)
            z = _dot(q_all[:, lanes], kt_blk[lanes, :])
            sp = jnp.maximum(z, 0.0) + jnp.log(1.0 + jnp.exp2(jnp.abs(z) * (-LOG2E)))
            nlk = sp if visible is None else jnp.where(visible, sp, 0.0)
            scores.append((z, nlk.astype(BF16), jnp.sum(nlk, axis=1, keepdims=True)))
        csums = [_dot(nlk_b, u_mat) for _, nlk_b, _ in scores]
        st = list(st)
        for ((_, v_blk, visible), hh), (z, _, row_sum), csum in zip(pairs, scores, csums):
            lanes = slice(hh * dh, (hh + 1) * dh)
            carry = st[2 * hh]
            a = jnp.exp(z - csum - carry)
            if visible is not None:
                a = jnp.where(visible, a, 0.0)
            st[2 * hh + 1] = st[2 * hh + 1] + _dot(a.astype(BF16), v_blk[:, lanes])
            st[2 * hh] = carry + row_sum
        return st

    q_all = q_ref[0]
    state = []
    for hh in range(heads):
        state.extend([jnp.zeros((tq, 1), F32), jnp.zeros((tq, dh), F32)])

    u_diag = from_key_matrix(td)
    rows_i = lax.broadcasted_iota(jnp.int32, (tq, td), 0)
    cols_i = lax.broadcasted_iota(jnp.int32, (tq, td), 1)
    diag_blocks = [(ktd_ref[0, 0, sd], vd_ref[0, sd * td:(sd + 1) * td, :], cols_i + sd * td < rows_i)
                   for sd in reversed(range(nd))]
    state = sweep(diag_blocks, u_diag, state)

    u_past = u_diag if tk == td else from_key_matrix(tk)
    n_pairs = qi * (tq // (2 * tk)) if causal_past else n_past_static // 2

    def body(i, st):
        blocks = []
        for sub in (1, 0):
            j = 2 * (n_pairs - 1 - i) + sub
            row0 = pl.multiple_of(j * tk, tk)
            blocks.append((ktp_ref[0, 0, j], vp_ref[0, pl.ds(row0, tk), :], None))
        return tuple(sweep(blocks, u_past, st))

    st = lax.fori_loop(0, n_pairs, body, tuple(state))
    o_ref[0] = jnp.concatenate([st[2 * hh + 1] for hh in range(heads)], axis=1).astype(BF16)


def _sb_attention(q, kt_diag, v_diag, kt_past, v_past, *, tq, td, tk, causal_past):
    b, l, w = q.shape
    n_g, hw = kt_diag.shape[1], kt_diag.shape[3]
    n_past_blocks = kt_past.shape[2]
    assert tq % td == 0 and n_past_blocks % 2 == 0 and (not causal_past or tq % (2 * tk) == 0)
    p = v_past.shape[1]
    dh = w // SB_HEADS
    kern = functools.partial(_sb_kernel, tq=tq, td=td, nd=tq // td, tk=tk, heads=hw // dh, dh=dh,
                             causal_past=causal_past, n_past_static=n_past_blocks)
    return pl.pallas_call(
        kern,
        grid=(b, n_g, l // tq),
        in_specs=[
            pl.BlockSpec((1, tq, hw), lambda bi, hp, qi: (bi, qi, hp)),
            pl.BlockSpec((1, 1, tq // td, hw, td), lambda bi, hp, qi: (bi, hp, qi, 0, 0)),
            pl.BlockSpec((1, tq, hw), lambda bi, hp, qi: (bi, qi, hp)),
            pl.BlockSpec((1, 1, n_past_blocks, hw, tk), lambda bi, hp, qi: (bi, hp, 0, 0, 0)),
            pl.BlockSpec((1, p, hw), lambda bi, hp, qi: (bi, 0, hp)),
        ],
        out_specs=pl.BlockSpec((1, tq, hw), lambda bi, hp, qi: (bi, qi, hp)),
        out_shape=jax.ShapeDtypeStruct((b, l, w), BF16),
        compiler_params=_params("parallel", "parallel", "arbitrary"),
        name="stick_breaking_attention",
    )(q, kt_diag, v_diag, kt_past, v_past)


def _gdn_kernel(u_ref, z_ref, ba_ref, bat_ref, hist_ref, s0_ref, cw_ref,
                alog_l_ref, dt_l_ref, alog_s_ref, dt_s_ref, hg_ref,
                o_ref, s_ref, cnew_ref, ext_ref, *, chunk, width, n_taps):
    c = pl.program_id(1)
    n_c = pl.num_programs(1)
    dk = width // GDN_HEADS
    pad = 8
    n_hist = n_taps - 1

    @pl.when(c == 0)
    def _():
        ext_ref[0:pad, :] = jnp.zeros((pad, ext_ref.shape[1]), F32)
        ext_ref[pad - n_hist:pad, :] = hist_ref[0]
        s_ref[...] = s0_ref[...]

    ext_ref[pad:pad + chunk, :] = u_ref[0]
    y = ext_ref[pad - n_hist:pad - n_hist + chunk, :] * cw_ref[0:1, :]
    for i in range(1, n_taps):
        y = y + ext_ref[pad - n_hist + i:pad - n_hist + i + chunk, :] * cw_ref[i:i + 1, :]
    new_tail = ext_ref[chunk:chunk + pad, :]
    ext_ref[0:pad, :] = new_tail

    @pl.when(c == n_c - 1)
    def _():
        cnew_ref[0] = new_tail[pad - n_hist:pad, :]

    qkv = y * jax.nn.sigmoid(y)

    r_i = lax.broadcasted_iota(jnp.int32, (chunk, chunk), 0)
    c_i = lax.broadcasted_iota(jnp.int32, (chunk, chunk), 1)
    incl = r_i >= c_i
    strict = r_i > c_i
    lower_incl = incl.astype(BF16)
    upper_incl = (r_i <= c_i).astype(BF16)
    eye = (r_i == c_i).astype(F32)

    def softplus(t):
        return jnp.maximum(t, 0.0) + jnp.log(1.0 + jnp.exp(-jnp.abs(t)))

    ba = ba_ref[0]
    g_cols = -jnp.exp(alog_l_ref[...]) * softplus(ba + dt_l_ref[...])
    gcum_cols = _dot_exactlhs_f32(lower_incl, g_cols)
    beta_cols = jax.nn.sigmoid(ba)
    bat = bat_ref[0, 0]
    g_rows = -jnp.exp(alog_s_ref[...]) * softplus(bat + dt_s_ref[...])
    gcum_rows = _dot_f32_exactrhs(g_rows, upper_incl)

    z = z_ref[0]
    outs = []
    for h in range(GDN_HEADS):
        q = qkv[:, h * dk:(h + 1) * dk]
        k = qkv[:, width + h * dk:width + (h + 1) * dk]
        v = qkv[:, 2 * width + h * dk:2 * width + (h + 1) * dk]
        q = q * lax.rsqrt(jnp.sum(q * q, axis=-1, keepdims=True) + NORM_EPS) * float(dk ** -0.5)
        k = k * lax.rsqrt(jnp.sum(k * k, axis=-1, keepdims=True) + NORM_EPS)
        gc_col = gcum_cols[:, GDN_HEADS + h:GDN_HEADS + h + 1]
        gc_row = gcum_rows[GDN_HEADS + h:GDN_HEADS + h + 1, :]
        gc_last = gc_row[:, chunk - 1:chunk]
        beta = beta_cols[:, h:h + 1]
        gamma = jnp.where(incl, jnp.exp(gc_col - gc_row), 0.0)
        decay_in = jnp.exp(gc_col)
        kb = k * beta
        k_bf = k.astype(BF16)
        a_strict = jnp.where(strict, _dot_nt(kb.astype(BF16), k_bf) * gamma, 0.0)
        n_pow = -a_strict
        t_inv = eye + n_pow
        steps = max(1, (chunk - 1).bit_length()) - 1
        for _ in range(steps):
            n_pow = _dot_x3(n_pow, n_pow)
            t_inv = t_inv + _dot_x3(t_inv, n_pow)
        rhs = jnp.concatenate([v * beta, kb * decay_in], axis=1)
        sol = _dot_x3(t_inv, rhs)
        u_c = sol[:, :dk]
        w_c = sol[:, dk:]
        qk = _dot_nt(q.astype(BF16), k_bf) * gamma
        q_dec = q * decay_in
        k_end = k * jnp.exp(gc_last - gc_col)
        s = s_ref[0, h]
        s_bf = s.astype(BF16)
        v_new = u_c - _dot(w_c.astype(BF16), s_bf)
        v_new_bf = v_new.astype(BF16)
        o_h = _dot(q_dec.astype(BF16), s_bf) + _dot(qk.astype(BF16), v_new_bf)
        s_ref[0, h] = s * jnp.exp(gc_last) + _dot_tn(k_end.astype(BF16), v_new_bf)
        z_h = z[:, h * dk:(h + 1) * dk]
        outs.append(_rms(o_h, hg_ref[...]) * (z_h * jax.nn.sigmoid(z_h)))
    o_ref[0] = jnp.concatenate(outs, axis=1).astype(BF16)


def _gdn(u, z, ba, bat, hist, s0, conv_w, alog_l, dt_l, alog_s, dt_s, head_gain, *, chunk):
    b, l, conv_dim = u.shape
    width = conv_dim // 3
    dk = width // GDN_HEADS
    n_taps = conv_w.shape[0]
    kern = functools.partial(_gdn_kernel, chunk=chunk, width=width, n_taps=n_taps)

    def rows(wd):
        return pl.BlockSpec((1, chunk, wd), lambda bi, ci: (bi, ci, 0))

    state_spec = pl.BlockSpec((1, GDN_HEADS, dk, dk), lambda bi, ci: (bi, 0, 0, 0))
    hist_spec = pl.BlockSpec((1, n_taps - 1, conv_dim), lambda bi, ci: (bi, 0, 0))
    return pl.pallas_call(
        kern,
        grid=(b, l // chunk),
        in_specs=[rows(conv_dim), rows(width), rows(LANES),
                  pl.BlockSpec((1, 1, 8, chunk), lambda bi, ci: (bi, ci, 0, 0)),
                  hist_spec, state_spec, _const_spec(conv_w.shape),
                  _const_spec((1, LANES)), _const_spec((1, LANES)),
                  _const_spec((8, 1)), _const_spec((8, 1)), _const_spec((1, dk))],
        out_specs=(rows(width), state_spec, hist_spec),
        out_shape=(jax.ShapeDtypeStruct((b, l, width), BF16),
                   jax.ShapeDtypeStruct(s0.shape, F32),
                   jax.ShapeDtypeStruct(hist.shape, F32)),
        scratch_shapes=[pltpu.VMEM((chunk + 8, conv_dim), F32)],
        compiler_params=_params("parallel", "arbitrary"),
        name="gated_delta_rule",
    )(u, z, ba, bat, hist, s0, conv_w, alog_l, dt_l, alog_s, dt_s, head_gain)


def _merge_kernel(x_ref, osb_ref, ogdn_ref, gin_ref, gout_ref, wgate_ref, wsb_ref, wgdn_ref, wout_ref, o_ref):
    x = x_ref[...]
    d = x.shape[1]
    h = _rms(x, gin_ref[...]).astype(BF16)
    gate_sb = jax.nn.sigmoid(_dot(h, wgate_ref[:, :d]))
    merged = gate_sb * _dot(osb_ref[...], wsb_ref[...])
    gate_gdn = jax.nn.sigmoid(_dot(h, wgate_ref[:, d:]))
    merged = merged + gate_gdn * _dot(ogdn_ref[...], wgdn_ref[...])
    m = _dot(merged.astype(BF16), wout_ref[...])
    o_ref[...] = x + _rms(m, gout_ref[...])


def _merge(x2, osb2, ogdn2, g_in, g_out, wgate, wsb, wgdn, wout):
    n, d = x2.shape
    tm = min(ROW_TILE, n)

    def rows(wd):
        return pl.BlockSpec((tm, wd), lambda i: (i, 0))

    return pl.pallas_call(
        _merge_kernel,
        grid=(n // tm,),
        in_specs=[rows(d), rows(osb2.shape[1]), rows(ogdn2.shape[1]),
                  _const_spec((1, d)), _const_spec((1, d)),
                  _const_spec(wgate.shape), _const_spec(wsb.shape), _const_spec(wgdn.shape),
                  _const_spec(wout.shape)],
        out_specs=rows(d),
        out_shape=jax.ShapeDtypeStruct((n, d), F32),
        compiler_params=_params("parallel"),
        name="branch_merge",
    )(x2, osb2, ogdn2, g_in, g_out, wgate, wsb, wgdn, wout)


def _layer_weights(l, norm_gains, w_ffn1_up, w_ffn1_down, w_in, conv_w, gdn_a_log, gdn_dt_bias,
                   gdn_norm_gain, w_branch_sb, w_branch_gdn, w_out, w_ffn2_up, w_ffn2_down):
    d = w_in.shape[1]
    d_ff = w_ffn1_down.shape[1]
    sb_w = w_branch_sb.shape[1]
    gdn_w = w_branch_gdn.shape[1]
    conv_dim = conv_w.shape[2]
    n_a = 3 * sb_w + conv_dim + gdn_w
    wi = w_in[l]
    wba = jnp.zeros((d, LANES), F32).at[:, :2 * GDN_HEADS].set(wi[:, n_a:n_a + 2 * GDN_HEADS])

    def lane_vec(p):
        return jnp.zeros((1, LANES), F32).at[0, GDN_HEADS:2 * GDN_HEADS].set(p)

    def sublane_vec(p):
        return jnp.zeros((8, 1), F32).at[GDN_HEADS:2 * GDN_HEADS, 0].set(p)

    return dict(
        gains=[norm_gains[l, i][None, :] for i in range(6)],
        ffn1=(w_ffn1_up[l][:, :d_ff].astype(BF16), w_ffn1_up[l][:, d_ff:].astype(BF16),
              w_ffn1_down[l].astype(BF16)),
        ffn2=(w_ffn2_up[l][:, :d_ff].astype(BF16), w_ffn2_up[l][:, d_ff:].astype(BF16),
              w_ffn2_down[l].astype(BF16)),
        wa=wi[:, :n_a].astype(BF16),
        wba=wba.astype(BF16),
        wgate=wi[:, n_a + 2 * GDN_HEADS:].astype(BF16),
        wsb=w_branch_sb[l].astype(BF16), wgdn=w_branch_gdn[l].astype(BF16), wout=w_out[l].astype(BF16),
        conv_w=conv_w[l],
        alog_l=lane_vec(gdn_a_log[l]), dt_l=lane_vec(gdn_dt_bias[l]),
        alog_s=sublane_vec(gdn_a_log[l]), dt_s=sublane_vec(gdn_dt_bias[l]),
        head_gain=gdn_norm_gain[l][None, :],
        dims=(sb_w, conv_dim, gdn_w),
    )


def _run_group(x, weights, past_k, past_v, conv_hist, s0, chunk):
    b, l, d = x.shape
    n = b * l
    k_list, v_list, s_list, c_list = [], [], [], []
    for li, w in enumerate(weights):
        sb_w, conv_dim, gdn_w = w["dims"]
        dh = sb_w // SB_HEADS
        tq = min(SB_QUERY_TILE, l)
        td = min(SB_KEY_TILE, l)
        g = w["gains"]
        x2 = _ffn(x.reshape(n, d), g[0], g[1], *w["ffn1"])
        q, k_rows, v_rows, kt, vb, u, z, ba, bat = _inproj(
            x2.reshape(b, l, d), g[2], w["wa"], w["wba"],
            sb_w=sb_w, conv_dim=conv_dim, gdn_w=gdn_w, sb_tile=td, chunk=chunk)
        if past_k is None:
            o_sb = _sb_attention(q, kt, vb, kt, vb, tq=tq, td=td, tk=td, causal_past=True)
        else:
            p = past_k.shape[2]
            tk = min(SB_KEY_TILE, p)
            ktp = past_k[li].astype(BF16).reshape(b, p // tk, tk, sb_w // SB_GROUP_LANES, SB_GROUP_LANES)
            ktp = ktp.transpose(0, 3, 1, 4, 2)
            vp = past_v[li].astype(BF16).reshape(b, p, sb_w)
            o_sb = _sb_attention(q, kt, vb, ktp, vp, tq=tq, td=td, tk=tk, causal_past=False)
        o_gdn, s_fin, conv_new = _gdn(u, z, ba, bat, conv_hist[li], s0[li], w["conv_w"],
                                      w["alog_l"], w["dt_l"], w["alog_s"], w["dt_s"], w["head_gain"],
                                      chunk=chunk)
        x2 = _merge(x2, o_sb.reshape(n, sb_w), o_gdn.reshape(n, gdn_w), g[2], g[3],
                    w["wgate"], w["wsb"], w["wgdn"], w["wout"])
        x2 = _ffn(x2, g[4], g[5], *w["ffn2"])
        x = x2.reshape(b, l, d)
        k_list.append(k_rows.reshape(b, l, SB_HEADS, dh))
        v_list.append(v_rows.reshape(b, l, SB_HEADS, dh))
        s_list.append(s_fin)
        c_list.append(conv_new)
    return x, jnp.stack(k_list), jnp.stack(v_list), jnp.stack(s_list), jnp.stack(c_list)


def kernel(x_prompt, x_sample, cache_sb_k, cache_sb_v, state_gdn, state_conv, norm_gains,
           w_ffn1_up, w_ffn1_down, w_in, conv_w, gdn_a_log, gdn_dt_bias, gdn_norm_gain,
           w_branch_sb, w_branch_gdn, w_out, w_ffn2_up, w_ffn2_down):
    depth = w_in.shape[0]
    weights = [_layer_weights(l, norm_gains, w_ffn1_up, w_ffn1_down, w_in, conv_w, gdn_a_log,
                              gdn_dt_bias, gdn_norm_gain, w_branch_sb, w_branch_gdn, w_out,
                              w_ffn2_up, w_ffn2_down) for l in range(depth)]
    bp = x_prompt.shape[0]
    chunk_prompt = 64
    zero_conv = jnp.zeros((depth, bp) + state_conv.shape[2:], state_conv.dtype)
    zero_state = jnp.zeros((depth, bp) + state_gdn.shape[2:], state_gdn.dtype)
    y_p, pk, pv, ps, pc = _run_group(x_prompt, weights, None, None, zero_conv, zero_state, chunk_prompt)
    y_s, sk, sv, ss, sc = _run_group(x_sample, weights, cache_sb_k, cache_sb_v, state_conv, state_gdn,
                                     x_sample.shape[1])
    return (y_p, y_s, pk, pv, ps, pc, sk, sv, ss, sc)
```

```python
import functools

import jax
import jax.numpy as jnp
from jax import lax
from jax.experimental import pallas as pl
from jax.experimental.pallas import tpu as pltpu

F32 = jnp.float32
BF16 = jnp.bfloat16

NORM_EPS = 1e-6
LOG2E = 1.4426950408889634
SB_HEADS = 8
GDN_HEADS = 4
LANES = 128
VMEM_LIMIT_BYTES = 56 * 1024 * 1024

ROW_TILE = 512
FF_CHUNK = 256
GDN_ROWS = 256
SB_KEY_TILE = 256
SB_QUERY_TILE = 512
SB_GROUP_LANES = 256


def _params(*sem):
    return pltpu.CompilerParams(dimension_semantics=sem, vmem_limit_bytes=VMEM_LIMIT_BYTES)


def _const_spec(shape):
    zeros = (0,) * len(shape)
    return pl.BlockSpec(shape, lambda *_: zeros)


def _rms(x, gain):
    ms = jnp.mean(x * x, axis=-1, keepdims=True)
    return x * lax.rsqrt(ms + NORM_EPS) * gain


def _dot(a, b):
    return jnp.dot(a, b, preferred_element_type=F32)


def _dot_nt(a, b):
    return lax.dot_general(a, b, (((1,), (1,)), ((), ())), preferred_element_type=F32)


def _dot_tn(a, b):
    return lax.dot_general(a, b, (((0,), (0,)), ((), ())), preferred_element_type=F32)


def _split2(x):
    hi = x.astype(BF16)
    lo = (x - hi.astype(F32)).astype(BF16)
    return hi, lo


def _split3(x):
    hi = x.astype(BF16)
    r = x - hi.astype(F32)
    mid = r.astype(BF16)
    lo = (r - mid.astype(F32)).astype(BF16)
    return hi, mid, lo


def _dot_f32_exactrhs(a, b_bf16):
    hi, mid, lo = _split3(a)
    return _dot(hi, b_bf16) + _dot(mid, b_bf16) + _dot(lo, b_bf16)


def _dot_exactlhs_f32(a_bf16, b):
    hi, mid, lo = _split3(b)
    return _dot(a_bf16, hi) + _dot(a_bf16, mid) + _dot(a_bf16, lo)


def _dot_x3(a, b):
    ah, al = _split2(a)
    bh, bl = _split2(b)
    return _dot(ah, bh) + _dot(ah, bl) + _dot(al, bh)


def _ffn_kernel(x_ref, gin_ref, gout_ref, wg_ref, wu_ref, wd_ref, o_ref, act_ref):
    x = x_ref[...]
    h = _rms(x, gin_ref[...]).astype(BF16)
    d_ff = wd_ref.shape[0]
    for c in range(d_ff // FF_CHUNK):
        sl = slice(c * FF_CHUNK, (c + 1) * FF_CHUNK)
        g = _dot(h, wg_ref[:, sl])
        u = _dot(h, wu_ref[:, sl])
        act_ref[:, sl] = (g * jax.nn.sigmoid(g) * u).astype(BF16)
    y = _dot(act_ref[...], wd_ref[...])
    o_ref[...] = x + 0.5 * _rms(y, gout_ref[...])


def _ffn(x2, g_in, g_out, wg, wu, wd):
    n, d = x2.shape
    d_ff = wd.shape[0]
    tm = min(ROW_TILE, n)
    row = pl.BlockSpec((tm, d), lambda i: (i, 0))
    return pl.pallas_call(
        _ffn_kernel,
        grid=(n // tm,),
        in_specs=[row, _const_spec((1, d)), _const_spec((1, d)),
                  _const_spec((d, d_ff)), _const_spec((d, d_ff)), _const_spec((d_ff, d))],
        out_specs=row,
        out_shape=jax.ShapeDtypeStruct((n, d), F32),
        scratch_shapes=[pltpu.VMEM((tm, d_ff), BF16)],
        compiler_params=_params("parallel"),
        name="ffn_half_step",
    )(x2, g_in, g_out, wg, wu, wd)


def _inproj_kernel(x_ref, g_ref, wa_ref, wba_ref,
                   q_ref, k_ref, v_ref, kt_ref, vb_ref, u_ref, z_ref, ba_ref, bat_ref,
                   *, sb_w, conv_dim, gdn_w, sb_tile, q_scale):
    h = _rms(x_ref[0], g_ref[...]).astype(BF16)
    tm = h.shape[0]

    def proj(lo, width):
        return _dot(h, wa_ref[:, lo:lo + width])

    q_ref[0] = (proj(0, sb_w) * q_scale).astype(BF16)
    k = proj(sb_w, sb_w)
    k_ref[0] = k
    kt = k.T.astype(BF16)
    for hp in range(sb_w // SB_GROUP_LANES):
        for s in range(tm // sb_tile):
            kt_ref[0, hp, s] = kt[hp * SB_GROUP_LANES:(hp + 1) * SB_GROUP_LANES,
                                  s * sb_tile:(s + 1) * sb_tile]
    v = proj(2 * sb_w, sb_w)
    v_ref[0] = v
    vb_ref[0] = v.astype(BF16)
    u_ref[0] = proj(3 * sb_w, conv_dim)
    z_ref[0] = proj(3 * sb_w + conv_dim, gdn_w)
    ba = _dot(h, wba_ref[...])
    ba_ref[0] = ba
    bat_ref[0] = ba.T[:8]


def _inproj(x, gain, wa, wba, *, sb_w, conv_dim, gdn_w, sb_tile):
    b, l, d = x.shape
    tm = min(ROW_TILE, l)
    n_hp = sb_w // SB_GROUP_LANES
    kern = functools.partial(_inproj_kernel, sb_w=sb_w, conv_dim=conv_dim, gdn_w=gdn_w,
                             sb_tile=sb_tile,
                             q_scale=float((sb_w // SB_HEADS) ** -0.5))

    def rows(width):
        return pl.BlockSpec((1, tm, width), lambda bi, ti: (bi, ti, 0))

    out_shape = (
        jax.ShapeDtypeStruct((b, l, sb_w), BF16),
        jax.ShapeDtypeStruct((b, l, sb_w), F32),
        jax.ShapeDtypeStruct((b, l, sb_w), F32),
        jax.ShapeDtypeStruct((b, n_hp, l // sb_tile, SB_GROUP_LANES, sb_tile), BF16),
        jax.ShapeDtypeStruct((b, l, sb_w), BF16),
        jax.ShapeDtypeStruct((b, l, conv_dim), F32),
        jax.ShapeDtypeStruct((b, l, gdn_w), F32),
        jax.ShapeDtypeStruct((b, l, LANES), F32),
        jax.ShapeDtypeStruct((b, 8, l), F32),
    )
    out_specs = (
        rows(sb_w), rows(sb_w), rows(sb_w),
        pl.BlockSpec((1, n_hp, tm // sb_tile, SB_GROUP_LANES, sb_tile), lambda bi, ti: (bi, 0, ti, 0, 0)),
        rows(sb_w), rows(conv_dim), rows(gdn_w), rows(LANES),
        pl.BlockSpec((1, 8, tm), lambda bi, ti: (bi, 0, ti)),
    )
    return pl.pallas_call(
        kern,
        grid=(b, l // tm),
        in_specs=[rows(d), _const_spec((1, d)), _const_spec(wa.shape), _const_spec(wba.shape)],
        out_specs=out_specs,
        out_shape=out_shape,
        compiler_params=_params("parallel", "parallel"),
        name="in_projection",
    )(x, gain, wa, wba)


def _sb_kernel(q_ref, ktd_ref, vd_ref, ktp_ref, vp_ref, o_ref, *, tq, td, nd, tk, heads, dh,
               causal_past, n_past_static):
    qi = pl.program_id(2)

    def from_key_matrix(n):
        r = lax.broadcasted_iota(jnp.int32, (n, n), 0)
        c = lax.broadcasted_iota(jnp.int32, (n, n), 1)
        return (r >= c).astype(BF16)

    def sweep(blocks, u_mat, st):
        pairs = [(blk, hh) for blk in blocks for hh in range(heads)]
        scores = []
        for (kt_blk, _, visible), hh in pairs:
            lanes = slice(hh * dh, (hh + 1) * dh)
            z = _dot(q_all[:, lanes], kt_blk[lanes, :])
            sp = jnp.maximum(z, 0.0) + jnp.log(1.0 + jnp.exp2(jnp.abs(z) * (-LOG2E)))
            nlk = sp if visible is None else jnp.where(visible, sp, 0.0)
            scores.append((z, nlk.astype(BF16), jnp.sum(nlk, axis=1, keepdims=True)))
        csums = [_dot(nlk_b, u_mat) for _, nlk_b, _ in scores]
        st = list(st)
        for ((_, v_blk, visible), hh), (z, _, row_sum), csum in zip(pairs, scores, csums):
            lanes = slice(hh * dh, (hh + 1) * dh)
            carry = st[2 * hh]
            a = jnp.exp(z - csum - carry)
            if visible is not None:
                a = jnp.where(visible, a, 0.0)
            st[2 * hh + 1] = st[2 * hh + 1] + _dot(a.astype(BF16), v_blk[:, lanes])
            st[2 * hh] = carry + row_sum
        return st

    q_all = q_ref[0]
    state = []
    for hh in range(heads):
        state.extend([jnp.zeros((tq, 1), F32), jnp.zeros((tq, dh), F32)])

    u_diag = from_key_matrix(td)
    rows_i = lax.broadcasted_iota(jnp.int32, (tq, td), 0)
    cols_i = lax.broadcasted_iota(jnp.int32, (tq, td), 1)
    diag_blocks = [(ktd_ref[0, 0, sd], vd_ref[0, sd * td:(sd + 1) * td, :], cols_i + sd * td < rows_i)
                   for sd in reversed(range(nd))]
    state = sweep(diag_blocks, u_diag, state)

    u_past = u_diag if tk == td else from_key_matrix(tk)
    n_pairs = qi * (tq // (2 * tk)) if causal_past else n_past_static // 2

    def body(i, st):
        blocks = []
        for sub in (1, 0):
            j = 2 * (n_pairs - 1 - i) + sub
            row0 = pl.multiple_of(j * tk, tk)
            blocks.append((ktp_ref[0, 0, j], vp_ref[0, pl.ds(row0, tk), :], None))
        return tuple(sweep(blocks, u_past, st))

    st = lax.fori_loop(0, n_pairs, body, tuple(state))
    o_ref[0] = jnp.concatenate([st[2 * hh + 1] for hh in range(heads)], axis=1).astype(BF16)


def _sb_attention(q, kt_diag, v_diag, kt_past, v_past, *, tq, td, tk, causal_past):
    b, l, w = q.shape
    n_g, hw = kt_diag.shape[1], kt_diag.shape[3]
    n_past_blocks = kt_past.shape[2]
    assert tq % td == 0 and n_past_blocks % 2 == 0 and (not causal_past or tq % (2 * tk) == 0)
    p = v_past.shape[1]
    dh = w // SB_HEADS
    kern = functools.partial(_sb_kernel, tq=tq, td=td, nd=tq // td, tk=tk, heads=hw // dh, dh=dh,
                             causal_past=causal_past, n_past_static=n_past_blocks)
    return pl.pallas_call(
        kern,
        grid=(b, n_g, l // tq),
        in_specs=[
            pl.BlockSpec((1, tq, hw), lambda bi, hp, qi: (bi, qi, hp)),
            pl.BlockSpec((1, 1, tq // td, hw, td), lambda bi, hp, qi: (bi, hp, qi, 0, 0)),
            pl.BlockSpec((1, tq, hw), lambda bi, hp, qi: (bi, qi, hp)),
            pl.BlockSpec((1, 1, n_past_blocks, hw, tk), lambda bi, hp, qi: (bi, hp, 0, 0, 0)),
            pl.BlockSpec((1, p, hw), lambda bi, hp, qi: (bi, 0, hp)),
        ],
        out_specs=pl.BlockSpec((1, tq, hw), lambda bi, hp, qi: (bi, qi, hp)),
        out_shape=jax.ShapeDtypeStruct((b, l, w), BF16),
        compiler_params=_params("parallel", "parallel", "arbitrary"),
        name="stick_breaking_attention",
    )(q, kt_diag, v_diag, kt_past, v_past)


def _gdn_kernel(u_ref, z_ref, ba_ref, bat_ref, hist_ref, s0_ref, cw_ref,
                alog_l_ref, dt_l_ref, alog_s_ref, dt_s_ref, hg_ref,
                o_ref, s_ref, cnew_ref, ext_ref, *, rows, chunk, width, n_taps):
    step = pl.program_id(1)
    n_steps = pl.num_programs(1)
    dk = width // GDN_HEADS
    n_sub = rows // chunk
    pad = 8
    n_hist = n_taps - 1

    @pl.when(step == 0)
    def _():
        ext_ref[0:pad, :] = jnp.zeros((pad, ext_ref.shape[1]), F32)
        ext_ref[pad - n_hist:pad, :] = hist_ref[0]
        s_ref[...] = s0_ref[...]

    ext_ref[pad:pad + rows, :] = u_ref[0]
    y = ext_ref[pad - n_hist:pad - n_hist + rows, :] * cw_ref[0:1, :]
    for i in range(1, n_taps):
        y = y + ext_ref[pad - n_hist + i:pad - n_hist + i + rows, :] * cw_ref[i:i + 1, :]
    new_tail = ext_ref[rows:rows + pad, :]
    ext_ref[0:pad, :] = new_tail

    @pl.when(step == n_steps - 1)
    def _():
        cnew_ref[0] = new_tail[pad - n_hist:pad, :]

    qkv = y * jax.nn.sigmoid(y)

    r_i = lax.broadcasted_iota(jnp.int32, (chunk, chunk), 0)
    c_i = lax.broadcasted_iota(jnp.int32, (chunk, chunk), 1)
    incl = r_i >= c_i
    strict = r_i > c_i
    eye = (r_i == c_i).astype(F32)
    rr = lax.broadcasted_iota(jnp.int32, (rows, rows), 0)
    cc = lax.broadcasted_iota(jnp.int32, (rows, rows), 1)
    same_chunk = (rr // chunk) == (cc // chunk)
    lower_incl = (same_chunk & (rr >= cc)).astype(BF16)
    upper_incl = (same_chunk & (rr <= cc)).astype(BF16)

    def softplus(t):
        return jnp.maximum(t, 0.0) + jnp.log(1.0 + jnp.exp(-jnp.abs(t)))

    ba = ba_ref[0]
    g_cols = -jnp.exp(alog_l_ref[...]) * softplus(ba + dt_l_ref[...])
    gcum_cols = _dot_exactlhs_f32(lower_incl, g_cols)
    beta_cols = jax.nn.sigmoid(ba)
    bat = bat_ref[0]
    g_rows = -jnp.exp(alog_s_ref[...]) * softplus(bat + dt_s_ref[...])
    gcum_rows = _dot_f32_exactrhs(g_rows, upper_incl)

    heads = range(GDN_HEADS)
    pairs = [(ci, h) for ci in range(n_sub) for h in heads]

    qs, ks, vs = [], [], []
    for h in heads:
        q = qkv[:, h * dk:(h + 1) * dk]
        k = qkv[:, width + h * dk:width + (h + 1) * dk]
        qs.append(q * lax.rsqrt(jnp.sum(q * q, axis=-1, keepdims=True) + NORM_EPS) * float(dk ** -0.5))
        ks.append(k * lax.rsqrt(jnp.sum(k * k, axis=-1, keepdims=True) + NORM_EPS))
        vs.append(qkv[:, 2 * width + h * dk:2 * width + (h + 1) * dk])

    pre = {}
    for ci, h in pairs:
        rs = slice(ci * chunk, (ci + 1) * chunk)
        gc_col = gcum_cols[rs, GDN_HEADS + h:GDN_HEADS + h + 1]
        gc_row = gcum_rows[GDN_HEADS + h:GDN_HEADS + h + 1, rs]
        gc_last = gc_row[:, chunk - 1:chunk]
        beta = beta_cols[rs, h:h + 1]
        gamma = jnp.where(incl, jnp.exp(gc_col - gc_row), 0.0)
        decay_in = jnp.exp(gc_col)
        q, k, v = qs[h][rs], ks[h][rs], vs[h][rs]
        kb = k * beta
        k_bf = k.astype(BF16)
        n_mat = jnp.where(strict, -(_dot_nt(kb.astype(BF16), k_bf) * gamma), 0.0)
        pre[ci, h] = dict(
            gamma=gamma, n=n_mat, k_bf=k_bf, q_bf=q.astype(BF16),
            rhs=jnp.concatenate([v * beta, kb * decay_in], axis=1),
            q_dec=(q * decay_in).astype(BF16),
            k_end=(k * jnp.exp(gc_last - gc_col)).astype(BF16),
            chunk_decay=jnp.exp(gc_last))

    n_pow = {p: pre[p]["n"] for p in pairs}
    t_inv = {p: eye + pre[p]["n"] for p in pairs}
    for _ in range(max(1, (chunk - 1).bit_length()) - 1):
        for p in pairs:
            n_bf = n_pow[p].astype(BF16)
            n_pow[p] = _dot(n_bf, n_bf)
        for p in pairs:
            t_inv[p] = t_inv[p] + _dot(t_inv[p].astype(BF16), n_pow[p].astype(BF16))

    sol = {p: _dot_x3(t_inv[p], pre[p]["rhs"]) for p in pairs}
    qk = {p: (_dot_nt(pre[p]["q_bf"], pre[p]["k_bf"]) * pre[p]["gamma"]).astype(BF16) for p in pairs}

    state = [s_ref[0, h] for h in heads]
    o_rows = [[None] * n_sub for _ in heads]
    for ci in range(n_sub):
        s_bf = [state[h].astype(BF16) for h in heads]
        v_new = [(sol[ci, h][:, :dk] - _dot(sol[ci, h][:, dk:].astype(BF16), s_bf[h])).astype(BF16)
                 for h in heads]
        for h in heads:
            o_rows[h][ci] = _dot(pre[ci, h]["q_dec"], s_bf[h]) + _dot(qk[ci, h], v_new[h])
        state = [state[h] * pre[ci, h]["chunk_decay"] + _dot_tn(pre[ci, h]["k_end"], v_new[h])
                 for h in heads]
    for h in heads:
        s_ref[0, h] = state[h]

    z = z_ref[0]
    outs = []
    for h in heads:
        o_h = o_rows[h][0] if n_sub == 1 else jnp.concatenate(o_rows[h], axis=0)
        z_h = z[:, h * dk:(h + 1) * dk]
        outs.append(_rms(o_h, hg_ref[...]) * (z_h * jax.nn.sigmoid(z_h)))
    o_ref[0] = jnp.concatenate(outs, axis=1).astype(BF16)


def _gdn(u, z, ba, bat, hist, s0, conv_w, alog_l, dt_l, alog_s, dt_s, head_gain, *, chunk):
    b, l, conv_dim = u.shape
    width = conv_dim // 3
    dk = width // GDN_HEADS
    n_taps = conv_w.shape[0]
    rows = min(GDN_ROWS, l)
    kern = functools.partial(_gdn_kernel, rows=rows, chunk=chunk, width=width, n_taps=n_taps)

    def row_block(wd):
        return pl.BlockSpec((1, rows, wd), lambda bi, si: (bi, si, 0))

    state_spec = pl.BlockSpec((1, GDN_HEADS, dk, dk), lambda bi, si: (bi, 0, 0, 0))
    hist_spec = pl.BlockSpec((1, n_taps - 1, conv_dim), lambda bi, si: (bi, 0, 0))
    return pl.pallas_call(
        kern,
        grid=(b, l // rows),
        in_specs=[row_block(conv_dim), row_block(width), row_block(LANES),
                  pl.BlockSpec((1, 8, rows), lambda bi, si: (bi, 0, si)),
                  hist_spec, state_spec, _const_spec(conv_w.shape),
                  _const_spec((1, LANES)), _const_spec((1, LANES)),
                  _const_spec((8, 1)), _const_spec((8, 1)), _const_spec((1, dk))],
        out_specs=(row_block(width), state_spec, hist_spec),
        out_shape=(jax.ShapeDtypeStruct((b, l, width), BF16),
                   jax.ShapeDtypeStruct(s0.shape, F32),
                   jax.ShapeDtypeStruct(hist.shape, F32)),
        scratch_shapes=[pltpu.VMEM((rows + 8, conv_dim), F32)],
        compiler_params=_params("parallel", "arbitrary"),
        name="gated_delta_rule",
    )(u, z, ba, bat, hist, s0, conv_w, alog_l, dt_l, alog_s, dt_s, head_gain)


def _merge_kernel(x_ref, osb_ref, ogdn_ref, gin_ref, gout_ref, wgate_ref, wsb_ref, wgdn_ref, wout_ref, o_ref):
    x = x_ref[...]
    d = x.shape[1]
    h = _rms(x, gin_ref[...]).astype(BF16)
    gate_sb = jax.nn.sigmoid(_dot(h, wgate_ref[:, :d]))
    merged = gate_sb * _dot(osb_ref[...], wsb_ref[...])
    gate_gdn = jax.nn.sigmoid(_dot(h, wgate_ref[:, d:]))
    merged = merged + gate_gdn * _dot(ogdn_ref[...], wgdn_ref[...])
    m = _dot(merged.astype(BF16), wout_ref[...])
    o_ref[...] = x + _rms(m, gout_ref[...])


def _merge(x2, osb2, ogdn2, g_in, g_out, wgate, wsb, wgdn, wout):
    n, d = x2.shape
    tm = min(ROW_TILE, n)

    def rows(wd):
        return pl.BlockSpec((tm, wd), lambda i: (i, 0))

    return pl.pallas_call(
        _merge_kernel,
        grid=(n // tm,),
        in_specs=[rows(d), rows(osb2.shape[1]), rows(ogdn2.shape[1]),
                  _const_spec((1, d)), _const_spec((1, d)),
                  _const_spec(wgate.shape), _const_spec(wsb.shape), _const_spec(wgdn.shape),
                  _const_spec(wout.shape)],
        out_specs=rows(d),
        out_shape=jax.ShapeDtypeStruct((n, d), F32),
        compiler_params=_params("parallel"),
        name="branch_merge",
    )(x2, osb2, ogdn2, g_in, g_out, wgate, wsb, wgdn, wout)


def _layer_weights(l, norm_gains, w_ffn1_up, w_ffn1_down, w_in, conv_w, gdn_a_log, gdn_dt_bias,
                   gdn_norm_gain, w_branch_sb, w_branch_gdn, w_out, w_ffn2_up, w_ffn2_down):
    d = w_in.shape[1]
    d_ff = w_ffn1_down.shape[1]
    sb_w = w_branch_sb.shape[1]
    gdn_w = w_branch_gdn.shape[1]
    conv_dim = conv_w.shape[2]
    n_a = 3 * sb_w + conv_dim + gdn_w
    wi = w_in[l]
    wba = jnp.zeros((d, LANES), F32).at[:, :2 * GDN_HEADS].set(wi[:, n_a:n_a + 2 * GDN_HEADS])

    def lane_vec(p):
        return jnp.zeros((1, LANES), F32).at[0, GDN_HEADS:2 * GDN_HEADS].set(p)

    def sublane_vec(p):
        return jnp.zeros((8, 1), F32).at[GDN_HEADS:2 * GDN_HEADS, 0].set(p)

    return dict(
        gains=[norm_gains[l, i][None, :] for i in range(6)],
        ffn1=(w_ffn1_up[l][:, :d_ff].astype(BF16), w_ffn1_up[l][:, d_ff:].astype(BF16),
              w_ffn1_down[l].astype(BF16)),
        ffn2=(w_ffn2_up[l][:, :d_ff].astype(BF16), w_ffn2_up[l][:, d_ff:].astype(BF16),
              w_ffn2_down[l].astype(BF16)),
        wa=wi[:, :n_a].astype(BF16),
        wba=wba.astype(BF16),
        wgate=wi[:, n_a + 2 * GDN_HEADS:].astype(BF16),
        wsb=w_branch_sb[l].astype(BF16), wgdn=w_branch_gdn[l].astype(BF16), wout=w_out[l].astype(BF16),
        conv_w=conv_w[l],
        alog_l=lane_vec(gdn_a_log[l]), dt_l=lane_vec(gdn_dt_bias[l]),
        alog_s=sublane_vec(gdn_a_log[l]), dt_s=sublane_vec(gdn_dt_bias[l]),
        head_gain=gdn_norm_gain[l][None, :],
        dims=(sb_w, conv_dim, gdn_w),
    )


def _run_group(x, weights, past_k, past_v, conv_hist, s0, chunk):
    b, l, d = x.shape
    n = b * l
    k_list, v_list, s_list, c_list = [], [], [], []
    for li, w in enumerate(weights):
        sb_w, conv_dim, gdn_w = w["dims"]
        dh = sb_w // SB_HEADS
        tq = min(SB_QUERY_TILE, l)
        td = min(SB_KEY_TILE, l)
        g = w["gains"]
        x2 = _ffn(x.reshape(n, d), g[0], g[1], *w["ffn1"])
        q, k_rows, v_rows, kt, vb, u, z, ba, bat = _inproj(
            x2.reshape(b, l, d), g[2], w["wa"], w["wba"],
            sb_w=sb_w, conv_dim=conv_dim, gdn_w=gdn_w, sb_tile=td)
        if past_k is None:
            o_sb = _sb_attention(q, kt, vb, kt, vb, tq=tq, td=td, tk=td, causal_past=True)
        else:
            p = past_k.shape[2]
            tk = min(SB_KEY_TILE, p)
            ktp = past_k[li].astype(BF16).reshape(b, p // tk, tk, sb_w // SB_GROUP_LANES, SB_GROUP_LANES)
            ktp = ktp.transpose(0, 3, 1, 4, 2)
            vp = past_v[li].astype(BF16).reshape(b, p, sb_w)
            o_sb = _sb_attention(q, kt, vb, ktp, vp, tq=tq, td=td, tk=tk, causal_past=False)
        o_gdn, s_fin, conv_new = _gdn(u, z, ba, bat, conv_hist[li], s0[li], w["conv_w"],
                                      w["alog_l"], w["dt_l"], w["alog_s"], w["dt_s"], w["head_gain"],
                                      chunk=chunk)
        x2 = _merge(x2, o_sb.reshape(n, sb_w), o_gdn.reshape(n, gdn_w), g[2], g[3],
                    w["wgate"], w["wsb"], w["wgdn"], w["wout"])
        x2 = _ffn(x2, g[4], g[5], *w["ffn2"])
        x = x2.reshape(b, l, d)
        k_list.append(k_rows.reshape(b, l, SB_HEADS, dh))
        v_list.append(v_rows.reshape(b, l, SB_HEADS, dh))
        s_list.append(s_fin)
        c_list.append(conv_new)
    return x, jnp.stack(k_list), jnp.stack(v_list), jnp.stack(s_list), jnp.stack(c_list)


def kernel(x_prompt, x_sample, cache_sb_k, cache_sb_v, state_gdn, state_conv, norm_gains,
           w_ffn1_up, w_ffn1_down, w_in, conv_w, gdn_a_log, gdn_dt_bias, gdn_norm_gain,
           w_branch_sb, w_branch_gdn, w_out, w_ffn2_up, w_ffn2_down):
    depth = w_in.shape[0]
    weights = [_layer_weights(l, norm_gains, w_ffn1_up, w_ffn1_down, w_in, conv_w, gdn_a_log,
                              gdn_dt_bias, gdn_norm_gain, w_branch_sb, w_branch_gdn, w_out,
                              w_ffn2_up, w_ffn2_down) for l in range(depth)]
    bp = x_prompt.shape[0]
    chunk_prompt = 64
    zero_conv = jnp.zeros((depth, bp) + state_conv.shape[2:], state_conv.dtype)
    zero_state = jnp.zeros((depth, bp) + state_gdn.shape[2:], state_gdn.dtype)
    y_p, pk, pv, ps, pc = _run_group(x_prompt, weights, None, None, zero_conv, zero_state, chunk_prompt)
    y_s, sk, sv, ss, sc = _run_group(x_sample, weights, cache_sb_k, cache_sb_v, state_conv, state_gdn,
                                     x_sample.shape[1])
    return (y_p, y_s, pk, pv, ps, pc, sk, sv, ss, sc)
```

```python
import functools

import jax
import jax.numpy as jnp
from jax import lax
from jax.experimental import pallas as pl
from jax.experimental.pallas import tpu as pltpu

F32 = jnp.float32
BF16 = jnp.bfloat16

NORM_EPS = 1e-6
LOG2E = 1.4426950408889634
SB_HEADS = 8
GDN_HEADS = 4
LANES = 128
VMEM_LIMIT_BYTES = 56 * 1024 * 1024

ROW_TILE = 512
FF_CHUNK = 256
GDN_ROWS = 256
SB_KEY_TILE = 256
SB_QUERY_TILE = 256
SB_GROUP_LANES = 512
SB_ZERO_WEIGHT_LOG = 110.0


def _params(*sem):
    return pltpu.CompilerParams(dimension_semantics=sem, vmem_limit_bytes=VMEM_LIMIT_BYTES)


def _const_spec(shape):
    zeros = (0,) * len(shape)
    return pl.BlockSpec(shape, lambda *_: zeros)


def _rms(x, gain):
    ms = jnp.mean(x * x, axis=-1, keepdims=True)
    return x * lax.rsqrt(ms + NORM_EPS) * gain


def _dot(a, b):
    return jnp.dot(a, b, preferred_element_type=F32)


def _dot_nt(a, b):
    return lax.dot_general(a, b, (((1,), (1,)), ((), ())), preferred_element_type=F32)


def _dot_tn(a, b):
    return lax.dot_general(a, b, (((0,), (0,)), ((), ())), preferred_element_type=F32)


def _split2(x):
    hi = x.astype(BF16)
    lo = (x - hi.astype(F32)).astype(BF16)
    return hi, lo


def _split3(x):
    hi = x.astype(BF16)
    r = x - hi.astype(F32)
    mid = r.astype(BF16)
    lo = (r - mid.astype(F32)).astype(BF16)
    return hi, mid, lo


def _dot_f32_exactrhs(a, b_bf16):
    hi, mid, lo = _split3(a)
    return _dot(hi, b_bf16) + _dot(mid, b_bf16) + _dot(lo, b_bf16)


def _dot_exactlhs_f32(a_bf16, b):
    hi, mid, lo = _split3(b)
    return _dot(a_bf16, hi) + _dot(a_bf16, mid) + _dot(a_bf16, lo)


def _dot_x3(a, b):
    ah, al = _split2(a)
    bh, bl = _split2(b)
    return _dot(ah, bh) + _dot(ah, bl) + _dot(al, bh)


def _ffn_kernel(x_ref, gin_ref, gout_ref, wg_ref, wu_ref, wd_ref, o_ref, act_ref):
    x = x_ref[...]
    h = _rms(x, gin_ref[...]).astype(BF16)
    d_ff = wd_ref.shape[0]
    for c in range(d_ff // FF_CHUNK):
        sl = slice(c * FF_CHUNK, (c + 1) * FF_CHUNK)
        g = _dot(h, wg_ref[:, sl])
        u = _dot(h, wu_ref[:, sl])
        act_ref[:, sl] = (g * jax.nn.sigmoid(g) * u).astype(BF16)
    y = _dot(act_ref[...], wd_ref[...])
    o_ref[...] = x + 0.5 * _rms(y, gout_ref[...])


def _ffn(x2, g_in, g_out, wg, wu, wd):
    n, d = x2.shape
    d_ff = wd.shape[0]
    tm = min(ROW_TILE, n)
    row = pl.BlockSpec((tm, d), lambda i: (i, 0))
    return pl.pallas_call(
        _ffn_kernel,
        grid=(n // tm,),
        in_specs=[row, _const_spec((1, d)), _const_spec((1, d)),
                  _const_spec((d, d_ff)), _const_spec((d, d_ff)), _const_spec((d_ff, d))],
        out_specs=row,
        out_shape=jax.ShapeDtypeStruct((n, d), F32),
        scratch_shapes=[pltpu.VMEM((tm, d_ff), BF16)],
        compiler_params=_params("parallel"),
        name="ffn_half_step",
    )(x2, g_in, g_out, wg, wu, wd)


def _inproj_kernel(x_ref, g_ref, wa_ref, wba_ref,
                   q_ref, k_ref, v_ref, kt_ref, vb_ref, u_ref, z_ref, ba_ref, bat_ref,
                   *, sb_w, conv_dim, gdn_w, sb_tile, q_scale):
    h = _rms(x_ref[0], g_ref[...]).astype(BF16)
    tm = h.shape[0]

    def proj(lo, width):
        return _dot(h, wa_ref[:, lo:lo + width])

    q_ref[0] = (proj(0, sb_w) * q_scale).astype(BF16)
    k = proj(sb_w, sb_w)
    k_ref[0] = k
    kt = k.T.astype(BF16)
    for hp in range(sb_w // SB_GROUP_LANES):
        for s in range(tm // sb_tile):
            kt_ref[0, hp, s] = kt[hp * SB_GROUP_LANES:(hp + 1) * SB_GROUP_LANES,
                                  s * sb_tile:(s + 1) * sb_tile]
    v = proj(2 * sb_w, sb_w)
    v_ref[0] = v
    vb_ref[0] = v.astype(BF16)
    u_ref[0] = proj(3 * sb_w, conv_dim)
    z_ref[0] = proj(3 * sb_w + conv_dim, gdn_w)
    ba = _dot(h, wba_ref[...])
    ba_ref[0] = ba
    bat_ref[0] = ba.T[:8]


def _inproj(x, gain, wa, wba, *, sb_w, conv_dim, gdn_w, sb_tile):
    b, l, d = x.shape
    tm = min(ROW_TILE, l)
    n_hp = sb_w // SB_GROUP_LANES
    kern = functools.partial(_inproj_kernel, sb_w=sb_w, conv_dim=conv_dim, gdn_w=gdn_w,
                             sb_tile=sb_tile,
                             q_scale=float((sb_w // SB_HEADS) ** -0.5))

    def rows(width):
        return pl.BlockSpec((1, tm, width), lambda bi, ti: (bi, ti, 0))

    out_shape = (
        jax.ShapeDtypeStruct((b, l, sb_w), BF16),
        jax.ShapeDtypeStruct((b, l, sb_w), F32),
        jax.ShapeDtypeStruct((b, l, sb_w), F32),
        jax.ShapeDtypeStruct((b, n_hp, l // sb_tile, SB_GROUP_LANES, sb_tile), BF16),
        jax.ShapeDtypeStruct((b, l, sb_w), BF16),
        jax.ShapeDtypeStruct((b, l, conv_dim), F32),
        jax.ShapeDtypeStruct((b, l, gdn_w), F32),
        jax.ShapeDtypeStruct((b, l, LANES), F32),
        jax.ShapeDtypeStruct((b, 8, l), F32),
    )
    out_specs = (
        rows(sb_w), rows(sb_w), rows(sb_w),
        pl.BlockSpec((1, n_hp, tm // sb_tile, SB_GROUP_LANES, sb_tile), lambda bi, ti: (bi, 0, ti, 0, 0)),
        rows(sb_w), rows(conv_dim), rows(gdn_w), rows(LANES),
        pl.BlockSpec((1, 8, tm), lambda bi, ti: (bi, 0, ti)),
    )
    return pl.pallas_call(
        kern,
        grid=(b, l // tm),
        in_specs=[rows(d), _const_spec((1, d)), _const_spec(wa.shape), _const_spec(wba.shape)],
        out_specs=out_specs,
        out_shape=out_shape,
        compiler_params=_params("parallel", "parallel"),
        name="in_projection",
    )(x, gain, wa, wba)


def _sb_kernel(q_ref, ktd_ref, vd_ref, ktp_ref, vp_ref, o_ref, *, tq, td, nd, tk, heads, dh,
               causal_past, n_past_static):
    qi = pl.program_id(2)

    def from_key_matrix(n):
        r = lax.broadcasted_iota(jnp.int32, (n, n), 0)
        c = lax.broadcasted_iota(jnp.int32, (n, n), 1)
        return (r >= c).astype(BF16)

    def sweep(blocks, u_mat, st):
        pairs = [(blk, hh) for blk in blocks for hh in range(heads)]
        scores = []
        for (kt_blk, _, visible), hh in pairs:
            lanes = slice(hh * dh, (hh + 1) * dh)
            z = _dot(q_all[:, lanes], kt_blk[lanes, :])
            sp = jnp.maximum(z, 0.0) + jnp.log(1.0 + jnp.exp2(jnp.abs(z) * (-LOG2E)))
            nlk = sp if visible is None else jnp.where(visible, sp, 0.0)
            scores.append((z, nlk.astype(BF16), jnp.sum(nlk, axis=1, keepdims=True)))
        csums = [_dot(nlk_b, u_mat) for _, nlk_b, _ in scores]
        st = list(st)
        for ((_, v_blk, visible), hh), (z, _, row_sum), csum in zip(pairs, scores, csums):
            lanes = slice(hh * dh, (hh + 1) * dh)
            carry = st[2 * hh]
            a = jnp.exp(z - csum - carry)
            if visible is not None:
                a = jnp.where(visible, a, 0.0)
            st[2 * hh + 1] = st[2 * hh + 1] + _dot(a.astype(BF16), v_blk[:, lanes])
            st[2 * hh] = carry + row_sum
        return st

    q_all = q_ref[0]
    state = []
    for hh in range(heads):
        state.extend([jnp.zeros((tq, 1), F32), jnp.zeros((tq, dh), F32)])

    u_diag = from_key_matrix(td)
    rows_i = lax.broadcasted_iota(jnp.int32, (tq, td), 0)
    cols_i = lax.broadcasted_iota(jnp.int32, (tq, td), 1)
    diag_blocks = [(ktd_ref[0, 0, sd], vd_ref[0, sd * td:(sd + 1) * td, :], cols_i + sd * td < rows_i)
                   for sd in reversed(range(nd))]
    state = sweep(diag_blocks, u_diag, state)

    u_past = u_diag if tk == td else from_key_matrix(tk)
    n_past = qi * (tq // tk) if causal_past else n_past_static

    def min_carry(st):
        m = st[0]
        for hh in range(1, heads):
            m = jnp.minimum(m, st[2 * hh])
        return jnp.min(m)

    def cond(loop):
        i, smallest, _ = loop
        return jnp.logical_and(i < n_past, smallest < SB_ZERO_WEIGHT_LOG)

    def body(loop):
        i, _, st = loop
        j = n_past - 1 - i
        row0 = pl.multiple_of(j * tk, tk)
        st = sweep([(ktp_ref[0, 0, j], vp_ref[0, pl.ds(row0, tk), :], None)], u_past, st)
        return i + 1, min_carry(st), tuple(st)

    _, _, st = lax.while_loop(cond, body, (jnp.int32(0), min_carry(state), tuple(state)))
    o_ref[0] = jnp.concatenate([st[2 * hh + 1] for hh in range(heads)], axis=1).astype(BF16)


def _sb_attention(q, kt_diag, v_diag, kt_past, v_past, *, tq, td, tk, causal_past):
    b, l, w = q.shape
    n_g, hw = kt_diag.shape[1], kt_diag.shape[3]
    n_past_blocks = kt_past.shape[2]
    assert tq % td == 0 and (not causal_past or tq % tk == 0)
    p = v_past.shape[1]
    dh = w // SB_HEADS
    kern = functools.partial(_sb_kernel, tq=tq, td=td, nd=tq // td, tk=tk, heads=hw // dh, dh=dh,
                             causal_past=causal_past, n_past_static=n_past_blocks)
    return pl.pallas_call(
        kern,
        grid=(b, n_g, l // tq),
        in_specs=[
            pl.BlockSpec((1, tq, hw), lambda bi, hp, qi: (bi, qi, hp)),
            pl.BlockSpec((1, 1, tq // td, hw, td), lambda bi, hp, qi: (bi, hp, qi, 0, 0)),
            pl.BlockSpec((1, tq, hw), lambda bi, hp, qi: (bi, qi, hp)),
            pl.BlockSpec((1, 1, n_past_blocks, hw, tk), lambda bi, hp, qi: (bi, hp, 0, 0, 0),
                         pipeline_mode=pl.Buffered(1)),
            pl.BlockSpec((1, p, hw), lambda bi, hp, qi: (bi, 0, hp), pipeline_mode=pl.Buffered(1)),
        ],
        out_specs=pl.BlockSpec((1, tq, hw), lambda bi, hp, qi: (bi, qi, hp)),
        out_shape=jax.ShapeDtypeStruct((b, l, w), BF16),
        compiler_params=_params("parallel", "parallel", "arbitrary"),
        name="stick_breaking_attention",
    )(q, kt_diag, v_diag, kt_past, v_past)


def _gdn_kernel(u_ref, z_ref, ba_ref, bat_ref, hist_ref, s0_ref, cw_ref,
                alog_l_ref, dt_l_ref, alog_s_ref, dt_s_ref, hg_ref,
                o_ref, s_ref, cnew_ref, ext_ref, *, rows, chunk, width, n_taps):
    step = pl.program_id(1)
    n_steps = pl.num_programs(1)
    dk = width // GDN_HEADS
    n_sub = rows // chunk
    pad = 8
    n_hist = n_taps - 1

    @pl.when(step == 0)
    def _():
        ext_ref[0:pad, :] = jnp.zeros((pad, ext_ref.shape[1]), F32)
        ext_ref[pad - n_hist:pad, :] = hist_ref[0]
        s_ref[...] = s0_ref[...]

    ext_ref[pad:pad + rows, :] = u_ref[0]
    y = ext_ref[pad - n_hist:pad - n_hist + rows, :] * cw_ref[0:1, :]
    for i in range(1, n_taps):
        y = y + ext_ref[pad - n_hist + i:pad - n_hist + i + rows, :] * cw_ref[i:i + 1, :]
    new_tail = ext_ref[rows:rows + pad, :]
    ext_ref[0:pad, :] = new_tail

    @pl.when(step == n_steps - 1)
    def _():
        cnew_ref[0] = new_tail[pad - n_hist:pad, :]

    qkv = y * jax.nn.sigmoid(y)

    r_i = lax.broadcasted_iota(jnp.int32, (chunk, chunk), 0)
    c_i = lax.broadcasted_iota(jnp.int32, (chunk, chunk), 1)
    incl = r_i >= c_i
    strict = r_i > c_i
    eye = (r_i == c_i).astype(F32)
    rr = lax.broadcasted_iota(jnp.int32, (rows, rows), 0)
    cc = lax.broadcasted_iota(jnp.int32, (rows, rows), 1)
    same_chunk = (rr // chunk) == (cc // chunk)
    lower_incl = (same_chunk & (rr >= cc)).astype(BF16)
    upper_incl = (same_chunk & (rr <= cc)).astype(BF16)

    def softplus(t):
        return jnp.maximum(t, 0.0) + jnp.log(1.0 + jnp.exp(-jnp.abs(t)))

    ba = ba_ref[0]
    g_cols = -jnp.exp(alog_l_ref[...]) * softplus(ba + dt_l_ref[...])
    gcum_cols = _dot_exactlhs_f32(lower_incl, g_cols)
    beta_cols = jax.nn.sigmoid(ba)
    bat = bat_ref[0]
    g_rows = -jnp.exp(alog_s_ref[...]) * softplus(bat + dt_s_ref[...])
    gcum_rows = _dot_f32_exactrhs(g_rows, upper_incl)

    heads = range(GDN_HEADS)
    pairs = [(ci, h) for ci in range(n_sub) for h in heads]

    qs, ks, vs = [], [], []
    for h in heads:
        q = qkv[:, h * dk:(h + 1) * dk]
        k = qkv[:, width + h * dk:width + (h + 1) * dk]
        qs.append(q * lax.rsqrt(jnp.sum(q * q, axis=-1, keepdims=True) + NORM_EPS) * float(dk ** -0.5))
        ks.append(k * lax.rsqrt(jnp.sum(k * k, axis=-1, keepdims=True) + NORM_EPS))
        vs.append(qkv[:, 2 * width + h * dk:2 * width + (h + 1) * dk])

    pre = {}
    for ci, h in pairs:
        rs = slice(ci * chunk, (ci + 1) * chunk)
        gc_col = gcum_cols[rs, GDN_HEADS + h:GDN_HEADS + h + 1]
        gc_row = gcum_rows[GDN_HEADS + h:GDN_HEADS + h + 1, rs]
        gc_last = gc_row[:, chunk - 1:chunk]
        beta = beta_cols[rs, h:h + 1]
        gamma = jnp.where(incl, jnp.exp(gc_col - gc_row), 0.0)
        decay_in = jnp.exp(gc_col)
        q, k, v = qs[h][rs], ks[h][rs], vs[h][rs]
        kb = k * beta
        k_bf = k.astype(BF16)
        n_mat = jnp.where(strict, -(_dot_nt(kb.astype(BF16), k_bf) * gamma), 0.0)
        pre[ci, h] = dict(
            gamma=gamma, n=n_mat, k_bf=k_bf, q_bf=q.astype(BF16),
            rhs=jnp.concatenate([v * beta, kb * decay_in], axis=1),
            q_dec=(q * decay_in).astype(BF16),
            k_end=(k * jnp.exp(gc_last - gc_col)).astype(BF16),
            chunk_decay=jnp.exp(gc_last))

    n_pow = {p: pre[p]["n"] for p in pairs}
    t_inv = {p: eye + pre[p]["n"] for p in pairs}
    for _ in range(max(1, (chunk - 1).bit_length()) - 1):
        for p in pairs:
            n_bf = n_pow[p].astype(BF16)
            n_pow[p] = _dot(n_bf, n_bf)
        for p in pairs:
            t_inv[p] = t_inv[p] + _dot(t_inv[p].astype(BF16), n_pow[p].astype(BF16))

    sol = {p: _dot_x3(t_inv[p], pre[p]["rhs"]) for p in pairs}
    qk = {p: (_dot_nt(pre[p]["q_bf"], pre[p]["k_bf"]) * pre[p]["gamma"]).astype(BF16) for p in pairs}

    state = [s_ref[0, h] for h in heads]
    o_rows = [[None] * n_sub for _ in heads]
    for ci in range(n_sub):
        s_bf = [state[h].astype(BF16) for h in heads]
        v_new = [(sol[ci, h][:, :dk] - _dot(sol[ci, h][:, dk:].astype(BF16), s_bf[h])).astype(BF16)
                 for h in heads]
        for h in heads:
            o_rows[h][ci] = _dot(pre[ci, h]["q_dec"], s_bf[h]) + _dot(qk[ci, h], v_new[h])
        state = [state[h] * pre[ci, h]["chunk_decay"] + _dot_tn(pre[ci, h]["k_end"], v_new[h])
                 for h in heads]
    for h in heads:
        s_ref[0, h] = state[h]

    z = z_ref[0]
    outs = []
    for h in heads:
        o_h = o_rows[h][0] if n_sub == 1 else jnp.concatenate(o_rows[h], axis=0)
        z_h = z[:, h * dk:(h + 1) * dk]
        outs.append(_rms(o_h, hg_ref[...]) * (z_h * jax.nn.sigmoid(z_h)))
    o_ref[0] = jnp.concatenate(outs, axis=1).astype(BF16)


def _gdn(u, z, ba, bat, hist, s0, conv_w, alog_l, dt_l, alog_s, dt_s, head_gain, *, chunk):
    b, l, conv_dim = u.shape
    width = conv_dim // 3
    dk = width // GDN_HEADS
    n_taps = conv_w.shape[0]
    rows = min(GDN_ROWS, l)
    kern = functools.partial(_gdn_kernel, rows=rows, chunk=chunk, width=width, n_taps=n_taps)

    def row_block(wd):
        return pl.BlockSpec((1, rows, wd), lambda bi, si: (bi, si, 0))

    state_spec = pl.BlockSpec((1, GDN_HEADS, dk, dk), lambda bi, si: (bi, 0, 0, 0))
    hist_spec = pl.BlockSpec((1, n_taps - 1, conv_dim), lambda bi, si: (bi, 0, 0))
    return pl.pallas_call(
        kern,
        grid=(b, l // rows),
        in_specs=[row_block(conv_dim), row_block(width), row_block(LANES),
                  pl.BlockSpec((1, 8, rows), lambda bi, si: (bi, 0, si)),
                  hist_spec, state_spec, _const_spec(conv_w.shape),
                  _const_spec((1, LANES)), _const_spec((1, LANES)),
                  _const_spec((8, 1)), _const_spec((8, 1)), _const_spec((1, dk))],
        out_specs=(row_block(width), state_spec, hist_spec),
        out_shape=(jax.ShapeDtypeStruct((b, l, width), BF16),
                   jax.ShapeDtypeStruct(s0.shape, F32),
                   jax.ShapeDtypeStruct(hist.shape, F32)),
        scratch_shapes=[pltpu.VMEM((rows + 8, conv_dim), F32)],
        compiler_params=_params("parallel", "arbitrary"),
        name="gated_delta_rule",
    )(u, z, ba, bat, hist, s0, conv_w, alog_l, dt_l, alog_s, dt_s, head_gain)


def _merge_kernel(x_ref, osb_ref, ogdn_ref, gin_ref, gout_ref, wgate_ref, wsb_ref, wgdn_ref, wout_ref, o_ref):
    x = x_ref[...]
    d = x.shape[1]
    h = _rms(x, gin_ref[...]).astype(BF16)
    gate_sb = jax.nn.sigmoid(_dot(h, wgate_ref[:, :d]))
    merged = gate_sb * _dot(osb_ref[...], wsb_ref[...])
    gate_gdn = jax.nn.sigmoid(_dot(h, wgate_ref[:, d:]))
    merged = merged + gate_gdn * _dot(ogdn_ref[...], wgdn_ref[...])
    m = _dot(merged.astype(BF16), wout_ref[...])
    o_ref[...] = x + _rms(m, gout_ref[...])


def _merge(x2, osb2, ogdn2, g_in, g_out, wgate, wsb, wgdn, wout):
    n, d = x2.shape
    tm = min(ROW_TILE, n)

    def rows(wd):
        return pl.BlockSpec((tm, wd), lambda i: (i, 0))

    return pl.pallas_call(
        _merge_kernel,
        grid=(n // tm,),
        in_specs=[rows(d), rows(osb2.shape[1]), rows(ogdn2.shape[1]),
                  _const_spec((1, d)), _const_spec((1, d)),
                  _const_spec(wgate.shape), _const_spec(wsb.shape), _const_spec(wgdn.shape),
                  _const_spec(wout.shape)],
        out_specs=rows(d),
        out_shape=jax.ShapeDtypeStruct((n, d), F32),
        compiler_params=_params("parallel"),
        name="branch_merge",
    )(x2, osb2, ogdn2, g_in, g_out, wgate, wsb, wgdn, wout)


def _layer_weights(l, norm_gains, w_ffn1_up, w_ffn1_down, w_in, conv_w, gdn_a_log, gdn_dt_bias,
                   gdn_norm_gain, w_branch_sb, w_branch_gdn, w_out, w_ffn2_up, w_ffn2_down):
    d = w_in.shape[1]
    d_ff = w_ffn1_down.shape[1]
    sb_w = w_branch_sb.shape[1]
    gdn_w = w_branch_gdn.shape[1]
    conv_dim = conv_w.shape[2]
    n_a = 3 * sb_w + conv_dim + gdn_w
    wi = w_in[l]
    wba = jnp.zeros((d, LANES), F32).at[:, :2 * GDN_HEADS].set(wi[:, n_a:n_a + 2 * GDN_HEADS])

    def lane_vec(p):
        return jnp.zeros((1, LANES), F32).at[0, GDN_HEADS:2 * GDN_HEADS].set(p)

    def sublane_vec(p):
        return jnp.zeros((8, 1), F32).at[GDN_HEADS:2 * GDN_HEADS, 0].set(p)

    return dict(
        gains=[norm_gains[l, i][None, :] for i in range(6)],
        ffn1=(w_ffn1_up[l][:, :d_ff].astype(BF16), w_ffn1_up[l][:, d_ff:].astype(BF16),
              w_ffn1_down[l].astype(BF16)),
        ffn2=(w_ffn2_up[l][:, :d_ff].astype(BF16), w_ffn2_up[l][:, d_ff:].astype(BF16),
              w_ffn2_down[l].astype(BF16)),
        wa=wi[:, :n_a].astype(BF16),
        wba=wba.astype(BF16),
        wgate=wi[:, n_a + 2 * GDN_HEADS:].astype(BF16),
        wsb=w_branch_sb[l].astype(BF16), wgdn=w_branch_gdn[l].astype(BF16), wout=w_out[l].astype(BF16),
        conv_w=conv_w[l],
        alog_l=lane_vec(gdn_a_log[l]), dt_l=lane_vec(gdn_dt_bias[l]),
        alog_s=sublane_vec(gdn_a_log[l]), dt_s=sublane_vec(gdn_dt_bias[l]),
        head_gain=gdn_norm_gain[l][None, :],
        dims=(sb_w, conv_dim, gdn_w),
    )


def _run_group(x, weights, past_k, past_v, conv_hist, s0, chunk):
    b, l, d = x.shape
    n = b * l
    k_list, v_list, s_list, c_list = [], [], [], []
    for li, w in enumerate(weights):
        sb_w, conv_dim, gdn_w = w["dims"]
        dh = sb_w // SB_HEADS
        tq = min(SB_QUERY_TILE, l)
        td = min(SB_KEY_TILE, l)
        g = w["gains"]
        x2 = _ffn(x.reshape(n, d), g[0], g[1], *w["ffn1"])
        q, k_rows, v_rows, kt, vb, u, z, ba, bat = _inproj(
            x2.reshape(b, l, d), g[2], w["wa"], w["wba"],
            sb_w=sb_w, conv_dim=conv_dim, gdn_w=gdn_w, sb_tile=td)
        if past_k is None:
            o_sb = _sb_attention(q, kt, vb, kt, vb, tq=tq, td=td, tk=td, causal_past=True)
        else:
            p = past_k.shape[2]
            tk = min(SB_KEY_TILE, p)
            ktp = past_k[li].astype(BF16).reshape(b, p // tk, tk, sb_w // SB_GROUP_LANES, SB_GROUP_LANES)
            ktp = ktp.transpose(0, 3, 1, 4, 2)
            vp = past_v[li].astype(BF16).reshape(b, p, sb_w)
            o_sb = _sb_attention(q, kt, vb, ktp, vp, tq=tq, td=td, tk=tk, causal_past=False)
        o_gdn, s_fin, conv_new = _gdn(u, z, ba, bat, conv_hist[li], s0[li], w["conv_w"],
                                      w["alog_l"], w["dt_l"], w["alog_s"], w["dt_s"], w["head_gain"],
                                      chunk=chunk)
        x2 = _merge(x2, o_sb.reshape(n, sb_w), o_gdn.reshape(n, gdn_w), g[2], g[3],
                    w["wgate"], w["wsb"], w["wgdn"], w["wout"])
        x2 = _ffn(x2, g[4], g[5], *w["ffn2"])
        x = x2.reshape(b, l, d)
        k_list.append(k_rows.reshape(b, l, SB_HEADS, dh))
        v_list.append(v_rows.reshape(b, l, SB_HEADS, dh))
        s_list.append(s_fin)
        c_list.append(conv_new)
    return x, jnp.stack(k_list), jnp.stack(v_list), jnp.stack(s_list), jnp.stack(c_list)


def kernel(x_prompt, x_sample, cache_sb_k, cache_sb_v, state_gdn, state_conv, norm_gains,
           w_ffn1_up, w_ffn1_down, w_in, conv_w, gdn_a_log, gdn_dt_bias, gdn_norm_gain,
           w_branch_sb, w_branch_gdn, w_out, w_ffn2_up, w_ffn2_down):
    depth = w_in.shape[0]
    weights = [_layer_weights(l, norm_gains, w_ffn1_up, w_ffn1_down, w_in, conv_w, gdn_a_log,
                              gdn_dt_bias, gdn_norm_gain, w_branch_sb, w_branch_gdn, w_out,
                              w_ffn2_up, w_ffn2_down) for l in range(depth)]
    bp = x_prompt.shape[0]
    chunk_prompt = 64
    zero_conv = jnp.zeros((depth, bp) + state_conv.shape[2:], state_conv.dtype)
    zero_state = jnp.zeros((depth, bp) + state_gdn.shape[2:], state_gdn.dtype)
    y_p, pk, pv, ps, pc = _run_group(x_prompt, weights, None, None, zero_conv, zero_state, chunk_prompt)
    y_s, sk, sv, ss, sc = _run_group(x_sample, weights, cache_sb_k, cache_sb_v, state_conv, state_gdn,
                                     x_sample.shape[1])
    return (y_p, y_s, pk, pv, ps, pc, sk, sv, ss, sc)
```

```python
import functools

import jax
import jax.numpy as jnp
from jax import lax
from jax.experimental import pallas as pl
from jax.experimental.pallas import tpu as pltpu

F32 = jnp.float32
BF16 = jnp.bfloat16

NORM_EPS = 1e-6
LOG2E = 1.4426950408889634
SB_HEADS = 8
GDN_HEADS = 4
LANES = 128
VMEM_LIMIT_BYTES = 56 * 1024 * 1024

ROW_TILE = 512
FF_CHUNK = 256
GDN_ROWS = 512
SB_KEY_TILE = 256
SB_QUERY_TILE = 256
SB_GROUP_LANES = 512
SB_ZERO_WEIGHT_LOG = 110.0


def _params(*sem):
    return pltpu.CompilerParams(dimension_semantics=sem, vmem_limit_bytes=VMEM_LIMIT_BYTES)


def _const_spec(shape):
    zeros = (0,) * len(shape)
    return pl.BlockSpec(shape, lambda *_: zeros)


def _rms(x, gain):
    ms = jnp.mean(x * x, axis=-1, keepdims=True)
    return x * lax.rsqrt(ms + NORM_EPS) * gain


def _dot(a, b):
    return jnp.dot(a, b, preferred_element_type=F32)


def _dot_nt(a, b):
    return lax.dot_general(a, b, (((1,), (1,)), ((), ())), preferred_element_type=F32)


def _dot_tn(a, b):
    return lax.dot_general(a, b, (((0,), (0,)), ((), ())), preferred_element_type=F32)


def _split3(x):
    hi = x.astype(BF16)
    r = x - hi.astype(F32)
    mid = r.astype(BF16)
    lo = (r - mid.astype(F32)).astype(BF16)
    return hi, mid, lo


def _dot_f32_exactrhs(a, b_bf16):
    hi, mid, lo = _split3(a)
    return _dot(hi, b_bf16) + _dot(mid, b_bf16) + _dot(lo, b_bf16)


def _dot_exactlhs_f32(a_bf16, b):
    hi, mid, lo = _split3(b)
    return _dot(a_bf16, hi) + _dot(a_bf16, mid) + _dot(a_bf16, lo)


def _ffn_kernel(x_ref, gin_ref, gout_ref, wg_ref, wu_ref, wd_ref, o_ref, act_ref):
    x = x_ref[...]
    h = _rms(x, gin_ref[...]).astype(BF16)
    d_ff = wd_ref.shape[0]
    for c in range(d_ff // FF_CHUNK):
        sl = slice(c * FF_CHUNK, (c + 1) * FF_CHUNK)
        g = _dot(h, wg_ref[:, sl])
        u = _dot(h, wu_ref[:, sl])
        act_ref[:, sl] = (g * jax.nn.sigmoid(g) * u).astype(BF16)
    y = _dot(act_ref[...], wd_ref[...])
    o_ref[...] = x + 0.5 * _rms(y, gout_ref[...])


def _ffn(x2, g_in, g_out, wg, wu, wd):
    n, d = x2.shape
    d_ff = wd.shape[0]
    tm = min(ROW_TILE, n)
    row = pl.BlockSpec((tm, d), lambda i: (i, 0))
    return pl.pallas_call(
        _ffn_kernel,
        grid=(n // tm,),
        in_specs=[row, _const_spec((1, d)), _const_spec((1, d)),
                  _const_spec((d, d_ff)), _const_spec((d, d_ff)), _const_spec((d_ff, d))],
        out_specs=row,
        out_shape=jax.ShapeDtypeStruct((n, d), F32),
        scratch_shapes=[pltpu.VMEM((tm, d_ff), BF16)],
        compiler_params=_params("parallel"),
        name="ffn_half_step",
    )(x2, g_in, g_out, wg, wu, wd)


def _inproj_kernel(x_ref, g_ref, wa_ref, wba_ref, *refs, sb_w, conv_dim, gdn_w, sb_tile, q_scale):
    q_ref, k_ref, v_ref, kt_ref, vb_ref, u_ref, z_ref, ba_ref, bat_ref = refs[-9:]
    h = _rms(x_ref[0], g_ref[...]).astype(BF16)
    tm = h.shape[0]

    def proj(lo, width):
        return _dot(h, wa_ref[:, lo:lo + width])

    q_ref[0] = (proj(0, sb_w) * q_scale).astype(BF16)
    k = proj(sb_w, sb_w)
    k_ref[0, 0] = k
    for later in range(1, k_ref.shape[0]):
        k_ref[later, 0] = jnp.zeros_like(k)
        v_ref[later, 0] = jnp.zeros_like(k)
    kt = k.T.astype(BF16)
    for hp in range(sb_w // SB_GROUP_LANES):
        for s in range(tm // sb_tile):
            kt_ref[0, hp, s] = kt[hp * SB_GROUP_LANES:(hp + 1) * SB_GROUP_LANES,
                                  s * sb_tile:(s + 1) * sb_tile]
    v = proj(2 * sb_w, sb_w)
    v_ref[0, 0] = v
    vb_ref[0] = v.astype(BF16)
    u_ref[0] = proj(3 * sb_w, conv_dim)
    z_ref[0] = proj(3 * sb_w + conv_dim, gdn_w)
    ba = _dot(h, wba_ref[...])
    ba_ref[0] = ba
    bat_ref[0] = ba.T[:8]


def _inproj(x, gain, wa, wba, kv_stacks, *, layer, depth, sb_w, conv_dim, gdn_w, sb_tile):
    b, l, d = x.shape
    tm = min(ROW_TILE, l)
    n_hp = sb_w // SB_GROUP_LANES
    kern = functools.partial(_inproj_kernel, sb_w=sb_w, conv_dim=conv_dim, gdn_w=gdn_w,
                             sb_tile=sb_tile,
                             q_scale=float((sb_w // SB_HEADS) ** -0.5))

    def rows(width):
        return pl.BlockSpec((1, tm, width), lambda bi, ti: (bi, ti, 0))

    out_shape = (
        jax.ShapeDtypeStruct((b, l, sb_w), BF16),
        jax.ShapeDtypeStruct((depth, b, l, sb_w), F32),
        jax.ShapeDtypeStruct((depth, b, l, sb_w), F32),
        jax.ShapeDtypeStruct((b, n_hp, l // sb_tile, SB_GROUP_LANES, sb_tile), BF16),
        jax.ShapeDtypeStruct((b, l, sb_w), BF16),
        jax.ShapeDtypeStruct((b, l, conv_dim), F32),
        jax.ShapeDtypeStruct((b, l, gdn_w), F32),
        jax.ShapeDtypeStruct((b, l, LANES), F32),
        jax.ShapeDtypeStruct((b, 8, l), F32),
    )
    if kv_stacks:
        layer_rows = pl.BlockSpec((1, 1, tm, sb_w), lambda bi, ti: (layer, bi, ti, 0))
    else:
        assert layer == 0
        layer_rows = pl.BlockSpec((depth, 1, tm, sb_w), lambda bi, ti: (0, bi, ti, 0))
    out_specs = (
        rows(sb_w), layer_rows, layer_rows,
        pl.BlockSpec((1, n_hp, tm // sb_tile, SB_GROUP_LANES, sb_tile), lambda bi, ti: (bi, 0, ti, 0, 0)),
        rows(sb_w), rows(conv_dim), rows(gdn_w), rows(LANES),
        pl.BlockSpec((1, 8, tm), lambda bi, ti: (bi, 0, ti)),
    )
    return pl.pallas_call(
        kern,
        grid=(b, l // tm),
        in_specs=[rows(d), _const_spec((1, d)), _const_spec(wa.shape), _const_spec(wba.shape)]
        + [pl.BlockSpec(memory_space=pl.ANY)] * len(kv_stacks),
        out_specs=out_specs,
        out_shape=out_shape,
        input_output_aliases={4 + i: 1 + i for i in range(len(kv_stacks))},
        compiler_params=_params("parallel", "parallel"),
        name="in_projection",
    )(x, gain, wa, wba, *kv_stacks)


def _sb_kernel(q_ref, ktd_ref, vd_ref, ktp_ref, vp_ref, o_ref, *, tq, td, nd, tk, heads, dh,
               causal_past, n_past_static):
    qi = pl.program_id(2)

    def from_key_matrix(n):
        r = lax.broadcasted_iota(jnp.int32, (n, n), 0)
        c = lax.broadcasted_iota(jnp.int32, (n, n), 1)
        return (r >= c).astype(BF16)

    def sweep(blocks, u_mat, st):
        pairs = [(blk, hh) for blk in blocks for hh in range(heads)]
        scores = []
        for (kt_blk, _, visible), hh in pairs:
            lanes = slice(hh * dh, (hh + 1) * dh)
            z = _dot(q_all[:, lanes], kt_blk[lanes, :])
            sp = jnp.maximum(z, 0.0) + jnp.log(1.0 + jnp.exp2(jnp.abs(z) * (-LOG2E)))
            nlk = sp if visible is None else jnp.where(visible, sp, 0.0)
            scores.append((z, nlk.astype(BF16), jnp.sum(nlk, axis=1, keepdims=True)))
        csums = [_dot(nlk_b, u_mat) for _, nlk_b, _ in scores]
        st = list(st)
        for ((_, v_blk, visible), hh), (z, _, row_sum), csum in zip(pairs, scores, csums):
            lanes = slice(hh * dh, (hh + 1) * dh)
            carry = st[2 * hh]
            a = jnp.exp(z - csum - carry)
            if visible is not None:
                a = jnp.where(visible, a, 0.0)
            st[2 * hh + 1] = st[2 * hh + 1] + _dot(a.astype(BF16), v_blk[:, lanes])
            st[2 * hh] = carry + row_sum
        return st

    q_all = q_ref[0]
    state = []
    for hh in range(heads):
        state.extend([jnp.zeros((tq, 1), F32), jnp.zeros((tq, dh), F32)])

    u_diag = from_key_matrix(td)
    rows_i = lax.broadcasted_iota(jnp.int32, (tq, td), 0)
    cols_i = lax.broadcasted_iota(jnp.int32, (tq, td), 1)
    diag_blocks = [(ktd_ref[0, 0, sd], vd_ref[0, sd * td:(sd + 1) * td, :], cols_i + sd * td < rows_i)
                   for sd in reversed(range(nd))]
    state = sweep(diag_blocks, u_diag, state)

    u_past = u_diag if tk == td else from_key_matrix(tk)
    n_past = qi * (tq // tk) if causal_past else n_past_static

    def min_carry(st):
        m = st[0]
        for hh in range(1, heads):
            m = jnp.minimum(m, st[2 * hh])
        return jnp.min(m)

    def cond(loop):
        i, smallest, _ = loop
        return jnp.logical_and(i < n_past, smallest < SB_ZERO_WEIGHT_LOG)

    def body(loop):
        i, _, st = loop
        j = n_past - 1 - i
        row0 = pl.multiple_of(j * tk, tk)
        st = sweep([(ktp_ref[0, 0, j], vp_ref[0, pl.ds(row0, tk), :], None)], u_past, st)
        return i + 1, min_carry(st), tuple(st)

    _, _, st = lax.while_loop(cond, body, (jnp.int32(0), min_carry(state), tuple(state)))
    o_ref[0] = jnp.concatenate([st[2 * hh + 1] for hh in range(heads)], axis=1).astype(BF16)


def _sb_attention(q, kt_diag, v_diag, kt_past, v_past, *, tq, td, tk, causal_past):
    b, l, w = q.shape
    n_g, hw = kt_diag.shape[1], kt_diag.shape[3]
    n_past_blocks = kt_past.shape[2]
    assert tq % td == 0 and (not causal_past or tq % tk == 0)
    p = v_past.shape[1]
    dh = w // SB_HEADS
    kern = functools.partial(_sb_kernel, tq=tq, td=td, nd=tq // td, tk=tk, heads=hw // dh, dh=dh,
                             causal_past=causal_past, n_past_static=n_past_blocks)
    return pl.pallas_call(
        kern,
        grid=(b, n_g, l // tq),
        in_specs=[
            pl.BlockSpec((1, tq, hw), lambda bi, hp, qi: (bi, qi, hp)),
            pl.BlockSpec((1, 1, tq // td, hw, td), lambda bi, hp, qi: (bi, hp, qi, 0, 0)),
            pl.BlockSpec((1, tq, hw), lambda bi, hp, qi: (bi, qi, hp)),
            pl.BlockSpec((1, 1, n_past_blocks, hw, tk), lambda bi, hp, qi: (bi, hp, 0, 0, 0),
                         pipeline_mode=pl.Buffered(1)),
            pl.BlockSpec((1, p, hw), lambda bi, hp, qi: (bi, 0, hp), pipeline_mode=pl.Buffered(1)),
        ],
        out_specs=pl.BlockSpec((1, tq, hw), lambda bi, hp, qi: (bi, qi, hp)),
        out_shape=jax.ShapeDtypeStruct((b, l, w), BF16),
        compiler_params=_params("parallel", "parallel", "arbitrary"),
        name="stick_breaking_attention",
    )(q, kt_diag, v_diag, kt_past, v_past)


def _gdn_kernel(u_ref, z_ref, ba_ref, bat_ref, hist_ref, s0_ref, cw_ref,
                alog_l_ref, dt_l_ref, alog_s_ref, dt_s_ref, hg_ref,
                o_ref, s_ref, cnew_ref, ext_ref, *, rows, chunk, width, n_taps):
    step = pl.program_id(1)
    n_steps = pl.num_programs(1)
    dk = width // GDN_HEADS
    n_sub = rows // chunk
    pad = 8
    n_hist = n_taps - 1

    @pl.when(step == 0)
    def _():
        ext_ref[0:pad, :] = jnp.zeros((pad, ext_ref.shape[1]), F32)
        ext_ref[pad - n_hist:pad, :] = hist_ref[0]
        s_ref[...] = s0_ref[...]

    ext_ref[pad:pad + rows, :] = u_ref[0]
    y = ext_ref[pad:pad + rows, :] * cw_ref[n_hist:n_taps, :]
    for i in reversed(range(n_hist)):
        y = y + ext_ref[pad - n_hist + i:pad - n_hist + i + rows, :] * cw_ref[i:i + 1, :]
    new_tail = ext_ref[rows:rows + pad, :]
    ext_ref[0:pad, :] = new_tail

    @pl.when(step == n_steps - 1)
    def _():
        cnew_ref[0] = new_tail[pad - n_hist:pad, :]

    qkv = y * jax.nn.sigmoid(y)

    heads = range(GDN_HEADS)
    qs, ks, vs = [], [], []
    for h in heads:
        q = qkv[:, h * dk:(h + 1) * dk]
        k = qkv[:, width + h * dk:width + (h + 1) * dk]
        qs.append(q * lax.rsqrt(jnp.sum(q * q, axis=-1, keepdims=True) + NORM_EPS) * float(dk ** -0.5))
        ks.append(k * lax.rsqrt(jnp.sum(k * k, axis=-1, keepdims=True) + NORM_EPS))
        vs.append(qkv[:, 2 * width + h * dk:2 * width + (h + 1) * dk])

    r_i = lax.broadcasted_iota(jnp.int32, (chunk, chunk), 0)
    c_i = lax.broadcasted_iota(jnp.int32, (chunk, chunk), 1)
    incl = r_i >= c_i
    strict = r_i > c_i
    eye = (r_i == c_i).astype(F32)
    rr = lax.broadcasted_iota(jnp.int32, (rows, rows), 0)
    cc = lax.broadcasted_iota(jnp.int32, (rows, rows), 1)
    same_chunk = (rr // chunk) == (cc // chunk)
    lower_incl = (same_chunk & (rr >= cc)).astype(BF16)
    upper_incl = (same_chunk & (rr <= cc)).astype(BF16)

    def softplus(t):
        return jnp.maximum(t, 0.0) + jnp.log(1.0 + jnp.exp(-jnp.abs(t)))

    ba = ba_ref[0]
    g_cols = -jnp.exp(alog_l_ref[...]) * softplus(ba + dt_l_ref[...])
    gcum_cols = _dot_exactlhs_f32(lower_incl, g_cols)
    beta_cols = jax.nn.sigmoid(ba)
    bat = bat_ref[0]
    g_rows = -jnp.exp(alog_s_ref[...]) * softplus(bat + dt_s_ref[...])
    gcum_rows = _dot_f32_exactrhs(g_rows, upper_incl)

    pairs = [(ci, h) for ci in range(n_sub) for h in heads]

    pre = {}
    for ci, h in pairs:
        rs = slice(ci * chunk, (ci + 1) * chunk)
        gc_col = gcum_cols[rs, GDN_HEADS + h:GDN_HEADS + h + 1]
        gc_row = gcum_rows[GDN_HEADS + h:GDN_HEADS + h + 1, rs]
        gc_last = gc_row[:, chunk - 1:chunk]
        beta = beta_cols[rs, h:h + 1]
        gamma = jnp.where(incl, jnp.exp(gc_col - gc_row), 0.0)
        decay_in = jnp.exp(gc_col)
        q, k, v = qs[h][rs], ks[h][rs], vs[h][rs]
        kb = k * beta
        kq = _dot_nt(jnp.concatenate([kb, q], axis=0).astype(BF16), k.astype(BF16))
        pre[ci, h] = dict(
            n=jnp.where(strict, -(kq[:chunk] * gamma), 0.0),
            qk=(kq[chunk:] * gamma).astype(BF16),
            rhs=jnp.concatenate([v * beta, kb * decay_in], axis=1).astype(BF16),
            q_dec=(q * decay_in).astype(BF16),
            k_end=(k * jnp.exp(gc_last - gc_col)).astype(BF16),
            chunk_decay=jnp.exp(gc_last))

    n_rounds = max(1, (chunk - 1).bit_length())
    m_pow = {p: pre[p]["n"] for p in pairs}
    t_inv = {p: eye + pre[p]["n"] for p in pairs}
    for i in range(n_rounds):
        last = i == n_rounds - 1
        for p in pairs:
            m_bf = m_pow[p].astype(BF16)
            if i == 0:
                m_pow[p] = _dot(m_bf, m_bf)
            elif last:
                t_inv[p] = t_inv[p] + _dot(t_inv[p].astype(BF16), m_bf)
            else:
                both = _dot(jnp.concatenate([m_bf, t_inv[p].astype(BF16)], axis=0), m_bf)
                m_pow[p] = both[:chunk]
                t_inv[p] = t_inv[p] + both[chunk:]

    sol = {p: _dot(t_inv[p].astype(BF16), pre[p]["rhs"]) for p in pairs}

    state = [s_ref[0, h] for h in heads]
    o_rows = [[None] * n_sub for _ in heads]
    for ci in range(n_sub):
        ws = [_dot(jnp.concatenate([sol[ci, h][:, dk:].astype(BF16), pre[ci, h]["q_dec"]], axis=0),
                   state[h].astype(BF16)) for h in heads]
        v_new = [(sol[ci, h][:, :dk] - ws[h][:chunk]).astype(BF16) for h in heads]
        for h in heads:
            o_rows[h][ci] = ws[h][chunk:] + _dot(pre[ci, h]["qk"], v_new[h])
        state = [state[h] * pre[ci, h]["chunk_decay"] + _dot_tn(pre[ci, h]["k_end"], v_new[h])
                 for h in heads]
    for h in heads:
        s_ref[0, h] = state[h]

    z = z_ref[0]
    outs = []
    for h in heads:
        o_h = o_rows[h][0] if n_sub == 1 else jnp.concatenate(o_rows[h], axis=0)
        z_h = z[:, h * dk:(h + 1) * dk]
        outs.append(_rms(o_h, hg_ref[...]) * (z_h * jax.nn.sigmoid(z_h)))
    o_ref[0] = jnp.concatenate(outs, axis=1).astype(BF16)


def _gdn(u, z, ba, bat, hist, s0, conv_w, alog_l, dt_l, alog_s, dt_s, head_gain, *, chunk):
    b, l, conv_dim = u.shape
    width = conv_dim // 3
    dk = width // GDN_HEADS
    n_taps = conv_w.shape[0]
    rows = min(GDN_ROWS, l)
    kern = functools.partial(_gdn_kernel, rows=rows, chunk=chunk, width=width, n_taps=n_taps)

    def row_block(wd):
        return pl.BlockSpec((1, rows, wd), lambda bi, si: (bi, si, 0))

    state_spec = pl.BlockSpec((1, GDN_HEADS, dk, dk), lambda bi, si: (bi, 0, 0, 0))
    hist_spec = pl.BlockSpec((1, n_taps - 1, conv_dim), lambda bi, si: (bi, 0, 0))
    return pl.pallas_call(
        kern,
        grid=(b, l // rows),
        in_specs=[row_block(conv_dim), row_block(width), row_block(LANES),
                  pl.BlockSpec((1, 8, rows), lambda bi, si: (bi, 0, si)),
                  hist_spec, state_spec, _const_spec(conv_w.shape),
                  _const_spec((1, LANES)), _const_spec((1, LANES)),
                  _const_spec((8, 1)), _const_spec((8, 1)), _const_spec((1, dk))],
        out_specs=(row_block(width), state_spec, hist_spec),
        out_shape=(jax.ShapeDtypeStruct((b, l, width), BF16),
                   jax.ShapeDtypeStruct(s0.shape, F32),
                   jax.ShapeDtypeStruct(hist.shape, F32)),
        scratch_shapes=[pltpu.VMEM((rows + 8, conv_dim), F32)],
        compiler_params=_params("parallel", "arbitrary"),
        name="gated_delta_rule",
    )(u, z, ba, bat, hist, s0, conv_w, alog_l, dt_l, alog_s, dt_s, head_gain)


def _merge_kernel(x_ref, osb_ref, ogdn_ref, gin_ref, gout_ref, wgate_ref, wsb_ref, wgdn_ref, wout_ref, o_ref):
    x = x_ref[...]
    d = x.shape[1]
    h = _rms(x, gin_ref[...]).astype(BF16)
    gate_sb = jax.nn.sigmoid(_dot(h, wgate_ref[:, :d]))
    merged = gate_sb * _dot(osb_ref[...], wsb_ref[...])
    gate_gdn = jax.nn.sigmoid(_dot(h, wgate_ref[:, d:]))
    merged = merged + gate_gdn * _dot(ogdn_ref[...], wgdn_ref[...])
    m = _dot(merged.astype(BF16), wout_ref[...])
    o_ref[...] = x + _rms(m, gout_ref[...])


def _merge(x2, osb2, ogdn2, g_in, g_out, wgate, wsb, wgdn, wout):
    n, d = x2.shape
    tm = min(ROW_TILE, n)

    def rows(wd):
        return pl.BlockSpec((tm, wd), lambda i: (i, 0))

    return pl.pallas_call(
        _merge_kernel,
        grid=(n // tm,),
        in_specs=[rows(d), rows(osb2.shape[1]), rows(ogdn2.shape[1]),
                  _const_spec((1, d)), _const_spec((1, d)),
                  _const_spec(wgate.shape), _const_spec(wsb.shape), _const_spec(wgdn.shape),
                  _const_spec(wout.shape)],
        out_specs=rows(d),
        out_shape=jax.ShapeDtypeStruct((n, d), F32),
        compiler_params=_params("parallel"),
        name="branch_merge",
    )(x2, osb2, ogdn2, g_in, g_out, wgate, wsb, wgdn, wout)


def _layer_weights(l, norm_gains, w_ffn1_up, w_ffn1_down, w_in, conv_w, gdn_a_log, gdn_dt_bias,
                   gdn_norm_gain, w_branch_sb, w_branch_gdn, w_out, w_ffn2_up, w_ffn2_down):
    d = w_in.shape[1]
    d_ff = w_ffn1_down.shape[1]
    sb_w = w_branch_sb.shape[1]
    gdn_w = w_branch_gdn.shape[1]
    conv_dim = conv_w.shape[2]
    n_a = 3 * sb_w + conv_dim + gdn_w
    wi = w_in[l]
    wba = jnp.zeros((d, LANES), F32).at[:, :2 * GDN_HEADS].set(wi[:, n_a:n_a + 2 * GDN_HEADS])

    def lane_vec(p):
        return jnp.zeros((1, LANES), F32).at[0, GDN_HEADS:2 * GDN_HEADS].set(p)

    def sublane_vec(p):
        return jnp.zeros((8, 1), F32).at[GDN_HEADS:2 * GDN_HEADS, 0].set(p)

    return dict(
        gains=[norm_gains[l, i][None, :] for i in range(6)],
        ffn1=(w_ffn1_up[l][:, :d_ff].astype(BF16), w_ffn1_up[l][:, d_ff:].astype(BF16),
              w_ffn1_down[l].astype(BF16)),
        ffn2=(w_ffn2_up[l][:, :d_ff].astype(BF16), w_ffn2_up[l][:, d_ff:].astype(BF16),
              w_ffn2_down[l].astype(BF16)),
        wa=wi[:, :n_a].astype(BF16),
        wba=wba.astype(BF16),
        wgate=wi[:, n_a + 2 * GDN_HEADS:].astype(BF16),
        wsb=w_branch_sb[l].astype(BF16), wgdn=w_branch_gdn[l].astype(BF16), wout=w_out[l].astype(BF16),
        conv_w=conv_w[l],
        alog_l=lane_vec(gdn_a_log[l]), dt_l=lane_vec(gdn_dt_bias[l]),
        alog_s=sublane_vec(gdn_a_log[l]), dt_s=sublane_vec(gdn_dt_bias[l]),
        head_gain=gdn_norm_gain[l][None, :],
        dims=(sb_w, conv_dim, gdn_w),
    )


def _run_group(x, weights, past_k, past_v, conv_hist, s0, chunk):
    b, l, d = x.shape
    n = b * l
    s_list, c_list = [], []
    kv_stacks = ()
    for li, w in enumerate(weights):
        sb_w, conv_dim, gdn_w = w["dims"]
        dh = sb_w // SB_HEADS
        tq = min(SB_QUERY_TILE, l)
        td = min(SB_KEY_TILE, l)
        g = w["gains"]
        x2 = _ffn(x.reshape(n, d), g[0], g[1], *w["ffn1"])
        q, k_stack, v_stack, kt, vb, u, z, ba, bat = _inproj(
            x2.reshape(b, l, d), g[2], w["wa"], w["wba"], kv_stacks,
            layer=li, depth=len(weights), sb_w=sb_w, conv_dim=conv_dim, gdn_w=gdn_w, sb_tile=td)
        kv_stacks = (k_stack, v_stack)
        if past_k is None:
            o_sb = _sb_attention(q, kt, vb, kt, vb, tq=tq, td=td, tk=td, causal_past=True)
        else:
            p = past_k.shape[2]
            tk = min(SB_KEY_TILE, p)
            ktp = past_k[li].astype(BF16).reshape(b, p // tk, tk, sb_w // SB_GROUP_LANES, SB_GROUP_LANES)
            ktp = ktp.transpose(0, 3, 1, 4, 2)
            vp = past_v[li].astype(BF16).reshape(b, p, sb_w)
            o_sb = _sb_attention(q, kt, vb, ktp, vp, tq=tq, td=td, tk=tk, causal_past=False)
        o_gdn, s_fin, conv_new = _gdn(u, z, ba, bat, conv_hist[li], s0[li], w["conv_w"],
                                      w["alog_l"], w["dt_l"], w["alog_s"], w["dt_s"], w["head_gain"],
                                      chunk=chunk)
        x2 = _merge(x2, o_sb.reshape(n, sb_w), o_gdn.reshape(n, gdn_w), g[2], g[3],
                    w["wgate"], w["wsb"], w["wgdn"], w["wout"])
        x2 = _ffn(x2, g[4], g[5], *w["ffn2"])
        x = x2.reshape(b, l, d)
        s_list.append(s_fin)
        c_list.append(conv_new)
    k_all, v_all = (t.reshape(len(weights), b, l, SB_HEADS, dh) for t in kv_stacks)
    return x, k_all, v_all, jnp.stack(s_list), jnp.stack(c_list)


def kernel(x_prompt, x_sample, cache_sb_k, cache_sb_v, state_gdn, state_conv, norm_gains,
           w_ffn1_up, w_ffn1_down, w_in, conv_w, gdn_a_log, gdn_dt_bias, gdn_norm_gain,
           w_branch_sb, w_branch_gdn, w_out, w_ffn2_up, w_ffn2_down):
    depth = w_in.shape[0]
    weights = [_layer_weights(l, norm_gains, w_ffn1_up, w_ffn1_down, w_in, conv_w, gdn_a_log,
                              gdn_dt_bias, gdn_norm_gain, w_branch_sb, w_branch_gdn, w_out,
                              w_ffn2_up, w_ffn2_down) for l in range(depth)]
    bp = x_prompt.shape[0]
    chunk_prompt = 64
    zero_conv = jnp.zeros((depth, bp) + state_conv.shape[2:], state_conv.dtype)
    zero_state = jnp.zeros((depth, bp) + state_gdn.shape[2:], state_gdn.dtype)
    y_p, pk, pv, ps, pc = _run_group(x_prompt, weights, None, None, zero_conv, zero_state, chunk_prompt)
    y_s, sk, sv, ss, sc = _run_group(x_sample, weights, cache_sb_k, cache_sb_v, state_conv, state_gdn,
                                     x_sample.shape[1])
    return (y_p, y_s, pk, pv, ps, pc, sk, sv, ss, sc)
```

```python
import functools

import jax
import jax.numpy as jnp
from jax import lax
from jax.experimental import pallas as pl
from jax.experimental.pallas import tpu as pltpu

F32 = jnp.float32
BF16 = jnp.bfloat16

NORM_EPS = 1e-6
LOG2E = 1.4426950408889634
SB_HEADS = 8
GDN_HEADS = 4
LANES = 128
VMEM_LIMIT_BYTES = 56 * 1024 * 1024

ROW_TILE = 512
CAST_ROWS = 256
FF_CHUNK = 256
GDN_ROWS = 512
SB_KEY_TILE = 256
SB_QUERY_TILE = 256
SB_GROUP_LANES = 512
SB_ZERO_WEIGHT_LOG = 110.0
SB_NO_PAST_CARRY = 1e30


def _params(*sem):
    return pltpu.CompilerParams(dimension_semantics=sem, vmem_limit_bytes=VMEM_LIMIT_BYTES)


def _const_spec(shape):
    zeros = (0,) * len(shape)
    return pl.BlockSpec(shape, lambda *_: zeros)


def _rms(x, gain):
    ms = jnp.mean(x * x, axis=-1, keepdims=True)
    return x * lax.rsqrt(ms + NORM_EPS) * gain


def _dot(a, b):
    return jnp.dot(a, b, preferred_element_type=F32)


def _dot_nt(a, b):
    return lax.dot_general(a, b, (((1,), (1,)), ((), ())), preferred_element_type=F32)


def _dot_tn(a, b):
    return lax.dot_general(a, b, (((0,), (0,)), ((), ())), preferred_element_type=F32)


def _split3(x):
    hi = x.astype(BF16)
    r = x - hi.astype(F32)
    mid = r.astype(BF16)
    lo = (r - mid.astype(F32)).astype(BF16)
    return hi, mid, lo


def _dot_f32_exactrhs(a, b_bf16):
    hi, mid, lo = _split3(a)
    return _dot(hi, b_bf16) + _dot(mid, b_bf16) + _dot(lo, b_bf16)


def _dot_exactlhs_f32(a_bf16, b):
    hi, mid, lo = _split3(b)
    return _dot(a_bf16, hi) + _dot(a_bf16, mid) + _dot(a_bf16, lo)


def _ffn_kernel(x_ref, gin_ref, gout_ref, wup_ref, wd_ref, o_ref, act_ref):
    x = x_ref[...]
    h = _rms(x, gin_ref[...]).astype(BF16)
    d_ff = wd_ref.shape[1]
    for c in range(d_ff // FF_CHUNK):
        sl = slice(c * FF_CHUNK, (c + 1) * FF_CHUNK)
        g = _dot(h, wup_ref[0, :, sl])
        u = _dot(h, wup_ref[0, :, d_ff + c * FF_CHUNK:d_ff + (c + 1) * FF_CHUNK])
        act_ref[:, sl] = (g * jax.nn.sigmoid(g) * u).astype(BF16)
    y = _dot(act_ref[...], wd_ref[0])
    o_ref[...] = x + 0.5 * _rms(y, gout_ref[...])


def _ffn(x2, g_in, g_out, w_up, w_down, layer):
    n, d = x2.shape
    d_ff = w_down.shape[1]
    tm = min(ROW_TILE, n)
    row = pl.BlockSpec((tm, d), lambda i: (i, 0))
    return pl.pallas_call(
        _ffn_kernel,
        grid=(n // tm,),
        in_specs=[row, _const_spec((1, d)), _const_spec((1, d)),
                  pl.BlockSpec((1, d, 2 * d_ff), lambda i: (layer, 0, 0)),
                  pl.BlockSpec((1, d_ff, d), lambda i: (layer, 0, 0))],
        out_specs=row,
        out_shape=jax.ShapeDtypeStruct((n, d), F32),
        scratch_shapes=[pltpu.VMEM((tm, d_ff), BF16)],
        compiler_params=_params("parallel"),
        name="ffn_half_step",
    )(x2, g_in, g_out, w_up, w_down)


def _cast_kernel(w_ref, o_ref):
    o_ref[...] = w_ref[...].astype(o_ref.dtype)


def _to_bf16(w):
    depth, rows, cols = w.shape
    tr = min(rows, CAST_ROWS)
    assert rows % tr == 0
    spec = pl.BlockSpec((1, tr, cols), lambda li, ri: (li, ri, 0))
    return pl.pallas_call(
        _cast_kernel,
        grid=(depth, rows // tr),
        in_specs=[spec],
        out_specs=spec,
        out_shape=jax.ShapeDtypeStruct(w.shape, BF16),
        compiler_params=_params("parallel", "parallel"),
        name="weights_to_bf16",
    )(w)


def _inproj_kernel(x_ref, g_ref, wa_ref, wba_ref, *refs, sb_w, conv_dim, gdn_w, sb_tile, q_scale):
    q_ref, k_ref, v_ref, kt_ref, vb_ref, u_ref, z_ref, ba_ref, bat_ref = refs[-9:]
    h = _rms(x_ref[0], g_ref[...]).astype(BF16)
    tm = h.shape[0]

    def proj(lo, width):
        return _dot(h, wa_ref[:, lo:lo + width])

    q_ref[0] = (proj(0, sb_w) * q_scale).astype(BF16)
    k = proj(sb_w, sb_w)
    k_ref[0, 0] = k
    for later in range(1, k_ref.shape[0]):
        k_ref[later, 0] = jnp.zeros_like(k)
        v_ref[later, 0] = jnp.zeros_like(k)
    kt = k.T.astype(BF16)
    for hp in range(sb_w // SB_GROUP_LANES):
        for s in range(tm // sb_tile):
            kt_ref[0, hp, s] = kt[hp * SB_GROUP_LANES:(hp + 1) * SB_GROUP_LANES,
                                  s * sb_tile:(s + 1) * sb_tile]
    v = proj(2 * sb_w, sb_w)
    v_ref[0, 0] = v
    vb_ref[0] = v.astype(BF16)
    u_ref[0] = proj(3 * sb_w, conv_dim)
    z_ref[0] = proj(3 * sb_w + conv_dim, gdn_w)
    ba = _dot(h, wba_ref[...])
    ba_ref[0] = ba
    bat_ref[0] = ba.T[:8]


def _inproj(x, gain, wa, wba, kv_stacks, *, layer, depth, sb_w, conv_dim, gdn_w, sb_tile):
    b, l, d = x.shape
    tm = min(ROW_TILE, l)
    n_hp = sb_w // SB_GROUP_LANES
    kern = functools.partial(_inproj_kernel, sb_w=sb_w, conv_dim=conv_dim, gdn_w=gdn_w,
                             sb_tile=sb_tile,
                             q_scale=float((sb_w // SB_HEADS) ** -0.5))

    def rows(width):
        return pl.BlockSpec((1, tm, width), lambda bi, ti: (bi, ti, 0))

    out_shape = (
        jax.ShapeDtypeStruct((b, l, sb_w), BF16),
        jax.ShapeDtypeStruct((depth, b, l, sb_w), F32),
        jax.ShapeDtypeStruct((depth, b, l, sb_w), F32),
        jax.ShapeDtypeStruct((b, n_hp, l // sb_tile, SB_GROUP_LANES, sb_tile), BF16),
        jax.ShapeDtypeStruct((b, l, sb_w), BF16),
        jax.ShapeDtypeStruct((b, l, conv_dim), F32),
        jax.ShapeDtypeStruct((b, l, gdn_w), F32),
        jax.ShapeDtypeStruct((b, l, LANES), F32),
        jax.ShapeDtypeStruct((b, 8, l), F32),
    )
    if kv_stacks:
        layer_rows = pl.BlockSpec((1, 1, tm, sb_w), lambda bi, ti: (layer, bi, ti, 0))
    else:
        assert layer == 0
        layer_rows = pl.BlockSpec((depth, 1, tm, sb_w), lambda bi, ti: (0, bi, ti, 0))
    out_specs = (
        rows(sb_w), layer_rows, layer_rows,
        pl.BlockSpec((1, n_hp, tm // sb_tile, SB_GROUP_LANES, sb_tile), lambda bi, ti: (bi, 0, ti, 0, 0)),
        rows(sb_w), rows(conv_dim), rows(gdn_w), rows(LANES),
        pl.BlockSpec((1, 8, tm), lambda bi, ti: (bi, 0, ti)),
    )
    return pl.pallas_call(
        kern,
        grid=(b, l // tm),
        in_specs=[rows(d), _const_spec((1, d)), _const_spec(wa.shape), _const_spec(wba.shape)]
        + [pl.BlockSpec(memory_space=pl.ANY)] * len(kv_stacks),
        out_specs=out_specs,
        out_shape=out_shape,
        input_output_aliases={4 + i: 1 + i for i in range(len(kv_stacks))},
        compiler_params=_params("parallel", "parallel"),
        name="in_projection",
    )(x, gain, wa, wba, *kv_stacks)


def _sb_kernel(q_ref, ktd_ref, vd_ref, ktp_ref, vp_ref, o_ref, *, tq, td, nd, tk, heads, dh,
               causal_past, n_past_static):
    qi = pl.program_id(2)

    def from_key_matrix(n):
        r = lax.broadcasted_iota(jnp.int32, (n, n), 0)
        c = lax.broadcasted_iota(jnp.int32, (n, n), 1)
        return (r >= c).astype(BF16)

    def sweep(blocks, st):
        pairs = [(blk, hh) for blk in blocks for hh in range(heads)]
        scores = []
        for blk, hh in pairs:
            lanes = slice(hh * dh, (hh + 1) * dh)
            z = _dot(q_all[:, lanes], blk["kt"][lanes, :])
            sp = jnp.maximum(z, 0.0) + jnp.log(1.0 + jnp.exp2(jnp.abs(z) * (-LOG2E)))
            nlk = sp if blk["visible"] is None else jnp.where(blk["visible"], sp, 0.0)
            scores.append((z, nlk.astype(BF16), jnp.sum(nlk, axis=1, keepdims=True)))
        csums = [_dot(nlk_b, blk["u"]) for (blk, _), (_, nlk_b, _) in zip(pairs, scores)]
        st = list(st)
        for (blk, hh), (z, _, row_sum), csum in zip(pairs, scores, csums):
            lanes = slice(hh * dh, (hh + 1) * dh)
            carry = st[2 * hh]
            if blk.get("carry_bias") is not None:
                carry = carry + blk["carry_bias"]
            a = jnp.exp(z - csum - carry)
            if blk["visible"] is not None:
                a = jnp.where(blk["visible"], a, 0.0)
            st[2 * hh + 1] = st[2 * hh + 1] + _dot(a.astype(BF16), blk["v"][:, lanes])
            st[2 * hh] = carry + row_sum
        return st

    q_all = q_ref[0]
    state = []
    for hh in range(heads):
        state.extend([jnp.zeros((tq, 1), F32), jnp.zeros((tq, dh), F32)])

    u_diag = from_key_matrix(td)
    u_past = u_diag if tk == td else from_key_matrix(tk)
    rows_i = lax.broadcasted_iota(jnp.int32, (tq, td), 0)
    cols_i = lax.broadcasted_iota(jnp.int32, (tq, td), 1)
    n_past = qi * (tq // tk) if causal_past else n_past_static

    def past_block(j, carry_bias=None):
        row0 = pl.multiple_of(j * tk, tk)
        return dict(kt=ktp_ref[0, 0, j], v=vp_ref[0, pl.ds(row0, tk), :], visible=None, u=u_past,
                    carry_bias=carry_bias)

    first = [dict(kt=ktd_ref[0, 0, sd], v=vd_ref[0, sd * td:(sd + 1) * td, :],
                  visible=cols_i + sd * td < rows_i, u=u_diag) for sd in reversed(range(nd))]
    no_past_bias = jnp.where(n_past > 0, 0.0, SB_NO_PAST_CARRY) if causal_past else None
    first.append(past_block(jnp.maximum(n_past - 1, 0), no_past_bias))
    state = sweep(first, state)

    def min_carry(st):
        m = st[0]
        for hh in range(1, heads):
            m = jnp.minimum(m, st[2 * hh])
        return jnp.min(m)

    def cond(loop):
        i, smallest, _ = loop
        return jnp.logical_and(i < n_past, smallest < SB_ZERO_WEIGHT_LOG)

    def body(loop):
        i, _, st = loop
        st = sweep([past_block(n_past - 1 - i)], st)
        return i + 1, min_carry(st), tuple(st)

    _, _, st = lax.while_loop(cond, body, (jnp.int32(1), min_carry(state), tuple(state)))
    o_ref[0] = jnp.concatenate([st[2 * hh + 1] for hh in range(heads)], axis=1).astype(BF16)


def _sb_attention(q, kt_diag, v_diag, kt_past, v_past, *, tq, td, tk, causal_past):
    b, l, w = q.shape
    n_g, hw = kt_diag.shape[1], kt_diag.shape[3]
    n_past_blocks = kt_past.shape[2]
    assert tq % td == 0 and (not causal_past or tq % tk == 0)
    p = v_past.shape[1]
    dh = w // SB_HEADS
    kern = functools.partial(_sb_kernel, tq=tq, td=td, nd=tq // td, tk=tk, heads=hw // dh, dh=dh,
                             causal_past=causal_past, n_past_static=n_past_blocks)
    return pl.pallas_call(
        kern,
        grid=(b, n_g, l // tq),
        in_specs=[
            pl.BlockSpec((1, tq, hw), lambda bi, hp, qi: (bi, qi, hp)),
            pl.BlockSpec((1, 1, tq // td, hw, td), lambda bi, hp, qi: (bi, hp, qi, 0, 0)),
            pl.BlockSpec((1, tq, hw), lambda bi, hp, qi: (bi, qi, hp)),
            pl.BlockSpec((1, 1, n_past_blocks, hw, tk), lambda bi, hp, qi: (bi, hp, 0, 0, 0),
                         pipeline_mode=pl.Buffered(1)),
            pl.BlockSpec((1, p, hw), lambda bi, hp, qi: (bi, 0, hp), pipeline_mode=pl.Buffered(1)),
        ],
        out_specs=pl.BlockSpec((1, tq, hw), lambda bi, hp, qi: (bi, qi, hp)),
        out_shape=jax.ShapeDtypeStruct((b, l, w), BF16),
        compiler_params=_params("parallel", "parallel", "arbitrary"),
        name="stick_breaking_attention",
    )(q, kt_diag, v_diag, kt_past, v_past)


def _cache_layout_kernel(k_ref, v_ref, kt_ref, vb_ref):
    _, n_g, n_blk, hw, tk = kt_ref.shape
    kt = k_ref[0, 0].T.astype(BF16)
    for g in range(n_g):
        for s in range(n_blk):
            kt_ref[0, g, s] = kt[g * hw:(g + 1) * hw, s * tk:(s + 1) * tk]
    vb_ref[0] = v_ref[0, 0].astype(BF16)


def _cache_layout(cache_k, cache_v, *, layer, tk):
    _, b, p, w = cache_k.shape
    n_g = w // SB_GROUP_LANES
    rows = min(p, 4 * tk)
    src = pl.BlockSpec((1, 1, rows, w), lambda bi, ji: (layer, bi, ji, 0))
    return pl.pallas_call(
        _cache_layout_kernel,
        grid=(b, p // rows),
        in_specs=[src, src],
        out_specs=(pl.BlockSpec((1, n_g, rows // tk, SB_GROUP_LANES, tk), lambda bi, ji: (bi, 0, ji, 0, 0)),
                   pl.BlockSpec((1, rows, w), lambda bi, ji: (bi, ji, 0))),
        out_shape=(jax.ShapeDtypeStruct((b, n_g, p // tk, SB_GROUP_LANES, tk), BF16),
                   jax.ShapeDtypeStruct((b, p, w), BF16)),
        compiler_params=_params("parallel", "parallel"),
        name="cache_layout",
    )(cache_k, cache_v)


def _gdn_kernel(u_ref, z_ref, ba_ref, bat_ref, hist_ref, s0_ref, cw_ref,
                alog_l_ref, dt_l_ref, alog_s_ref, dt_s_ref, hg_ref,
                o_ref, s_ref, cnew_ref, ext_ref, *, rows, chunk, width, n_taps):
    step = pl.program_id(1)
    n_steps = pl.num_programs(1)
    dk = width // GDN_HEADS
    n_sub = rows // chunk
    pad = 8
    n_hist = n_taps - 1

    @pl.when(step == 0)
    def _():
        ext_ref[0:pad, :] = jnp.zeros((pad, ext_ref.shape[1]), F32)
        ext_ref[pad - n_hist:pad, :] = hist_ref[0]
        s_ref[...] = s0_ref[...]

    ext_ref[pad:pad + rows, :] = u_ref[0]
    y = ext_ref[pad:pad + rows, :] * cw_ref[n_hist:n_taps, :]
    for i in reversed(range(n_hist)):
        y = y + ext_ref[pad - n_hist + i:pad - n_hist + i + rows, :] * cw_ref[i:i + 1, :]
    new_tail = ext_ref[rows:rows + pad, :]
    ext_ref[0:pad, :] = new_tail

    @pl.when(step == n_steps - 1)
    def _():
        cnew_ref[0] = new_tail[pad - n_hist:pad, :]

    qkv = y * jax.nn.sigmoid(y)

    heads = range(GDN_HEADS)
    qs, ks, vs = [], [], []
    for h in heads:
        q = qkv[:, h * dk:(h + 1) * dk]
        k = qkv[:, width + h * dk:width + (h + 1) * dk]
        qs.append(q * lax.rsqrt(jnp.sum(q * q, axis=-1, keepdims=True) + NORM_EPS) * float(dk ** -0.5))
        ks.append(k * lax.rsqrt(jnp.sum(k * k, axis=-1, keepdims=True) + NORM_EPS))
        vs.append(qkv[:, 2 * width + h * dk:2 * width + (h + 1) * dk])

    r_i = lax.broadcasted_iota(jnp.int32, (chunk, chunk), 0)
    c_i = lax.broadcasted_iota(jnp.int32, (chunk, chunk), 1)
    incl = r_i >= c_i
    strict = r_i > c_i
    eye = (r_i == c_i).astype(F32)
    rr = lax.broadcasted_iota(jnp.int32, (rows, rows), 0)
    cc = lax.broadcasted_iota(jnp.int32, (rows, rows), 1)
    same_chunk = (rr // chunk) == (cc // chunk)
    lower_incl = (same_chunk & (rr >= cc)).astype(BF16)
    upper_incl = (same_chunk & (rr <= cc)).astype(BF16)

    def softplus(t):
        return jnp.maximum(t, 0.0) + jnp.log(1.0 + jnp.exp(-jnp.abs(t)))

    ba = ba_ref[0]
    g_cols = -jnp.exp(alog_l_ref[...]) * softplus(ba + dt_l_ref[...])
    gcum_cols = _dot_exactlhs_f32(lower_incl, g_cols)
    beta_cols = jax.nn.sigmoid(ba)
    bat = bat_ref[0]
    g_rows = -jnp.exp(alog_s_ref[...]) * softplus(bat + dt_s_ref[...])
    gcum_rows = _dot_f32_exactrhs(g_rows, upper_incl)

    pairs = [(ci, h) for ci in range(n_sub) for h in heads]

    pre = {}
    for ci, h in pairs:
        rs = slice(ci * chunk, (ci + 1) * chunk)
        gc_col = gcum_cols[rs, GDN_HEADS + h:GDN_HEADS + h + 1]
        gc_row = gcum_rows[GDN_HEADS + h:GDN_HEADS + h + 1, rs]
        gc_last = gc_row[:, chunk - 1:chunk]
        beta = beta_cols[rs, h:h + 1]
        gamma = jnp.where(incl, jnp.exp(gc_col - gc_row), 0.0)
        decay_in = jnp.exp(gc_col)
        q, k, v = qs[h][rs], ks[h][rs], vs[h][rs]
        kb = k * beta
        kq = _dot_nt(jnp.concatenate([kb, q], axis=0).astype(BF16), k.astype(BF16))
        pre[ci, h] = dict(
            n=jnp.where(strict, -(kq[:chunk] * gamma), 0.0),
            qk=(kq[chunk:] * gamma).astype(BF16),
            rhs=jnp.concatenate([v * beta, kb * decay_in], axis=1).astype(BF16),
            q_dec=(q * decay_in).astype(BF16),
            k_end=(k * jnp.exp(gc_last - gc_col)).astype(BF16),
            chunk_decay=jnp.exp(gc_last))

    n_rounds = max(1, (chunk - 1).bit_length())
    m_pow = {p: pre[p]["n"] for p in pairs}
    t_inv = {p: eye + pre[p]["n"] for p in pairs}
    for i in range(n_rounds):
        last = i == n_rounds - 1
        for p in pairs:
            m_bf = m_pow[p].astype(BF16)
            if i == 0:
                m_pow[p] = _dot(m_bf, m_bf)
            elif last:
                t_inv[p] = t_inv[p] + _dot(t_inv[p].astype(BF16), m_bf)
            else:
                both = _dot(jnp.concatenate([m_bf, t_inv[p].astype(BF16)], axis=0), m_bf)
                m_pow[p] = both[:chunk]
                t_inv[p] = t_inv[p] + both[chunk:]

    sol = {p: _dot(t_inv[p].astype(BF16), pre[p]["rhs"]) for p in pairs}

    state = [s_ref[0, h] for h in heads]
    o_rows = [[None] * n_sub for _ in heads]
    for ci in range(n_sub):
        ws = [_dot(jnp.concatenate([sol[ci, h][:, dk:].astype(BF16), pre[ci, h]["q_dec"]], axis=0),
                   state[h].astype(BF16)) for h in heads]
        v_new = [(sol[ci, h][:, :dk] - ws[h][:chunk]).astype(BF16) for h in heads]
        for h in heads:
            o_rows[h][ci] = ws[h][chunk:] + _dot(pre[ci, h]["qk"], v_new[h])
        state = [state[h] * pre[ci, h]["chunk_decay"] + _dot_tn(pre[ci, h]["k_end"], v_new[h])
                 for h in heads]
    for h in heads:
        s_ref[0, h] = state[h]

    z = z_ref[0]
    outs = []
    for h in heads:
        o_h = o_rows[h][0] if n_sub == 1 else jnp.concatenate(o_rows[h], axis=0)
        z_h = z[:, h * dk:(h + 1) * dk]
        outs.append(_rms(o_h, hg_ref[...]) * (z_h * jax.nn.sigmoid(z_h)))
    o_ref[0] = jnp.concatenate(outs, axis=1).astype(BF16)


def _gdn(u, z, ba, bat, hist, s0, conv_w, alog_l, dt_l, alog_s, dt_s, head_gain, *, chunk):
    b, l, conv_dim = u.shape
    width = conv_dim // 3
    dk = width // GDN_HEADS
    n_taps = conv_w.shape[0]
    rows = min(GDN_ROWS, l)
    kern = functools.partial(_gdn_kernel, rows=rows, chunk=chunk, width=width, n_taps=n_taps)

    def row_block(wd):
        return pl.BlockSpec((1, rows, wd), lambda bi, si: (bi, si, 0))

    state_spec = pl.BlockSpec((1, GDN_HEADS, dk, dk), lambda bi, si: (bi, 0, 0, 0))
    hist_spec = pl.BlockSpec((1, n_taps - 1, conv_dim), lambda bi, si: (bi, 0, 0))
    return pl.pallas_call(
        kern,
        grid=(b, l // rows),
        in_specs=[row_block(conv_dim), row_block(width), row_block(LANES),
                  pl.BlockSpec((1, 8, rows), lambda bi, si: (bi, 0, si)),
                  hist_spec, state_spec, _const_spec(conv_w.shape),
                  _const_spec((1, LANES)), _const_spec((1, LANES)),
                  _const_spec((8, 1)), _const_spec((8, 1)), _const_spec((1, dk))],
        out_specs=(row_block(width), state_spec, hist_spec),
        out_shape=(jax.ShapeDtypeStruct((b, l, width), BF16),
                   jax.ShapeDtypeStruct(s0.shape, F32),
                   jax.ShapeDtypeStruct(hist.shape, F32)),
        scratch_shapes=[pltpu.VMEM((rows + 8, conv_dim), F32)],
        compiler_params=_params("parallel", "arbitrary"),
        name="gated_delta_rule",
    )(u, z, ba, bat, hist, s0, conv_w, alog_l, dt_l, alog_s, dt_s, head_gain)


def _merge_kernel(x_ref, osb_ref, ogdn_ref, gin_ref, gout_ref, wgate_ref, wsb_ref, wgdn_ref, wout_ref, o_ref):
    x = x_ref[...]
    d = x.shape[1]
    h = _rms(x, gin_ref[...]).astype(BF16)
    gate_sb = jax.nn.sigmoid(_dot(h, wgate_ref[:, :d]))
    merged = gate_sb * _dot(osb_ref[...], wsb_ref[...])
    gate_gdn = jax.nn.sigmoid(_dot(h, wgate_ref[:, d:]))
    merged = merged + gate_gdn * _dot(ogdn_ref[...], wgdn_ref[...])
    m = _dot(merged.astype(BF16), wout_ref[...])
    o_ref[...] = x + _rms(m, gout_ref[...])


def _merge(x2, osb2, ogdn2, g_in, g_out, wgate, wsb, wgdn, wout):
    n, d = x2.shape
    tm = min(ROW_TILE, n)

    def rows(wd):
        return pl.BlockSpec((tm, wd), lambda i: (i, 0))

    return pl.pallas_call(
        _merge_kernel,
        grid=(n // tm,),
        in_specs=[rows(d), rows(osb2.shape[1]), rows(ogdn2.shape[1]),
                  _const_spec((1, d)), _const_spec((1, d)),
                  _const_spec(wgate.shape), _const_spec(wsb.shape), _const_spec(wgdn.shape),
                  _const_spec(wout.shape)],
        out_specs=rows(d),
        out_shape=jax.ShapeDtypeStruct((n, d), F32),
        compiler_params=_params("parallel"),
        name="branch_merge",
    )(x2, osb2, ogdn2, g_in, g_out, wgate, wsb, wgdn, wout)


def _layer_weights(l, norm_gains, w_in_bf, conv_w, gdn_a_log, gdn_dt_bias, gdn_norm_gain,
                   w_branch_sb, w_branch_gdn, w_out):
    d = w_in_bf.shape[1]
    sb_w = w_branch_sb.shape[1]
    gdn_w = w_branch_gdn.shape[1]
    conv_dim = conv_w.shape[2]
    n_a = 3 * sb_w + conv_dim + gdn_w
    wi = w_in_bf[l]
    wba = jnp.zeros((d, LANES), BF16).at[:, :2 * GDN_HEADS].set(wi[:, n_a:n_a + 2 * GDN_HEADS])

    def lane_vec(p):
        return jnp.zeros((1, LANES), F32).at[0, GDN_HEADS:2 * GDN_HEADS].set(p)

    def sublane_vec(p):
        return jnp.zeros((8, 1), F32).at[GDN_HEADS:2 * GDN_HEADS, 0].set(p)

    return dict(
        gains=[norm_gains[l, i][None, :] for i in range(6)],
        wa=wi[:, :n_a],
        wba=wba,
        wgate=wi[:, n_a + 2 * GDN_HEADS:],
        wsb=w_branch_sb[l].astype(BF16), wgdn=w_branch_gdn[l].astype(BF16), wout=w_out[l].astype(BF16),
        conv_w=conv_w[l],
        alog_l=lane_vec(gdn_a_log[l]), dt_l=lane_vec(gdn_dt_bias[l]),
        alog_s=sublane_vec(gdn_a_log[l]), dt_s=sublane_vec(gdn_dt_bias[l]),
        head_gain=gdn_norm_gain[l][None, :],
        dims=(sb_w, conv_dim, gdn_w),
    )


def _run_group(x, weights, ffn_w, past_k, past_v, conv_hist, s0, chunk):
    b, l, d = x.shape
    n = b * l
    s_list, c_list = [], []
    kv_stacks = ()
    for li, w in enumerate(weights):
        sb_w, conv_dim, gdn_w = w["dims"]
        dh = sb_w // SB_HEADS
        tq = min(SB_QUERY_TILE, l)
        td = min(SB_KEY_TILE, l)
        g = w["gains"]
        x2 = _ffn(x.reshape(n, d), g[0], g[1], ffn_w[0], ffn_w[1], li)
        q, k_stack, v_stack, kt, vb, u, z, ba, bat = _inproj(
            x2.reshape(b, l, d), g[2], w["wa"], w["wba"], kv_stacks,
            layer=li, depth=len(weights), sb_w=sb_w, conv_dim=conv_dim, gdn_w=gdn_w, sb_tile=td)
        kv_stacks = (k_stack, v_stack)
        if past_k is None:
            o_sb = _sb_attention(q, kt, vb, kt, vb, tq=tq, td=td, tk=td, causal_past=True)
        else:
            p = past_k.shape[2]
            tk = min(SB_KEY_TILE, p)
            ktp, vp = _cache_layout(past_k.reshape(past_k.shape[:3] + (sb_w,)),
                                    past_v.reshape(past_v.shape[:3] + (sb_w,)), layer=li, tk=tk)
            o_sb = _sb_attention(q, kt, vb, ktp, vp, tq=tq, td=td, tk=tk, causal_past=False)
        o_gdn, s_fin, conv_new = _gdn(u, z, ba, bat, conv_hist[li], s0[li], w["conv_w"],
                                      w["alog_l"], w["dt_l"], w["alog_s"], w["dt_s"], w["head_gain"],
                                      chunk=chunk)
        x2 = _merge(x2, o_sb.reshape(n, sb_w), o_gdn.reshape(n, gdn_w), g[2], g[3],
                    w["wgate"], w["wsb"], w["wgdn"], w["wout"])
        x2 = _ffn(x2, g[4], g[5], ffn_w[2], ffn_w[3], li)
        x = x2.reshape(b, l, d)
        s_list.append(s_fin)
        c_list.append(conv_new)
    k_all, v_all = (t.reshape(len(weights), b, l, SB_HEADS, dh) for t in kv_stacks)
    return x, k_all, v_all, jnp.stack(s_list), jnp.stack(c_list)


def kernel(x_prompt, x_sample, cache_sb_k, cache_sb_v, state_gdn, state_conv, norm_gains,
           w_ffn1_up, w_ffn1_down, w_in, conv_w, gdn_a_log, gdn_dt_bias, gdn_norm_gain,
           w_branch_sb, w_branch_gdn, w_out, w_ffn2_up, w_ffn2_down):
    depth = w_in.shape[0]
    ffn_w = tuple(_to_bf16(w) for w in (w_ffn1_up, w_ffn1_down, w_ffn2_up, w_ffn2_down))
    w_in_bf = _to_bf16(w_in)
    weights = [_layer_weights(l, norm_gains, w_in_bf, conv_w, gdn_a_log, gdn_dt_bias, gdn_norm_gain,
                              w_branch_sb, w_branch_gdn, w_out) for l in range(depth)]
    bp = x_prompt.shape[0]
    chunk_prompt = 64
    zero_conv = jnp.zeros((depth, bp) + state_conv.shape[2:], state_conv.dtype)
    zero_state = jnp.zeros((depth, bp) + state_gdn.shape[2:], state_gdn.dtype)
    y_p, pk, pv, ps, pc = _run_group(x_prompt, weights, ffn_w, None, None, zero_conv, zero_state, chunk_prompt)
    y_s, sk, sv, ss, sc = _run_group(x_sample, weights, ffn_w, cache_sb_k, cache_sb_v, state_conv, state_gdn,
                                     x_sample.shape[1])
    return (y_p, y_s, pk, pv, ps, pc, sk, sv, ss, sc)
```

```python
import functools

import jax
import jax.numpy as jnp
from jax import lax
from jax.experimental import pallas as pl
from jax.experimental.pallas import tpu as pltpu

F32 = jnp.float32
BF16 = jnp.bfloat16

NORM_EPS = 1e-6
LOG2E = 1.4426950408889634
SB_HEADS = 8
GDN_HEADS = 4
LANES = 128
VMEM_LIMIT_BYTES = 56 * 1024 * 1024

ROW_TILE = 512
CAST_ROWS = 256
FF_CHUNK = 256
GDN_ROWS = 512
SB_KEY_TILE = 256
SB_QUERY_TILE = 256
SB_GROUP_LANES = 512
SB_ZERO_WEIGHT_LOG = 110.0
SB_MASKED_SCORE = -1e30
SB_NO_PAST_CARRY = 1e30


def _params(*sem):
    return pltpu.CompilerParams(dimension_semantics=sem, vmem_limit_bytes=VMEM_LIMIT_BYTES)


def _const_spec(shape):
    zeros = (0,) * len(shape)
    return pl.BlockSpec(shape, lambda *_: zeros)


def _rms(x, gain):
    ms = jnp.mean(x * x, axis=-1, keepdims=True)
    return x * lax.rsqrt(ms + NORM_EPS) * gain


def _dot(a, b):
    return jnp.dot(a, b, preferred_element_type=F32)


def _dot_nt(a, b):
    return lax.dot_general(a, b, (((1,), (1,)), ((), ())), preferred_element_type=F32)


def _dot_tn(a, b):
    return lax.dot_general(a, b, (((0,), (0,)), ((), ())), preferred_element_type=F32)


def _split3(x):
    hi = x.astype(BF16)
    r = x - hi.astype(F32)
    mid = r.astype(BF16)
    lo = (r - mid.astype(F32)).astype(BF16)
    return hi, mid, lo


def _dot_f32_exactrhs(a, b_bf16):
    hi, mid, lo = _split3(a)
    return _dot(hi, b_bf16) + _dot(mid, b_bf16) + _dot(lo, b_bf16)


def _dot_exactlhs_f32(a_bf16, b):
    hi, mid, lo = _split3(b)
    return _dot(a_bf16, hi) + _dot(a_bf16, mid) + _dot(a_bf16, lo)


def _ffn_kernel(x_ref, gin_ref, gout_ref, wup_ref, wd_ref, o_ref, act_ref):
    x = x_ref[...]
    h = _rms(x, gin_ref[...]).astype(BF16)
    d_ff = wd_ref.shape[1]
    for c in range(d_ff // FF_CHUNK):
        sl = slice(c * FF_CHUNK, (c + 1) * FF_CHUNK)
        g = _dot(h, wup_ref[0, :, sl])
        u = _dot(h, wup_ref[0, :, d_ff + c * FF_CHUNK:d_ff + (c + 1) * FF_CHUNK])
        act_ref[:, sl] = (g * jax.nn.sigmoid(g) * u).astype(BF16)
    y = _dot(act_ref[...], wd_ref[0])
    o_ref[...] = x + 0.5 * _rms(y, gout_ref[...])


def _ffn(x2, g_in, g_out, w_up, w_down, layer):
    n, d = x2.shape
    d_ff = w_down.shape[1]
    tm = min(ROW_TILE, n)
    row = pl.BlockSpec((tm, d), lambda i: (i, 0))
    return pl.pallas_call(
        _ffn_kernel,
        grid=(n // tm,),
        in_specs=[row, _const_spec((1, d)), _const_spec((1, d)),
                  pl.BlockSpec((1, d, 2 * d_ff), lambda i: (layer, 0, 0)),
                  pl.BlockSpec((1, d_ff, d), lambda i: (layer, 0, 0))],
        out_specs=row,
        out_shape=jax.ShapeDtypeStruct((n, d), F32),
        scratch_shapes=[pltpu.VMEM((tm, d_ff), BF16)],
        compiler_params=_params("parallel"),
        name="ffn_half_step",
    )(x2, g_in, g_out, w_up, w_down)


def _cast_kernel(w_ref, o_ref):
    o_ref[...] = w_ref[...].astype(o_ref.dtype)


def _to_bf16(w):
    depth, rows, cols = w.shape
    tr = min(rows, CAST_ROWS)
    assert rows % tr == 0
    spec = pl.BlockSpec((1, tr, cols), lambda li, ri: (li, ri, 0))
    return pl.pallas_call(
        _cast_kernel,
        grid=(depth, rows // tr),
        in_specs=[spec],
        out_specs=spec,
        out_shape=jax.ShapeDtypeStruct(w.shape, BF16),
        compiler_params=_params("parallel", "parallel"),
        name="weights_to_bf16",
    )(w)


def _inproj_kernel(x_ref, g_ref, wa_ref, wba_ref, *refs, sb_w, conv_dim, gdn_w, sb_tile, q_scale):
    q_ref, k_ref, v_ref, kt_ref, vb_ref, u_ref, z_ref, ba_ref, bat_ref = refs[-9:]
    h = _rms(x_ref[0], g_ref[...]).astype(BF16)
    tm = h.shape[0]

    def proj(lo, width):
        return _dot(h, wa_ref[:, lo:lo + width])

    q_ref[0] = (proj(0, sb_w) * q_scale).astype(BF16)
    k = proj(sb_w, sb_w)
    k_ref[0, 0] = k
    for later in range(1, k_ref.shape[0]):
        k_ref[later, 0] = jnp.zeros_like(k)
        v_ref[later, 0] = jnp.zeros_like(k)
    kt = k.T.astype(BF16)
    for hp in range(sb_w // SB_GROUP_LANES):
        for s in range(tm // sb_tile):
            kt_ref[0, hp, s] = kt[hp * SB_GROUP_LANES:(hp + 1) * SB_GROUP_LANES,
                                  s * sb_tile:(s + 1) * sb_tile]
    v = proj(2 * sb_w, sb_w)
    v_ref[0, 0] = v
    vb_ref[0] = v.astype(BF16)
    u_ref[0] = proj(3 * sb_w, conv_dim)
    z_ref[0] = proj(3 * sb_w + conv_dim, gdn_w)
    ba = _dot(h, wba_ref[...])
    ba_ref[0] = ba
    bat_ref[0] = ba.T[:8]


def _inproj(x, gain, wa, wba, kv_stacks, *, layer, depth, sb_w, conv_dim, gdn_w, sb_tile):
    b, l, d = x.shape
    tm = min(ROW_TILE, l)
    n_hp = sb_w // SB_GROUP_LANES
    kern = functools.partial(_inproj_kernel, sb_w=sb_w, conv_dim=conv_dim, gdn_w=gdn_w,
                             sb_tile=sb_tile,
                             q_scale=float((sb_w // SB_HEADS) ** -0.5))

    def rows(width):
        return pl.BlockSpec((1, tm, width), lambda bi, ti: (bi, ti, 0))

    out_shape = (
        jax.ShapeDtypeStruct((b, l, sb_w), BF16),
        jax.ShapeDtypeStruct((depth, b, l, sb_w), F32),
        jax.ShapeDtypeStruct((depth, b, l, sb_w), F32),
        jax.ShapeDtypeStruct((b, n_hp, l // sb_tile, SB_GROUP_LANES, sb_tile), BF16),
        jax.ShapeDtypeStruct((b, l, sb_w), BF16),
        jax.ShapeDtypeStruct((b, l, conv_dim), F32),
        jax.ShapeDtypeStruct((b, l, gdn_w), F32),
        jax.ShapeDtypeStruct((b, l, LANES), F32),
        jax.ShapeDtypeStruct((b, 8, l), F32),
    )
    if kv_stacks:
        layer_rows = pl.BlockSpec((1, 1, tm, sb_w), lambda bi, ti: (layer, bi, ti, 0))
    else:
        assert layer == 0
        layer_rows = pl.BlockSpec((depth, 1, tm, sb_w), lambda bi, ti: (0, bi, ti, 0))
    out_specs = (
        rows(sb_w), layer_rows, layer_rows,
        pl.BlockSpec((1, n_hp, tm // sb_tile, SB_GROUP_LANES, sb_tile), lambda bi, ti: (bi, 0, ti, 0, 0)),
        rows(sb_w), rows(conv_dim), rows(gdn_w), rows(LANES),
        pl.BlockSpec((1, 8, tm), lambda bi, ti: (bi, 0, ti)),
    )
    return pl.pallas_call(
        kern,
        grid=(b, l // tm),
        in_specs=[rows(d), _const_spec((1, d)), _const_spec(wa.shape), _const_spec(wba.shape)]
        + [pl.BlockSpec(memory_space=pl.ANY)] * len(kv_stacks),
        out_specs=out_specs,
        out_shape=out_shape,
        input_output_aliases={4 + i: 1 + i for i in range(len(kv_stacks))},
        compiler_params=_params("parallel", "parallel"),
        name="in_projection",
    )(x, gain, wa, wba, *kv_stacks)


def _sb_kernel(q_ref, ktd_ref, vd_ref, ktp_ref, vp_ref, o_ref, *, tq, td, nd, tk, heads, dh,
               causal_past, n_past_static):
    qi = pl.program_id(2)

    def later_key_matrix(n):
        r = lax.broadcasted_iota(jnp.int32, (n, n), 0)
        c = lax.broadcasted_iota(jnp.int32, (n, n), 1)
        return (r > c).astype(BF16)

    def sweep(blocks, st):
        pairs = [(blk, hh) for blk in blocks for hh in range(heads)]
        scores = []
        for blk, hh in pairs:
            lanes = slice(hh * dh, (hh + 1) * dh)
            z = _dot(q_all[:, lanes], blk["kt"][lanes, :])
            if blk["visible"] is not None:
                z = jnp.where(blk["visible"], z, SB_MASKED_SCORE)
            nlk = jnp.maximum(z, 0.0) + jnp.log(1.0 + jnp.exp2(jnp.abs(z) * (-LOG2E)))
            scores.append((z - nlk, nlk.astype(BF16), jnp.sum(nlk, axis=1, keepdims=True)))
        csums = [_dot(nlk_b, blk["u"]) for (blk, _), (_, nlk_b, _) in zip(pairs, scores)]
        st = list(st)
        for (blk, hh), (log_beta, _, row_sum), csum in zip(pairs, scores, csums):
            lanes = slice(hh * dh, (hh + 1) * dh)
            carry, acc = st[2 * hh], st[2 * hh + 1]
            if carry is not None and blk.get("carry_bias") is not None:
                carry = carry + blk["carry_bias"]
            log_a = log_beta - csum if carry is None else log_beta - csum - carry
            pv = _dot(jnp.exp(log_a).astype(BF16), blk["v"][:, lanes])
            st[2 * hh + 1] = pv if acc is None else acc + pv
            st[2 * hh] = row_sum if carry is None else carry + row_sum
        return st

    q_all = q_ref[0]
    state = [None] * (2 * heads)

    u_diag = later_key_matrix(td)
    u_past = u_diag if tk == td else later_key_matrix(tk)
    rows_i = lax.broadcasted_iota(jnp.int32, (tq, td), 0)
    cols_i = lax.broadcasted_iota(jnp.int32, (tq, td), 1)
    n_past = qi * (tq // tk) if causal_past else n_past_static

    def past_block(j, carry_bias=None):
        row0 = pl.multiple_of(j * tk, tk)
        return dict(kt=ktp_ref[0, 0, j], v=vp_ref[0, pl.ds(row0, tk), :], visible=None, u=u_past,
                    carry_bias=carry_bias)

    first = [dict(kt=ktd_ref[0, 0, sd], v=vd_ref[0, sd * td:(sd + 1) * td, :],
                  visible=cols_i + sd * td < rows_i, u=u_diag) for sd in reversed(range(nd))]
    no_past_bias = jnp.where(n_past > 0, 0.0, SB_NO_PAST_CARRY) if causal_past else None
    first.append(past_block(jnp.maximum(n_past - 1, 0), no_past_bias))
    state = sweep(first, state)

    def min_carry(st):
        m = st[0]
        for hh in range(1, heads):
            m = jnp.minimum(m, st[2 * hh])
        return jnp.min(m)

    def cond(loop):
        i, smallest, _ = loop
        return jnp.logical_and(i < n_past, smallest < SB_ZERO_WEIGHT_LOG)

    def body(loop):
        i, _, st = loop
        st = sweep([past_block(n_past - 1 - i)], st)
        return i + 1, min_carry(st), tuple(st)

    _, _, st = lax.while_loop(cond, body, (jnp.int32(1), min_carry(state), tuple(state)))
    o_ref[0] = jnp.concatenate([st[2 * hh + 1] for hh in range(heads)], axis=1).astype(BF16)


def _sb_attention(q, kt_diag, v_diag, kt_past, v_past, *, tq, td, tk, causal_past):
    b, l, w = q.shape
    n_g, hw = kt_diag.shape[1], kt_diag.shape[3]
    n_past_blocks = kt_past.shape[2]
    assert tq % td == 0 and (not causal_past or tq % tk == 0)
    p = v_past.shape[1]
    dh = w // SB_HEADS
    kern = functools.partial(_sb_kernel, tq=tq, td=td, nd=tq // td, tk=tk, heads=hw // dh, dh=dh,
                             causal_past=causal_past, n_past_static=n_past_blocks)
    return pl.pallas_call(
        kern,
        grid=(b, n_g, l // tq),
        in_specs=[
            pl.BlockSpec((1, tq, hw), lambda bi, hp, qi: (bi, qi, hp)),
            pl.BlockSpec((1, 1, tq // td, hw, td), lambda bi, hp, qi: (bi, hp, qi, 0, 0)),
            pl.BlockSpec((1, tq, hw), lambda bi, hp, qi: (bi, qi, hp)),
            pl.BlockSpec((1, 1, n_past_blocks, hw, tk), lambda bi, hp, qi: (bi, hp, 0, 0, 0),
                         pipeline_mode=pl.Buffered(1)),
            pl.BlockSpec((1, p, hw), lambda bi, hp, qi: (bi, 0, hp), pipeline_mode=pl.Buffered(1)),
        ],
        out_specs=pl.BlockSpec((1, tq, hw), lambda bi, hp, qi: (bi, qi, hp)),
        out_shape=jax.ShapeDtypeStruct((b, l, w), BF16),
        compiler_params=_params("parallel", "parallel", "arbitrary"),
        name="stick_breaking_attention",
    )(q, kt_diag, v_diag, kt_past, v_past)


def _gdn_kernel(u_ref, z_ref, ba_ref, bat_ref, hist_ref, s0_ref, cw_ref,
                alog_l_ref, dt_l_ref, alog_s_ref, dt_s_ref, hg_ref,
                o_ref, s_ref, cnew_ref, ext_ref, *, rows, chunk, width, n_taps):
    step = pl.program_id(1)
    n_steps = pl.num_programs(1)
    dk = width // GDN_HEADS
    n_sub = rows // chunk
    pad = 8
    n_hist = n_taps - 1

    @pl.when(step == 0)
    def _():
        ext_ref[0:pad, :] = jnp.zeros((pad, ext_ref.shape[1]), F32)
        ext_ref[pad - n_hist:pad, :] = hist_ref[0]
        s_ref[...] = s0_ref[...]

    ext_ref[pad:pad + rows, :] = u_ref[0]
    y = ext_ref[pad:pad + rows, :] * cw_ref[n_hist:n_taps, :]
    for i in reversed(range(n_hist)):
        y = y + ext_ref[pad - n_hist + i:pad - n_hist + i + rows, :] * cw_ref[i:i + 1, :]
    new_tail = ext_ref[rows:rows + pad, :]
    ext_ref[0:pad, :] = new_tail

    @pl.when(step == n_steps - 1)
    def _():
        cnew_ref[0] = new_tail[pad - n_hist:pad, :]

    qkv = y * jax.nn.sigmoid(y)

    heads = range(GDN_HEADS)
    qs, ks, vs = [], [], []
    for h in heads:
        q = qkv[:, h * dk:(h + 1) * dk]
        k = qkv[:, width + h * dk:width + (h + 1) * dk]
        qs.append(q * lax.rsqrt(jnp.sum(q * q, axis=-1, keepdims=True) + NORM_EPS) * float(dk ** -0.5))
        ks.append(k * lax.rsqrt(jnp.sum(k * k, axis=-1, keepdims=True) + NORM_EPS))
        vs.append(qkv[:, 2 * width + h * dk:2 * width + (h + 1) * dk])

    r_i = lax.broadcasted_iota(jnp.int32, (chunk, chunk), 0)
    c_i = lax.broadcasted_iota(jnp.int32, (chunk, chunk), 1)
    incl = r_i >= c_i
    strict = r_i > c_i
    eye = (r_i == c_i).astype(F32)
    rr = lax.broadcasted_iota(jnp.int32, (rows, rows), 0)
    cc = lax.broadcasted_iota(jnp.int32, (rows, rows), 1)
    same_chunk = (rr // chunk) == (cc // chunk)
    lower_incl = (same_chunk & (rr >= cc)).astype(BF16)
    upper_incl = (same_chunk & (rr <= cc)).astype(BF16)

    def softplus(t):
        return jnp.maximum(t, 0.0) + jnp.log(1.0 + jnp.exp(-jnp.abs(t)))

    ba = ba_ref[0]
    g_cols = -jnp.exp(alog_l_ref[...]) * softplus(ba + dt_l_ref[...])
    gcum_cols = _dot_exactlhs_f32(lower_incl, g_cols)
    beta_cols = jax.nn.sigmoid(ba)
    bat = bat_ref[0]
    g_rows = -jnp.exp(alog_s_ref[...]) * softplus(bat + dt_s_ref[...])
    gcum_rows = _dot_f32_exactrhs(g_rows, upper_incl)

    pairs = [(ci, h) for ci in range(n_sub) for h in heads]

    pre = {}
    for ci, h in pairs:
        rs = slice(ci * chunk, (ci + 1) * chunk)
        gc_col = gcum_cols[rs, GDN_HEADS + h:GDN_HEADS + h + 1]
        gc_row = gcum_rows[GDN_HEADS + h:GDN_HEADS + h + 1, rs]
        gc_last = gc_row[:, chunk - 1:chunk]
        beta = beta_cols[rs, h:h + 1]
        gamma = jnp.where(incl, jnp.exp(gc_col - gc_row), 0.0)
        decay_in = jnp.exp(gc_col)
        q, k, v = qs[h][rs], ks[h][rs], vs[h][rs]
        kb = k * beta
        kq = _dot_nt(jnp.concatenate([kb, q], axis=0).astype(BF16), k.astype(BF16))
        pre[ci, h] = dict(
            n=jnp.where(strict, -(kq[:chunk] * gamma), 0.0),
            qk=(kq[chunk:] * gamma).astype(BF16),
            rhs=jnp.concatenate([v * beta, kb * decay_in], axis=1).astype(BF16),
            q_dec=(q * decay_in).astype(BF16),
            k_end=(k * jnp.exp(gc_last - gc_col)).astype(BF16),
            chunk_decay=jnp.exp(gc_last))

    n_rounds = max(1, (chunk - 1).bit_length())
    m_pow = {p: pre[p]["n"] for p in pairs}
    t_inv = {p: eye + pre[p]["n"] for p in pairs}
    for i in range(n_rounds):
        last = i == n_rounds - 1
        for p in pairs:
            m_bf = m_pow[p].astype(BF16)
            if i == 0:
                m_pow[p] = _dot(m_bf, m_bf)
            elif last:
                t_inv[p] = t_inv[p] + _dot(t_inv[p].astype(BF16), m_bf)
            else:
                both = _dot(jnp.concatenate([m_bf, t_inv[p].astype(BF16)], axis=0), m_bf)
                m_pow[p] = both[:chunk]
                t_inv[p] = t_inv[p] + both[chunk:]

    sol = {p: _dot(t_inv[p].astype(BF16), pre[p]["rhs"]) for p in pairs}

    state = [s_ref[0, h] for h in heads]
    o_rows = [[None] * n_sub for _ in heads]
    for ci in range(n_sub):
        ws = [_dot(jnp.concatenate([sol[ci, h][:, dk:].astype(BF16), pre[ci, h]["q_dec"]], axis=0),
                   state[h].astype(BF16)) for h in heads]
        v_new = [(sol[ci, h][:, :dk] - ws[h][:chunk]).astype(BF16) for h in heads]
        for h in heads:
            o_rows[h][ci] = ws[h][chunk:] + _dot(pre[ci, h]["qk"], v_new[h])
        state = [state[h] * pre[ci, h]["chunk_decay"] + _dot_tn(pre[ci, h]["k_end"], v_new[h])
                 for h in heads]
    for h in heads:
        s_ref[0, h] = state[h]

    z = z_ref[0]
    outs = []
    for h in heads:
        o_h = o_rows[h][0] if n_sub == 1 else jnp.concatenate(o_rows[h], axis=0)
        z_h = z[:, h * dk:(h + 1) * dk]
        outs.append(_rms(o_h, hg_ref[...]) * (z_h * jax.nn.sigmoid(z_h)))
    o_ref[0] = jnp.concatenate(outs, axis=1).astype(BF16)


def _gdn(u, z, ba, bat, hist, s0, conv_w, alog_l, dt_l, alog_s, dt_s, head_gain, *, chunk):
    b, l, conv_dim = u.shape
    width = conv_dim // 3
    dk = width // GDN_HEADS
    n_taps = conv_w.shape[0]
    rows = min(GDN_ROWS, l)
    kern = functools.partial(_gdn_kernel, rows=rows, chunk=chunk, width=width, n_taps=n_taps)

    def row_block(wd):
        return pl.BlockSpec((1, rows, wd), lambda bi, si: (bi, si, 0))

    state_spec = pl.BlockSpec((1, GDN_HEADS, dk, dk), lambda bi, si: (bi, 0, 0, 0))
    hist_spec = pl.BlockSpec((1, n_taps - 1, conv_dim), lambda bi, si: (bi, 0, 0))
    return pl.pallas_call(
        kern,
        grid=(b, l // rows),
        in_specs=[row_block(conv_dim), row_block(width), row_block(LANES),
                  pl.BlockSpec((1, 8, rows), lambda bi, si: (bi, 0, si)),
                  hist_spec, state_spec, _const_spec(conv_w.shape),
                  _const_spec((1, LANES)), _const_spec((1, LANES)),
                  _const_spec((8, 1)), _const_spec((8, 1)), _const_spec((1, dk))],
        out_specs=(row_block(width), state_spec, hist_spec),
        out_shape=(jax.ShapeDtypeStruct((b, l, width), BF16),
                   jax.ShapeDtypeStruct(s0.shape, F32),
                   jax.ShapeDtypeStruct(hist.shape, F32)),
        scratch_shapes=[pltpu.VMEM((rows + 8, conv_dim), F32)],
        compiler_params=_params("parallel", "arbitrary"),
        name="gated_delta_rule",
    )(u, z, ba, bat, hist, s0, conv_w, alog_l, dt_l, alog_s, dt_s, head_gain)


def _merge_kernel(x_ref, osb_ref, ogdn_ref, gin_ref, gout_ref, wgate_ref, wsb_ref, wgdn_ref, wout_ref, o_ref):
    x = x_ref[...]
    d = x.shape[1]
    h = _rms(x, gin_ref[...]).astype(BF16)
    gate_sb = jax.nn.sigmoid(_dot(h, wgate_ref[:, :d]))
    merged = gate_sb * _dot(osb_ref[...], wsb_ref[...])
    gate_gdn = jax.nn.sigmoid(_dot(h, wgate_ref[:, d:]))
    merged = merged + gate_gdn * _dot(ogdn_ref[...], wgdn_ref[...])
    m = _dot(merged.astype(BF16), wout_ref[...])
    o_ref[...] = x + _rms(m, gout_ref[...])


def _merge(x2, osb2, ogdn2, g_in, g_out, wgate, wsb, wgdn, wout):
    n, d = x2.shape
    tm = min(ROW_TILE, n)

    def rows(wd):
        return pl.BlockSpec((tm, wd), lambda i: (i, 0))

    return pl.pallas_call(
        _merge_kernel,
        grid=(n // tm,),
        in_specs=[rows(d), rows(osb2.shape[1]), rows(ogdn2.shape[1]),
                  _const_spec((1, d)), _const_spec((1, d)),
                  _const_spec(wgate.shape), _const_spec(wsb.shape), _const_spec(wgdn.shape),
                  _const_spec(wout.shape)],
        out_specs=rows(d),
        out_shape=jax.ShapeDtypeStruct((n, d), F32),
        compiler_params=_params("parallel"),
        name="branch_merge",
    )(x2, osb2, ogdn2, g_in, g_out, wgate, wsb, wgdn, wout)


def _layer_weights(l, norm_gains, w_in_bf, conv_w, gdn_a_log, gdn_dt_bias, gdn_norm_gain,
                   w_branch_sb, w_branch_gdn, w_out):
    d = w_in_bf.shape[1]
    sb_w = w_branch_sb.shape[1]
    gdn_w = w_branch_gdn.shape[1]
    conv_dim = conv_w.shape[2]
    n_a = 3 * sb_w + conv_dim + gdn_w
    wi = w_in_bf[l]
    wba = jnp.zeros((d, LANES), BF16).at[:, :2 * GDN_HEADS].set(wi[:, n_a:n_a + 2 * GDN_HEADS])

    def lane_vec(p):
        return jnp.zeros((1, LANES), F32).at[0, GDN_HEADS:2 * GDN_HEADS].set(p)

    def sublane_vec(p):
        return jnp.zeros((8, 1), F32).at[GDN_HEADS:2 * GDN_HEADS, 0].set(p)

    return dict(
        gains=[norm_gains[l, i][None, :] for i in range(6)],
        wa=wi[:, :n_a],
        wba=wba,
        wgate=wi[:, n_a + 2 * GDN_HEADS:],
        wsb=w_branch_sb[l].astype(BF16), wgdn=w_branch_gdn[l].astype(BF16), wout=w_out[l].astype(BF16),
        conv_w=conv_w[l],
        alog_l=lane_vec(gdn_a_log[l]), dt_l=lane_vec(gdn_dt_bias[l]),
        alog_s=sublane_vec(gdn_a_log[l]), dt_s=sublane_vec(gdn_dt_bias[l]),
        head_gain=gdn_norm_gain[l][None, :],
        dims=(sb_w, conv_dim, gdn_w),
    )


def _run_group(x, weights, ffn_w, past_k, past_v, conv_hist, s0, chunk):
    b, l, d = x.shape
    n = b * l
    s_list, c_list = [], []
    kv_stacks = ()
    for li, w in enumerate(weights):
        sb_w, conv_dim, gdn_w = w["dims"]
        dh = sb_w // SB_HEADS
        tq = min(SB_QUERY_TILE, l)
        td = min(SB_KEY_TILE, l)
        g = w["gains"]
        x2 = _ffn(x.reshape(n, d), g[0], g[1], ffn_w[0], ffn_w[1], li)
        q, k_stack, v_stack, kt, vb, u, z, ba, bat = _inproj(
            x2.reshape(b, l, d), g[2], w["wa"], w["wba"], kv_stacks,
            layer=li, depth=len(weights), sb_w=sb_w, conv_dim=conv_dim, gdn_w=gdn_w, sb_tile=td)
        kv_stacks = (k_stack, v_stack)
        if past_k is None:
            o_sb = _sb_attention(q, kt, vb, kt, vb, tq=tq, td=td, tk=td, causal_past=True)
        else:
            p = past_k.shape[2]
            tk = min(SB_KEY_TILE, p)
            ktp = past_k[li].astype(BF16).reshape(b, p // tk, tk, sb_w // SB_GROUP_LANES, SB_GROUP_LANES)
            ktp = ktp.transpose(0, 3, 1, 4, 2)
            vp = past_v[li].astype(BF16).reshape(b, p, sb_w)
            o_sb = _sb_attention(q, kt, vb, ktp, vp, tq=tq, td=td, tk=tk, causal_past=False)
        o_gdn, s_fin, conv_new = _gdn(u, z, ba, bat, conv_hist[li], s0[li], w["conv_w"],
                                      w["alog_l"], w["dt_l"], w["alog_s"], w["dt_s"], w["head_gain"],
                                      chunk=chunk)
        x2 = _merge(x2, o_sb.reshape(n, sb_w), o_gdn.reshape(n, gdn_w), g[2], g[3],
                    w["wgate"], w["wsb"], w["wgdn"], w["wout"])
        x2 = _ffn(x2, g[4], g[5], ffn_w[2], ffn_w[3], li)
        x = x2.reshape(b, l, d)
        s_list.append(s_fin)
        c_list.append(conv_new)
    k_all, v_all = (t.reshape(len(weights), b, l, SB_HEADS, dh) for t in kv_stacks)
    return x, k_all, v_all, jnp.stack(s_list), jnp.stack(c_list)


def kernel(x_prompt, x_sample, cache_sb_k, cache_sb_v, state_gdn, state_conv, norm_gains,
           w_ffn1_up, w_ffn1_down, w_in, conv_w, gdn_a_log, gdn_dt_bias, gdn_norm_gain,
           w_branch_sb, w_branch_gdn, w_out, w_ffn2_up, w_ffn2_down):
    depth = w_in.shape[0]
    ffn_w = tuple(_to_bf16(w) for w in (w_ffn1_up, w_ffn1_down, w_ffn2_up, w_ffn2_down))
    w_in_bf = _to_bf16(w_in)
    weights = [_layer_weights(l, norm_gains, w_in_bf, conv_w, gdn_a_log, gdn_dt_bias, gdn_norm_gain,
                              w_branch_sb, w_branch_gdn, w_out) for l in range(depth)]
    bp = x_prompt.shape[0]
    chunk_prompt = 64
    zero_conv = jnp.zeros((depth, bp) + state_conv.shape[2:], state_conv.dtype)
    zero_state = jnp.zeros((depth, bp) + state_gdn.shape[2:], state_gdn.dtype)
    y_p, pk, pv, ps, pc = _run_group(x_prompt, weights, ffn_w, None, None, zero_conv, zero_state, chunk_prompt)
    y_s, sk, sv, ss, sc = _run_group(x_sample, weights, ffn_w, cache_sb_k, cache_sb_v, state_conv, state_gdn,
                                     x_sample.shape[1])
    return (y_p, y_s, pk, pv, ps, pc, sk, sv, ss, sc)
```

```python
import functools

import jax
import jax.numpy as jnp
from jax import lax
from jax.experimental import pallas as pl
from jax.experimental.pallas import tpu as pltpu

F32 = jnp.float32
BF16 = jnp.bfloat16

NORM_EPS = 1e-6
LOG2E = 1.4426950408889634
SB_HEADS = 8
GDN_HEADS = 4
LANES = 128
VMEM_LIMIT_BYTES = 56 * 1024 * 1024

ROW_TILE = 512
CAST_ROWS = 256
FF_CHUNK = 256
GDN_ROWS = 512
SB_KEY_TILE = 256
SB_QUERY_TILE = 256
SB_GROUP_LANES = 512
SB_ZERO_WEIGHT_LOG = 110.0
SB_MASKED_SCORE = -1e30
SB_NO_PAST_CARRY = 1e30


def _params(*sem):
    return pltpu.CompilerParams(dimension_semantics=sem, vmem_limit_bytes=VMEM_LIMIT_BYTES)


def _const_spec(shape):
    zeros = (0,) * len(shape)
    return pl.BlockSpec(shape, lambda *_: zeros)


def _rms(x, gain):
    ms = jnp.mean(x * x, axis=-1, keepdims=True)
    return x * lax.rsqrt(ms + NORM_EPS) * gain


def _dot(a, b):
    return jnp.dot(a, b, preferred_element_type=F32)


def _dot_nt(a, b):
    return lax.dot_general(a, b, (((1,), (1,)), ((), ())), preferred_element_type=F32)


def _dot_tn(a, b):
    return lax.dot_general(a, b, (((0,), (0,)), ((), ())), preferred_element_type=F32)


def _split3(x):
    hi = x.astype(BF16)
    r = x - hi.astype(F32)
    mid = r.astype(BF16)
    lo = (r - mid.astype(F32)).astype(BF16)
    return hi, mid, lo


def _dot_f32_exactrhs(a, b_bf16):
    hi, mid, lo = _split3(a)
    return _dot(hi, b_bf16) + _dot(mid, b_bf16) + _dot(lo, b_bf16)


def _dot_exactlhs_f32(a_bf16, b):
    hi, mid, lo = _split3(b)
    return _dot(a_bf16, hi) + _dot(a_bf16, mid) + _dot(a_bf16, lo)


def _ffn_kernel(x_ref, gin_ref, gout_ref, wup_ref, wd_ref, o_ref, act_ref):
    x = x_ref[...]
    h = _rms(x, gin_ref[...]).astype(BF16)
    d_ff = wd_ref.shape[1]
    for c in range(d_ff // FF_CHUNK):
        sl = slice(c * FF_CHUNK, (c + 1) * FF_CHUNK)
        g = _dot(h, wup_ref[0, :, sl])
        u = _dot(h, wup_ref[0, :, d_ff + c * FF_CHUNK:d_ff + (c + 1) * FF_CHUNK])
        act_ref[:, sl] = (g * jax.nn.sigmoid(g) * u).astype(BF16)
    y = _dot(act_ref[...], wd_ref[0])
    o_ref[...] = x + 0.5 * _rms(y, gout_ref[...])


def _ffn(x2, g_in, g_out, w_up, w_down, layer):
    n, d = x2.shape
    d_ff = w_down.shape[1]
    tm = min(ROW_TILE, n)
    row = pl.BlockSpec((tm, d), lambda i: (i, 0))
    return pl.pallas_call(
        _ffn_kernel,
        grid=(n // tm,),
        in_specs=[row, _const_spec((1, d)), _const_spec((1, d)),
                  pl.BlockSpec((1, d, 2 * d_ff), lambda i: (layer, 0, 0)),
                  pl.BlockSpec((1, d_ff, d), lambda i: (layer, 0, 0))],
        out_specs=row,
        out_shape=jax.ShapeDtypeStruct((n, d), F32),
        scratch_shapes=[pltpu.VMEM((tm, d_ff), BF16)],
        compiler_params=_params("parallel"),
        name="ffn_half_step",
    )(x2, g_in, g_out, w_up, w_down)


def _cast_kernel(w_ref, o_ref):
    o_ref[...] = w_ref[...].astype(o_ref.dtype)


def _to_bf16(w):
    depth, rows, cols = w.shape
    tr = min(rows, CAST_ROWS)
    assert rows % tr == 0
    spec = pl.BlockSpec((1, tr, cols), lambda li, ri: (li, ri, 0))
    return pl.pallas_call(
        _cast_kernel,
        grid=(depth, rows // tr),
        in_specs=[spec],
        out_specs=spec,
        out_shape=jax.ShapeDtypeStruct(w.shape, BF16),
        compiler_params=_params("parallel", "parallel"),
        name="weights_to_bf16",
    )(w)


def _inproj_kernel(x_ref, g_ref, wa_ref, wba_ref, *refs, sb_w, conv_dim, gdn_w, sb_tile, q_scale):
    q_ref, k_ref, v_ref, kt_ref, vb_ref, u_ref, z_ref, ba_ref, bat_ref = refs[-9:]
    h = _rms(x_ref[0], g_ref[...]).astype(BF16)
    tm = h.shape[0]

    def proj(lo, width):
        return _dot(h, wa_ref[:, lo:lo + width])

    q_ref[0] = (proj(0, sb_w) * q_scale).astype(BF16)
    k = proj(sb_w, sb_w)
    k_ref[0, 0] = k
    for later in range(1, k_ref.shape[0]):
        k_ref[later, 0] = jnp.zeros_like(k)
        v_ref[later, 0] = jnp.zeros_like(k)
    kt = k.T.astype(BF16)
    for hp in range(sb_w // SB_GROUP_LANES):
        for s in range(tm // sb_tile):
            kt_ref[0, hp, s] = kt[hp * SB_GROUP_LANES:(hp + 1) * SB_GROUP_LANES,
                                  s * sb_tile:(s + 1) * sb_tile]
    v = proj(2 * sb_w, sb_w)
    v_ref[0, 0] = v
    vb_ref[0] = v.astype(BF16)
    u_ref[0] = proj(3 * sb_w, conv_dim)
    z_ref[0] = proj(3 * sb_w + conv_dim, gdn_w)
    ba = _dot(h, wba_ref[...])
    ba_ref[0] = ba
    bat_ref[0] = ba.T[:8]


def _inproj(x, gain, wa, wba, kv_stacks, *, layer, depth, sb_w, conv_dim, gdn_w, sb_tile):
    b, l, d = x.shape
    tm = min(ROW_TILE, l)
    n_hp = sb_w // SB_GROUP_LANES
    kern = functools.partial(_inproj_kernel, sb_w=sb_w, conv_dim=conv_dim, gdn_w=gdn_w,
                             sb_tile=sb_tile,
                             q_scale=float((sb_w // SB_HEADS) ** -0.5))

    def rows(width):
        return pl.BlockSpec((1, tm, width), lambda bi, ti: (bi, ti, 0))

    out_shape = (
        jax.ShapeDtypeStruct((b, l, sb_w), BF16),
        jax.ShapeDtypeStruct((depth, b, l, sb_w), F32),
        jax.ShapeDtypeStruct((depth, b, l, sb_w), F32),
        jax.ShapeDtypeStruct((b, n_hp, l // sb_tile, SB_GROUP_LANES, sb_tile), BF16),
        jax.ShapeDtypeStruct((b, l, sb_w), BF16),
        jax.ShapeDtypeStruct((b, l, conv_dim), F32),
        jax.ShapeDtypeStruct((b, l, gdn_w), F32),
        jax.ShapeDtypeStruct((b, l, LANES), F32),
        jax.ShapeDtypeStruct((b, 8, l), F32),
    )
    if kv_stacks:
        layer_rows = pl.BlockSpec((1, 1, tm, sb_w), lambda bi, ti: (layer, bi, ti, 0))
    else:
        assert layer == 0
        layer_rows = pl.BlockSpec((depth, 1, tm, sb_w), lambda bi, ti: (0, bi, ti, 0))
    out_specs = (
        rows(sb_w), layer_rows, layer_rows,
        pl.BlockSpec((1, n_hp, tm // sb_tile, SB_GROUP_LANES, sb_tile), lambda bi, ti: (bi, 0, ti, 0, 0)),
        rows(sb_w), rows(conv_dim), rows(gdn_w), rows(LANES),
        pl.BlockSpec((1, 8, tm), lambda bi, ti: (bi, 0, ti)),
    )
    return pl.pallas_call(
        kern,
        grid=(b, l // tm),
        in_specs=[rows(d), _const_spec((1, d)), _const_spec(wa.shape), _const_spec(wba.shape)]
        + [pl.BlockSpec(memory_space=pl.ANY)] * len(kv_stacks),
        out_specs=out_specs,
        out_shape=out_shape,
        input_output_aliases={4 + i: 1 + i for i in range(len(kv_stacks))},
        compiler_params=_params("parallel", "parallel"),
        name="in_projection",
    )(x, gain, wa, wba, *kv_stacks)


def _sb_kernel(q_ref, ktd_ref, vd_ref, ktp_ref, vp_ref, o_ref, *, tq, td, nd, tk, heads, dh,
               causal_past, n_past_static):
    qi = pl.program_id(2)

    def later_key_matrix(n):
        r = lax.broadcasted_iota(jnp.int32, (n, n), 0)
        c = lax.broadcasted_iota(jnp.int32, (n, n), 1)
        return (r > c).astype(BF16)

    def sweep(blocks, st):
        pairs = [(blk, hh) for blk in blocks for hh in range(heads)]
        scores = []
        for blk, hh in pairs:
            lanes = slice(hh * dh, (hh + 1) * dh)
            z = _dot(q_all[:, lanes], blk["kt"][lanes, :])
            if blk["visible"] is not None:
                z = jnp.where(blk["visible"], z, SB_MASKED_SCORE)
            nlk = jnp.maximum(z, 0.0) + jnp.log(1.0 + jnp.exp2(jnp.abs(z) * (-LOG2E)))
            scores.append((z - nlk, nlk.astype(BF16), jnp.sum(nlk, axis=1, keepdims=True)))
        csums = [_dot(nlk_b, blk["u"]) for (blk, _), (_, nlk_b, _) in zip(pairs, scores)]
        st = list(st)
        for (blk, hh), (log_beta, _, row_sum), csum in zip(pairs, scores, csums):
            lanes = slice(hh * dh, (hh + 1) * dh)
            carry, acc = st[2 * hh], st[2 * hh + 1]
            if carry is not None and blk.get("carry_bias") is not None:
                carry = carry + blk["carry_bias"]
            log_a = log_beta - csum if carry is None else log_beta - csum - carry
            pv = _dot(jnp.exp(log_a).astype(BF16), blk["v"][:, lanes])
            st[2 * hh + 1] = pv if acc is None else acc + pv
            st[2 * hh] = row_sum if carry is None else carry + row_sum
        return st

    q_all = q_ref[0]
    state = [None] * (2 * heads)

    u_diag = later_key_matrix(td)
    u_past = u_diag if tk == td else later_key_matrix(tk)
    rows_i = lax.broadcasted_iota(jnp.int32, (tq, td), 0)
    cols_i = lax.broadcasted_iota(jnp.int32, (tq, td), 1)
    n_past = qi * (tq // tk) if causal_past else n_past_static

    def past_block(j, carry_bias=None):
        row0 = pl.multiple_of(j * tk, tk)
        return dict(kt=ktp_ref[0, 0, j], v=vp_ref[0, pl.ds(row0, tk), :], visible=None, u=u_past,
                    carry_bias=carry_bias)

    first = [dict(kt=ktd_ref[0, 0, sd], v=vd_ref[0, sd * td:(sd + 1) * td, :],
                  visible=cols_i + sd * td < rows_i, u=u_diag) for sd in reversed(range(nd))]
    no_past_bias = jnp.where(n_past > 0, 0.0, SB_NO_PAST_CARRY) if causal_past else None
    first.append(past_block(jnp.maximum(n_past - 1, 0), no_past_bias))
    state = sweep(first, state)

    def min_carry(st):
        m = st[0]
        for hh in range(1, heads):
            m = jnp.minimum(m, st[2 * hh])
        return jnp.min(m)

    def cond(loop):
        i, smallest, _ = loop
        return jnp.logical_and(i < n_past, smallest < SB_ZERO_WEIGHT_LOG)

    def body(loop):
        i, _, st = loop
        st = sweep([past_block(n_past - 1 - i)], st)
        return i + 1, min_carry(st), tuple(st)

    _, _, st = lax.while_loop(cond, body, (jnp.int32(1), min_carry(state), tuple(state)))
    o_ref[0] = jnp.concatenate([st[2 * hh + 1] for hh in range(heads)], axis=1).astype(BF16)


def _sb_attention(q, kt_diag, v_diag, kt_past, v_past, *, tq, td, tk, causal_past):
    b, l, w = q.shape
    n_g, hw = kt_diag.shape[1], kt_diag.shape[3]
    n_past_blocks = kt_past.shape[2]
    assert tq % td == 0 and (not causal_past or tq % tk == 0)
    p = v_past.shape[1]
    dh = w // SB_HEADS
    kern = functools.partial(_sb_kernel, tq=tq, td=td, nd=tq // td, tk=tk, heads=hw // dh, dh=dh,
                             causal_past=causal_past, n_past_static=n_past_blocks)
    return pl.pallas_call(
        kern,
        grid=(b, n_g, l // tq),
        in_specs=[
            pl.BlockSpec((1, tq, hw), lambda bi, hp, qi: (bi, qi, hp)),
            pl.BlockSpec((1, 1, tq // td, hw, td), lambda bi, hp, qi: (bi, hp, qi, 0, 0)),
            pl.BlockSpec((1, tq, hw), lambda bi, hp, qi: (bi, qi, hp)),
            pl.BlockSpec((1, 1, n_past_blocks, hw, tk), lambda bi, hp, qi: (bi, hp, 0, 0, 0),
                         pipeline_mode=pl.Buffered(1)),
            pl.BlockSpec((1, p, hw), lambda bi, hp, qi: (bi, 0, hp), pipeline_mode=pl.Buffered(1)),
        ],
        out_specs=pl.BlockSpec((1, tq, hw), lambda bi, hp, qi: (bi, qi, hp)),
        out_shape=jax.ShapeDtypeStruct((b, l, w), BF16),
        compiler_params=_params("parallel", "parallel", "arbitrary"),
        name="stick_breaking_attention",
    )(q, kt_diag, v_diag, kt_past, v_past)


def _cache_layout_kernel(k_ref, v_ref, kt_ref, vb_ref):
    _, _, n_blk, hw, tk = kt_ref.shape
    n_h, dh = k_ref.shape[3], k_ref.shape[4]
    k_heads = pltpu.einshape("mhd->hmd", k_ref[0, 0])
    v_heads = pltpu.einshape("mhd->hmd", v_ref[0, 0])
    for h in range(n_h):
        kt = k_heads[h].T.astype(BF16)
        g, off = divmod(h * dh, hw)
        for s in range(n_blk):
            kt_ref[0, g, s, off:off + dh, :] = kt[:, s * tk:(s + 1) * tk]
        vb_ref[0, :, h * dh:(h + 1) * dh] = v_heads[h].astype(BF16)


def _cache_layout(cache_k, cache_v, *, layer, tk):
    _, b, p, n_h, dh = cache_k.shape
    w = n_h * dh
    n_g = w // SB_GROUP_LANES
    rows = min(p, 4 * tk)
    src = pl.BlockSpec((1, 1, rows, n_h, dh), lambda bi, ji: (layer, bi, ji, 0, 0))
    return pl.pallas_call(
        _cache_layout_kernel,
        grid=(b, p // rows),
        in_specs=[src, src],
        out_specs=(pl.BlockSpec((1, n_g, rows // tk, SB_GROUP_LANES, tk), lambda bi, ji: (bi, 0, ji, 0, 0)),
                   pl.BlockSpec((1, rows, w), lambda bi, ji: (bi, ji, 0))),
        out_shape=(jax.ShapeDtypeStruct((b, n_g, p // tk, SB_GROUP_LANES, tk), BF16),
                   jax.ShapeDtypeStruct((b, p, w), BF16)),
        compiler_params=_params("parallel", "parallel"),
        name="cache_layout",
    )(cache_k, cache_v)


def _gdn_kernel(u_ref, z_ref, ba_ref, bat_ref, hist_ref, s0_ref, cw_ref,
                alog_l_ref, dt_l_ref, alog_s_ref, dt_s_ref, hg_ref,
                o_ref, s_ref, cnew_ref, ext_ref, *, rows, chunk, width, n_taps):
    step = pl.program_id(1)
    n_steps = pl.num_programs(1)
    dk = width // GDN_HEADS
    n_sub = rows // chunk
    pad = 8
    n_hist = n_taps - 1

    @pl.when(step == 0)
    def _():
        ext_ref[0:pad, :] = jnp.zeros((pad, ext_ref.shape[1]), F32)
        ext_ref[pad - n_hist:pad, :] = hist_ref[0]
        s_ref[...] = s0_ref[...]

    ext_ref[pad:pad + rows, :] = u_ref[0]
    y = ext_ref[pad:pad + rows, :] * cw_ref[n_hist:n_taps, :]
    for i in reversed(range(n_hist)):
        y = y + ext_ref[pad - n_hist + i:pad - n_hist + i + rows, :] * cw_ref[i:i + 1, :]
    new_tail = ext_ref[rows:rows + pad, :]
    ext_ref[0:pad, :] = new_tail

    @pl.when(step == n_steps - 1)
    def _():
        cnew_ref[0] = new_tail[pad - n_hist:pad, :]

    qkv = y * jax.nn.sigmoid(y)

    heads = range(GDN_HEADS)
    qs, ks, vs = [], [], []
    for h in heads:
        q = qkv[:, h * dk:(h + 1) * dk]
        k = qkv[:, width + h * dk:width + (h + 1) * dk]
        qs.append(q * lax.rsqrt(jnp.sum(q * q, axis=-1, keepdims=True) + NORM_EPS) * float(dk ** -0.5))
        ks.append(k * lax.rsqrt(jnp.sum(k * k, axis=-1, keepdims=True) + NORM_EPS))
        vs.append(qkv[:, 2 * width + h * dk:2 * width + (h + 1) * dk])

    r_i = lax.broadcasted_iota(jnp.int32, (chunk, chunk), 0)
    c_i = lax.broadcasted_iota(jnp.int32, (chunk, chunk), 1)
    incl = r_i >= c_i
    strict = r_i > c_i
    eye = (r_i == c_i).astype(F32)
    rr = lax.broadcasted_iota(jnp.int32, (rows, rows), 0)
    cc = lax.broadcasted_iota(jnp.int32, (rows, rows), 1)
    same_chunk = (rr // chunk) == (cc // chunk)
    lower_incl = (same_chunk & (rr >= cc)).astype(BF16)
    upper_incl = (same_chunk & (rr <= cc)).astype(BF16)

    def softplus(t):
        return jnp.maximum(t, 0.0) + jnp.log(1.0 + jnp.exp(-jnp.abs(t)))

    ba = ba_ref[0]
    g_cols = -jnp.exp(alog_l_ref[...]) * softplus(ba + dt_l_ref[...])
    gcum_cols = _dot_exactlhs_f32(lower_incl, g_cols)
    beta_cols = jax.nn.sigmoid(ba)
    bat = bat_ref[0]
    g_rows = -jnp.exp(alog_s_ref[...]) * softplus(bat + dt_s_ref[...])
    gcum_rows = _dot_f32_exactrhs(g_rows, upper_incl)

    pairs = [(ci, h) for ci in range(n_sub) for h in heads]

    pre = {}
    for ci, h in pairs:
        rs = slice(ci * chunk, (ci + 1) * chunk)
        gc_col = gcum_cols[rs, GDN_HEADS + h:GDN_HEADS + h + 1]
        gc_row = gcum_rows[GDN_HEADS + h:GDN_HEADS + h + 1, rs]
        gc_last = gc_row[:, chunk - 1:chunk]
        beta = beta_cols[rs, h:h + 1]
        gamma = jnp.where(incl, jnp.exp(gc_col - gc_row), 0.0)
        decay_in = jnp.exp(gc_col)
        q, k, v = qs[h][rs], ks[h][rs], vs[h][rs]
        kb = k * beta
        kq = _dot_nt(jnp.concatenate([kb, q], axis=0).astype(BF16), k.astype(BF16))
        pre[ci, h] = dict(
            n=jnp.where(strict, -(kq[:chunk] * gamma), 0.0),
            qk=(kq[chunk:] * gamma).astype(BF16),
            rhs=jnp.concatenate([v * beta, kb * decay_in], axis=1).astype(BF16),
            q_dec=(q * decay_in).astype(BF16),
            k_end=(k * jnp.exp(gc_last - gc_col)).astype(BF16),
            chunk_decay=jnp.exp(gc_last))

    n_rounds = max(1, (chunk - 1).bit_length())
    m_pow = {p: pre[p]["n"] for p in pairs}
    t_inv = {p: eye + pre[p]["n"] for p in pairs}
    for i in range(n_rounds):
        last = i == n_rounds - 1
        for p in pairs:
            m_bf = m_pow[p].astype(BF16)
            if i == 0:
                m_pow[p] = _dot(m_bf, m_bf)
            elif last:
                t_inv[p] = t_inv[p] + _dot(t_inv[p].astype(BF16), m_bf)
            else:
                both = _dot(jnp.concatenate([m_bf, t_inv[p].astype(BF16)], axis=0), m_bf)
                m_pow[p] = both[:chunk]
                t_inv[p] = t_inv[p] + both[chunk:]

    sol = {p: _dot(t_inv[p].astype(BF16), pre[p]["rhs"]) for p in pairs}

    state = [s_ref[0, h] for h in heads]
    o_rows = [[None] * n_sub for _ in heads]
    for ci in range(n_sub):
        ws = [_dot(jnp.concatenate([sol[ci, h][:, dk:].astype(BF16), pre[ci, h]["q_dec"]], axis=0),
                   state[h].astype(BF16)) for h in heads]
        v_new = [(sol[ci, h][:, :dk] - ws[h][:chunk]).astype(BF16) for h in heads]
        for h in heads:
            o_rows[h][ci] = ws[h][chunk:] + _dot(pre[ci, h]["qk"], v_new[h])
        state = [state[h] * pre[ci, h]["chunk_decay"] + _dot_tn(pre[ci, h]["k_end"], v_new[h])
                 for h in heads]
    for h in heads:
        s_ref[0, h] = state[h]

    z = z_ref[0]
    outs = []
    for h in heads:
        o_h = o_rows[h][0] if n_sub == 1 else jnp.concatenate(o_rows[h], axis=0)
        z_h = z[:, h * dk:(h + 1) * dk]
        outs.append(_rms(o_h, hg_ref[...]) * (z_h * jax.nn.sigmoid(z_h)))
    o_ref[0] = jnp.concatenate(outs, axis=1).astype(BF16)


def _gdn(u, z, ba, bat, hist, s0, conv_w, alog_l, dt_l, alog_s, dt_s, head_gain, *, chunk):
    b, l, conv_dim = u.shape
    width = conv_dim // 3
    dk = width // GDN_HEADS
    n_taps = conv_w.shape[0]
    rows = min(GDN_ROWS, l)
    kern = functools.partial(_gdn_kernel, rows=rows, chunk=chunk, width=width, n_taps=n_taps)

    def row_block(wd):
        return pl.BlockSpec((1, rows, wd), lambda bi, si: (bi, si, 0))

    state_spec = pl.BlockSpec((1, GDN_HEADS, dk, dk), lambda bi, si: (bi, 0, 0, 0))
    hist_spec = pl.BlockSpec((1, n_taps - 1, conv_dim), lambda bi, si: (bi, 0, 0))
    return pl.pallas_call(
        kern,
        grid=(b, l // rows),
        in_specs=[row_block(conv_dim), row_block(width), row_block(LANES),
                  pl.BlockSpec((1, 8, rows), lambda bi, si: (bi, 0, si)),
                  hist_spec, state_spec, _const_spec(conv_w.shape),
                  _const_spec((1, LANES)), _const_spec((1, LANES)),
                  _const_spec((8, 1)), _const_spec((8, 1)), _const_spec((1, dk))],
        out_specs=(row_block(width), state_spec, hist_spec),
        out_shape=(jax.ShapeDtypeStruct((b, l, width), BF16),
                   jax.ShapeDtypeStruct(s0.shape, F32),
                   jax.ShapeDtypeStruct(hist.shape, F32)),
        scratch_shapes=[pltpu.VMEM((rows + 8, conv_dim), F32)],
        compiler_params=_params("parallel", "arbitrary"),
        name="gated_delta_rule",
    )(u, z, ba, bat, hist, s0, conv_w, alog_l, dt_l, alog_s, dt_s, head_gain)


def _merge_kernel(x_ref, osb_ref, ogdn_ref, gin_ref, gout_ref, wgate_ref, wsb_ref, wgdn_ref, wout_ref, o_ref):
    x = x_ref[...]
    d = x.shape[1]
    h = _rms(x, gin_ref[...]).astype(BF16)
    gate_sb = jax.nn.sigmoid(_dot(h, wgate_ref[:, :d]))
    merged = gate_sb * _dot(osb_ref[...], wsb_ref[...])
    gate_gdn = jax.nn.sigmoid(_dot(h, wgate_ref[:, d:]))
    merged = merged + gate_gdn * _dot(ogdn_ref[...], wgdn_ref[...])
    m = _dot(merged.astype(BF16), wout_ref[...])
    o_ref[...] = x + _rms(m, gout_ref[...])


def _merge(x2, osb2, ogdn2, g_in, g_out, wgate, wsb, wgdn, wout):
    n, d = x2.shape
    tm = min(ROW_TILE, n)

    def rows(wd):
        return pl.BlockSpec((tm, wd), lambda i: (i, 0))

    return pl.pallas_call(
        _merge_kernel,
        grid=(n // tm,),
        in_specs=[rows(d), rows(osb2.shape[1]), rows(ogdn2.shape[1]),
                  _const_spec((1, d)), _const_spec((1, d)),
                  _const_spec(wgate.shape), _const_spec(wsb.shape), _const_spec(wgdn.shape),
                  _const_spec(wout.shape)],
        out_specs=rows(d),
        out_shape=jax.ShapeDtypeStruct((n, d), F32),
        compiler_params=_params("parallel"),
        name="branch_merge",
    )(x2, osb2, ogdn2, g_in, g_out, wgate, wsb, wgdn, wout)


def _layer_weights(l, norm_gains, w_in_bf, conv_w, gdn_a_log, gdn_dt_bias, gdn_norm_gain,
                   w_branch_sb, w_branch_gdn, w_out):
    d = w_in_bf.shape[1]
    sb_w = w_branch_sb.shape[1]
    gdn_w = w_branch_gdn.shape[1]
    conv_dim = conv_w.shape[2]
    n_a = 3 * sb_w + conv_dim + gdn_w
    wi = w_in_bf[l]
    wba = jnp.zeros((d, LANES), BF16).at[:, :2 * GDN_HEADS].set(wi[:, n_a:n_a + 2 * GDN_HEADS])

    def lane_vec(p):
        return jnp.zeros((1, LANES), F32).at[0, GDN_HEADS:2 * GDN_HEADS].set(p)

    def sublane_vec(p):
        return jnp.zeros((8, 1), F32).at[GDN_HEADS:2 * GDN_HEADS, 0].set(p)

    return dict(
        gains=[norm_gains[l, i][None, :] for i in range(6)],
        wa=wi[:, :n_a],
        wba=wba,
        wgate=wi[:, n_a + 2 * GDN_HEADS:],
        wsb=w_branch_sb[l].astype(BF16), wgdn=w_branch_gdn[l].astype(BF16), wout=w_out[l].astype(BF16),
        conv_w=conv_w[l],
        alog_l=lane_vec(gdn_a_log[l]), dt_l=lane_vec(gdn_dt_bias[l]),
        alog_s=sublane_vec(gdn_a_log[l]), dt_s=sublane_vec(gdn_dt_bias[l]),
        head_gain=gdn_norm_gain[l][None, :],
        dims=(sb_w, conv_dim, gdn_w),
    )


def _run_group(x, weights, ffn_w, past_k, past_v, conv_hist, s0, chunk):
    b, l, d = x.shape
    n = b * l
    s_list, c_list = [], []
    kv_stacks = ()
    for li, w in enumerate(weights):
        sb_w, conv_dim, gdn_w = w["dims"]
        dh = sb_w // SB_HEADS
        tq = min(SB_QUERY_TILE, l)
        td = min(SB_KEY_TILE, l)
        g = w["gains"]
        x2 = _ffn(x.reshape(n, d), g[0], g[1], ffn_w[0], ffn_w[1], li)
        q, k_stack, v_stack, kt, vb, u, z, ba, bat = _inproj(
            x2.reshape(b, l, d), g[2], w["wa"], w["wba"], kv_stacks,
            layer=li, depth=len(weights), sb_w=sb_w, conv_dim=conv_dim, gdn_w=gdn_w, sb_tile=td)
        kv_stacks = (k_stack, v_stack)
        if past_k is None:
            o_sb = _sb_attention(q, kt, vb, kt, vb, tq=tq, td=td, tk=td, causal_past=True)
        else:
            p = past_k.shape[2]
            tk = min(SB_KEY_TILE, p)
            ktp, vp = _cache_layout(past_k, past_v, layer=li, tk=tk)
            o_sb = _sb_attention(q, kt, vb, ktp, vp, tq=tq, td=td, tk=tk, causal_past=False)
        o_gdn, s_fin, conv_new = _gdn(u, z, ba, bat, conv_hist[li], s0[li], w["conv_w"],
                                      w["alog_l"], w["dt_l"], w["alog_s"], w["dt_s"], w["head_gain"],
                                      chunk=chunk)
        x2 = _merge(x2, o_sb.reshape(n, sb_w), o_gdn.reshape(n, gdn_w), g[2], g[3],
                    w["wgate"], w["wsb"], w["wgdn"], w["wout"])
        x2 = _ffn(x2, g[4], g[5], ffn_w[2], ffn_w[3], li)
        x = x2.reshape(b, l, d)
        s_list.append(s_fin)
        c_list.append(conv_new)
    k_all, v_all = (t.reshape(len(weights), b, l, SB_HEADS, dh) for t in kv_stacks)
    return x, k_all, v_all, jnp.stack(s_list), jnp.stack(c_list)


def kernel(x_prompt, x_sample, cache_sb_k, cache_sb_v, state_gdn, state_conv, norm_gains,
           w_ffn1_up, w_ffn1_down, w_in, conv_w, gdn_a_log, gdn_dt_bias, gdn_norm_gain,
           w_branch_sb, w_branch_gdn, w_out, w_ffn2_up, w_ffn2_down):
    depth = w_in.shape[0]
    ffn_w = tuple(_to_bf16(w) for w in (w_ffn1_up, w_ffn1_down, w_ffn2_up, w_ffn2_down))
    w_in_bf = _to_bf16(w_in)
    weights = [_layer_weights(l, norm_gains, w_in_bf, conv_w, gdn_a_log, gdn_dt_bias, gdn_norm_gain,
                              w_branch_sb, w_branch_gdn, w_out) for l in range(depth)]
    bp = x_prompt.shape[0]
    chunk_prompt = 64
    zero_conv = jnp.zeros((depth, bp) + state_conv.shape[2:], state_conv.dtype)
    zero_state = jnp.zeros((depth, bp) + state_gdn.shape[2:], state_gdn.dtype)
    y_p, pk, pv, ps, pc = _run_group(x_prompt, weights, ffn_w, None, None, zero_conv, zero_state, chunk_prompt)
    y_s, sk, sv, ss, sc = _run_group(x_sample, weights, ffn_w, cache_sb_k, cache_sb_v, state_conv, state_gdn,
                                     x_sample.shape[1])
    return (y_p, y_s, pk, pv, ps, pc, sk, sv, ss, sc)
```

```python
import functools

import jax
import jax.numpy as jnp
from jax import lax
from jax.experimental import pallas as pl
from jax.experimental.pallas import tpu as pltpu

F32 = jnp.float32
BF16 = jnp.bfloat16

NORM_EPS = 1e-6
LOG2E = 1.4426950408889634
SB_HEADS = 8
GDN_HEADS = 4
LANES = 128
VMEM_LIMIT_BYTES = 56 * 1024 * 1024

ROW_TILE = 512
CAST_ROWS = 256
FF_CHUNK = 256
GDN_ROWS = 512
SB_KEY_TILE = 256
SB_QUERY_TILE = 256
SB_GROUP_LANES = 512
SB_ZERO_WEIGHT_LOG = 110.0
SB_MASKED_SCORE = -1e30
SB_NO_PAST_CARRY = 1e30


def _params(*sem):
    return pltpu.CompilerParams(dimension_semantics=sem, vmem_limit_bytes=VMEM_LIMIT_BYTES)


def _const_spec(shape):
    zeros = (0,) * len(shape)
    return pl.BlockSpec(shape, lambda *_: zeros)


def _rms(x, gain):
    ms = jnp.mean(x * x, axis=-1, keepdims=True)
    return x * lax.rsqrt(ms + NORM_EPS) * gain


def _dot(a, b):
    return jnp.dot(a, b, preferred_element_type=F32)


def _dot_nt(a, b):
    return lax.dot_general(a, b, (((1,), (1,)), ((), ())), preferred_element_type=F32)


def _dot_tn(a, b):
    return lax.dot_general(a, b, (((0,), (0,)), ((), ())), preferred_element_type=F32)


def _split3(x):
    hi = x.astype(BF16)
    r = x - hi.astype(F32)
    mid = r.astype(BF16)
    lo = (r - mid.astype(F32)).astype(BF16)
    return hi, mid, lo


def _dot_f32_exactrhs(a, b_bf16):
    hi, mid, lo = _split3(a)
    return _dot(hi, b_bf16) + _dot(mid, b_bf16) + _dot(lo, b_bf16)


def _dot_exactlhs_f32(a_bf16, b):
    hi, mid, lo = _split3(b)
    return _dot(a_bf16, hi) + _dot(a_bf16, mid) + _dot(a_bf16, lo)


def _ffn_kernel(x_ref, gin_ref, gout_ref, wup_ref, wd_ref, o_ref, act_ref):
    x = x_ref[...]
    h = _rms(x, gin_ref[...]).astype(BF16)
    d_ff = wd_ref.shape[1]
    for c in range(d_ff // FF_CHUNK):
        sl = slice(c * FF_CHUNK, (c + 1) * FF_CHUNK)
        g = _dot(h, wup_ref[0, :, sl])
        u = _dot(h, wup_ref[0, :, d_ff + c * FF_CHUNK:d_ff + (c + 1) * FF_CHUNK])
        act_ref[:, sl] = (g * jax.nn.sigmoid(g) * u).astype(BF16)
    y = _dot(act_ref[...], wd_ref[0])
    o_ref[...] = x + 0.5 * _rms(y, gout_ref[...])


def _ffn(x2, g_in, g_out, w_up, w_down, layer):
    n, d = x2.shape
    d_ff = w_down.shape[1]
    tm = min(ROW_TILE, n)
    row = pl.BlockSpec((tm, d), lambda i: (i, 0))
    return pl.pallas_call(
        _ffn_kernel,
        grid=(n // tm,),
        in_specs=[row, _const_spec((1, d)), _const_spec((1, d)),
                  pl.BlockSpec((1, d, 2 * d_ff), lambda i: (layer, 0, 0)),
                  pl.BlockSpec((1, d_ff, d), lambda i: (layer, 0, 0))],
        out_specs=row,
        out_shape=jax.ShapeDtypeStruct((n, d), F32),
        scratch_shapes=[pltpu.VMEM((tm, d_ff), BF16)],
        compiler_params=_params("parallel"),
        name="ffn_half_step",
    )(x2, g_in, g_out, w_up, w_down)


def _cast_kernel(w_ref, o_ref):
    o_ref[...] = w_ref[...].astype(o_ref.dtype)


def _to_bf16(w):
    depth, rows, cols = w.shape
    tr = min(rows, CAST_ROWS)
    assert rows % tr == 0
    spec = pl.BlockSpec((1, tr, cols), lambda li, ri: (li, ri, 0))
    return pl.pallas_call(
        _cast_kernel,
        grid=(depth, rows // tr),
        in_specs=[spec],
        out_specs=spec,
        out_shape=jax.ShapeDtypeStruct(w.shape, BF16),
        compiler_params=_params("parallel", "parallel"),
        name="weights_to_bf16",
    )(w)


def _inproj_kernel(x_ref, g_ref, wa_ref, wba_ref, *refs, sb_w, conv_dim, gdn_w, sb_tile, q_scale):
    q_ref, k_ref, v_ref, kt_ref, vb_ref, u_ref, z_ref, ba_ref, bat_ref = refs[-9:]
    h = _rms(x_ref[0], g_ref[...]).astype(BF16)
    tm = h.shape[0]

    def proj(lo, width):
        return _dot(h, wa_ref[:, lo:lo + width])

    q_ref[0] = (proj(0, sb_w) * q_scale).astype(BF16)
    k = proj(sb_w, sb_w)
    k_ref[0, 0] = k
    for later in range(1, k_ref.shape[0]):
        k_ref[later, 0] = jnp.zeros_like(k)
        v_ref[later, 0] = jnp.zeros_like(k)
    kt = k.T.astype(BF16)
    for hp in range(sb_w // SB_GROUP_LANES):
        for s in range(tm // sb_tile):
            kt_ref[0, hp, s] = kt[hp * SB_GROUP_LANES:(hp + 1) * SB_GROUP_LANES,
                                  s * sb_tile:(s + 1) * sb_tile]
    v = proj(2 * sb_w, sb_w)
    v_ref[0, 0] = v
    vb_ref[0] = v.astype(BF16)
    u_ref[0] = proj(3 * sb_w, conv_dim)
    z_ref[0] = proj(3 * sb_w + conv_dim, gdn_w)
    ba = _dot(h, wba_ref[...])
    ba_ref[0] = ba
    bat_ref[0] = ba.T[:8]


def _inproj(x, gain, wa, wba, kv_stacks, *, layer, depth, sb_w, conv_dim, gdn_w, sb_tile):
    b, l, d = x.shape
    tm = min(ROW_TILE, l)
    n_hp = sb_w // SB_GROUP_LANES
    kern = functools.partial(_inproj_kernel, sb_w=sb_w, conv_dim=conv_dim, gdn_w=gdn_w,
                             sb_tile=sb_tile,
                             q_scale=float((sb_w // SB_HEADS) ** -0.5))

    def rows(width):
        return pl.BlockSpec((1, tm, width), lambda bi, ti: (bi, ti, 0))

    out_shape = (
        jax.ShapeDtypeStruct((b, l, sb_w), BF16),
        jax.ShapeDtypeStruct((depth, b, l, sb_w), F32),
        jax.ShapeDtypeStruct((depth, b, l, sb_w), F32),
        jax.ShapeDtypeStruct((b, n_hp, l // sb_tile, SB_GROUP_LANES, sb_tile), BF16),
        jax.ShapeDtypeStruct((b, l, sb_w), BF16),
        jax.ShapeDtypeStruct((b, l, conv_dim), F32),
        jax.ShapeDtypeStruct((b, l, gdn_w), F32),
        jax.ShapeDtypeStruct((b, l, LANES), F32),
        jax.ShapeDtypeStruct((b, 8, l), F32),
    )
    if kv_stacks:
        layer_rows = pl.BlockSpec((1, 1, tm, sb_w), lambda bi, ti: (layer, bi, ti, 0))
    else:
        assert layer == 0
        layer_rows = pl.BlockSpec((depth, 1, tm, sb_w), lambda bi, ti: (0, bi, ti, 0))
    out_specs = (
        rows(sb_w), layer_rows, layer_rows,
        pl.BlockSpec((1, n_hp, tm // sb_tile, SB_GROUP_LANES, sb_tile), lambda bi, ti: (bi, 0, ti, 0, 0)),
        rows(sb_w), rows(conv_dim), rows(gdn_w), rows(LANES),
        pl.BlockSpec((1, 8, tm), lambda bi, ti: (bi, 0, ti)),
    )
    return pl.pallas_call(
        kern,
        grid=(b, l // tm),
        in_specs=[rows(d), _const_spec((1, d)), _const_spec(wa.shape), _const_spec(wba.shape)]
        + [pl.BlockSpec(memory_space=pl.ANY)] * len(kv_stacks),
        out_specs=out_specs,
        out_shape=out_shape,
        input_output_aliases={4 + i: 1 + i for i in range(len(kv_stacks))},
        compiler_params=_params("parallel", "parallel"),
        name="in_projection",
    )(x, gain, wa, wba, *kv_stacks)


def _sb_kernel(q_ref, ktd_ref, vd_ref, ktp_ref, vp_ref, o_ref, carry_ref, *, tq, td, nd, tk, heads, dh,
               causal_past, n_past_static):
    qi = pl.program_id(2)

    def later_key_matrix(n):
        r = lax.broadcasted_iota(jnp.int32, (n, n), 0)
        c = lax.broadcasted_iota(jnp.int32, (n, n), 1)
        return (r > c).astype(BF16)

    def sweep(blocks, st):
        pairs = [(blk, hh) for blk in blocks for hh in range(heads)]
        scores = []
        for blk, hh in pairs:
            lanes = slice(hh * dh, (hh + 1) * dh)
            z = _dot(q_all[:, lanes], blk["kt"][lanes, :])
            if blk["visible"] is not None:
                z = jnp.where(blk["visible"], z, SB_MASKED_SCORE)
            nlk = jnp.maximum(z, 0.0) + jnp.log(1.0 + jnp.exp2(jnp.abs(z) * (-LOG2E)))
            scores.append((z - nlk, nlk.astype(BF16), jnp.sum(nlk, axis=1, keepdims=True)))
        csums = [_dot(nlk_b, blk["u"]) for (blk, _), (_, nlk_b, _) in zip(pairs, scores)]
        st = list(st)
        for (blk, hh), (log_beta, _, row_sum), csum in zip(pairs, scores, csums):
            lanes = slice(hh * dh, (hh + 1) * dh)
            carry, acc = st[2 * hh], st[2 * hh + 1]
            if carry is not None and blk.get("carry_bias") is not None:
                carry = carry + blk["carry_bias"]
            log_a = log_beta - csum if carry is None else log_beta - csum - carry
            pv = _dot(jnp.exp(log_a).astype(BF16), blk["v"][:, lanes])
            st[2 * hh + 1] = pv if acc is None else acc + pv
            st[2 * hh] = row_sum if carry is None else carry + row_sum
        return st

    q_all = q_ref[0]
    state = [None] * (2 * heads)

    u_diag = later_key_matrix(td)
    u_past = u_diag if tk == td else later_key_matrix(tk)
    rows_i = lax.broadcasted_iota(jnp.int32, (tq, td), 0)
    cols_i = lax.broadcasted_iota(jnp.int32, (tq, td), 1)
    n_past = qi * (tq // tk) if causal_past else n_past_static

    def past_block(j, carry_bias=None):
        row0 = pl.multiple_of(j * tk, tk)
        return dict(kt=ktp_ref[0, 0, j], v=vp_ref[0, pl.ds(row0, tk), :], visible=None, u=u_past,
                    carry_bias=carry_bias)

    first = [dict(kt=ktd_ref[0, 0, sd], v=vd_ref[0, sd * td:(sd + 1) * td, :],
                  visible=cols_i + sd * td < rows_i, u=u_diag) for sd in reversed(range(nd))]
    no_past_bias = jnp.where(n_past > 0, 0.0, SB_NO_PAST_CARRY) if causal_past else None
    first.append(past_block(jnp.maximum(n_past - 1, 0), no_past_bias))
    state = sweep(first, state)

    def min_carry(st):
        m = st[0]
        for hh in range(1, heads):
            m = jnp.minimum(m, st[2 * hh])
        return jnp.min(m)

    def cond(loop):
        i, smallest, _ = loop
        return jnp.logical_and(i < n_past, smallest < SB_ZERO_WEIGHT_LOG)

    def body(loop):
        i, _, st = loop
        st = sweep([past_block(n_past - 1 - i)], st)
        return i + 1, min_carry(st), tuple(st)

    _, smallest, st = lax.while_loop(cond, body, (jnp.int32(1), min_carry(state), tuple(state)))
    o_ref[0] = jnp.concatenate([st[2 * hh + 1] for hh in range(heads)], axis=1).astype(BF16)
    carry_ref[...] = jnp.full(carry_ref.shape, smallest, F32)


def _sb_attention(q, kt_diag, v_diag, kt_past, v_past, *, tq, td, tk, causal_past):
    b, l, w = q.shape
    n_g, hw = kt_diag.shape[1], kt_diag.shape[3]
    n_past_blocks = kt_past.shape[2]
    assert tq % td == 0 and (not causal_past or tq % tk == 0)
    p = v_past.shape[1]
    dh = w // SB_HEADS
    kern = functools.partial(_sb_kernel, tq=tq, td=td, nd=tq // td, tk=tk, heads=hw // dh, dh=dh,
                             causal_past=causal_past, n_past_static=n_past_blocks)
    return pl.pallas_call(
        kern,
        grid=(b, n_g, l // tq),
        in_specs=[
            pl.BlockSpec((1, tq, hw), lambda bi, hp, qi: (bi, qi, hp)),
            pl.BlockSpec((1, 1, tq // td, hw, td), lambda bi, hp, qi: (bi, hp, qi, 0, 0)),
            pl.BlockSpec((1, tq, hw), lambda bi, hp, qi: (bi, qi, hp)),
            pl.BlockSpec((1, 1, n_past_blocks, hw, tk), lambda bi, hp, qi: (bi, hp, 0, 0, 0),
                         pipeline_mode=pl.Buffered(1)),
            pl.BlockSpec((1, p, hw), lambda bi, hp, qi: (bi, 0, hp), pipeline_mode=pl.Buffered(1)),
        ],
        out_specs=(pl.BlockSpec((1, tq, hw), lambda bi, hp, qi: (bi, qi, hp)),
                   pl.BlockSpec((1, 1, 1, 8, LANES), lambda bi, hp, qi: (bi, hp, qi, 0, 0))),
        out_shape=(jax.ShapeDtypeStruct((b, l, w), BF16),
                   jax.ShapeDtypeStruct((b, n_g, l // tq, 8, LANES), F32)),
        compiler_params=_params("parallel", "parallel", "arbitrary"),
        name="stick_breaking_attention",
    )(q, kt_diag, v_diag, kt_past, v_past)


def _gdn_kernel(u_ref, z_ref, ba_ref, bat_ref, hist_ref, s0_ref, cw_ref,
                alog_l_ref, dt_l_ref, alog_s_ref, dt_s_ref, hg_ref,
                o_ref, s_ref, cnew_ref, ext_ref, *, rows, chunk, width, n_taps):
    step = pl.program_id(1)
    n_steps = pl.num_programs(1)
    dk = width // GDN_HEADS
    n_sub = rows // chunk
    pad = 8
    n_hist = n_taps - 1

    @pl.when(step == 0)
    def _():
        ext_ref[0:pad, :] = jnp.zeros((pad, ext_ref.shape[1]), F32)
        ext_ref[pad - n_hist:pad, :] = hist_ref[0]
        s_ref[...] = s0_ref[...]

    ext_ref[pad:pad + rows, :] = u_ref[0]
    y = ext_ref[pad:pad + rows, :] * cw_ref[n_hist:n_taps, :]
    for i in reversed(range(n_hist)):
        y = y + ext_ref[pad - n_hist + i:pad - n_hist + i + rows, :] * cw_ref[i:i + 1, :]
    new_tail = ext_ref[rows:rows + pad, :]
    ext_ref[0:pad, :] = new_tail

    @pl.when(step == n_steps - 1)
    def _():
        cnew_ref[0] = new_tail[pad - n_hist:pad, :]

    qkv = y * jax.nn.sigmoid(y)

    heads = range(GDN_HEADS)
    qs, ks, vs = [], [], []
    for h in heads:
        q = qkv[:, h * dk:(h + 1) * dk]
        k = qkv[:, width + h * dk:width + (h + 1) * dk]
        qs.append(q * lax.rsqrt(jnp.sum(q * q, axis=-1, keepdims=True) + NORM_EPS) * float(dk ** -0.5))
        ks.append(k * lax.rsqrt(jnp.sum(k * k, axis=-1, keepdims=True) + NORM_EPS))
        vs.append(qkv[:, 2 * width + h * dk:2 * width + (h + 1) * dk])

    r_i = lax.broadcasted_iota(jnp.int32, (chunk, chunk), 0)
    c_i = lax.broadcasted_iota(jnp.int32, (chunk, chunk), 1)
    incl = r_i >= c_i
    strict = r_i > c_i
    eye = (r_i == c_i).astype(F32)
    rr = lax.broadcasted_iota(jnp.int32, (rows, rows), 0)
    cc = lax.broadcasted_iota(jnp.int32, (rows, rows), 1)
    same_chunk = (rr // chunk) == (cc // chunk)
    lower_incl = (same_chunk & (rr >= cc)).astype(BF16)
    upper_incl = (same_chunk & (rr <= cc)).astype(BF16)

    def softplus(t):
        return jnp.maximum(t, 0.0) + jnp.log(1.0 + jnp.exp(-jnp.abs(t)))

    ba = ba_ref[0]
    g_cols = -jnp.exp(alog_l_ref[...]) * softplus(ba + dt_l_ref[...])
    gcum_cols = _dot_exactlhs_f32(lower_incl, g_cols)
    beta_cols = jax.nn.sigmoid(ba)
    bat = bat_ref[0]
    g_rows = -jnp.exp(alog_s_ref[...]) * softplus(bat + dt_s_ref[...])
    gcum_rows = _dot_f32_exactrhs(g_rows, upper_incl)

    pairs = [(ci, h) for ci in range(n_sub) for h in heads]

    pre = {}
    for ci, h in pairs:
        rs = slice(ci * chunk, (ci + 1) * chunk)
        gc_col = gcum_cols[rs, GDN_HEADS + h:GDN_HEADS + h + 1]
        gc_row = gcum_rows[GDN_HEADS + h:GDN_HEADS + h + 1, rs]
        gc_last = gc_row[:, chunk - 1:chunk]
        beta = beta_cols[rs, h:h + 1]
        gamma = jnp.where(incl, jnp.exp(gc_col - gc_row), 0.0)
        decay_in = jnp.exp(gc_col)
        q, k, v = qs[h][rs], ks[h][rs], vs[h][rs]
        kb = k * beta
        kq = _dot_nt(jnp.concatenate([kb, q], axis=0).astype(BF16), k.astype(BF16))
        pre[ci, h] = dict(
            n=jnp.where(strict, -(kq[:chunk] * gamma), 0.0),
            qk=(kq[chunk:] * gamma).astype(BF16),
            rhs=jnp.concatenate([v * beta, kb * decay_in], axis=1).astype(BF16),
            q_dec=(q * decay_in).astype(BF16),
            k_end=(k * jnp.exp(gc_last - gc_col)).astype(BF16),
            chunk_decay=jnp.exp(gc_last))

    n_rounds = max(1, (chunk - 1).bit_length())
    m_pow = {p: pre[p]["n"] for p in pairs}
    t_inv = {p: eye + pre[p]["n"] for p in pairs}
    for i in range(n_rounds):
        last = i == n_rounds - 1
        for p in pairs:
            m_bf = m_pow[p].astype(BF16)
            if i == 0:
                m_pow[p] = _dot(m_bf, m_bf)
            elif last:
                t_inv[p] = t_inv[p] + _dot(t_inv[p].astype(BF16), m_bf)
            else:
                both = _dot(jnp.concatenate([m_bf, t_inv[p].astype(BF16)], axis=0), m_bf)
                m_pow[p] = both[:chunk]
                t_inv[p] = t_inv[p] + both[chunk:]

    sol = {p: _dot(t_inv[p].astype(BF16), pre[p]["rhs"]) for p in pairs}

    state = [s_ref[0, h] for h in heads]
    o_rows = [[None] * n_sub for _ in heads]
    for ci in range(n_sub):
        ws = [_dot(jnp.concatenate([sol[ci, h][:, dk:].astype(BF16), pre[ci, h]["q_dec"]], axis=0),
                   state[h].astype(BF16)) for h in heads]
        v_new = [(sol[ci, h][:, :dk] - ws[h][:chunk]).astype(BF16) for h in heads]
        for h in heads:
            o_rows[h][ci] = ws[h][chunk:] + _dot(pre[ci, h]["qk"], v_new[h])
        state = [state[h] * pre[ci, h]["chunk_decay"] + _dot_tn(pre[ci, h]["k_end"], v_new[h])
                 for h in heads]
    for h in heads:
        s_ref[0, h] = state[h]

    z = z_ref[0]
    outs = []
    for h in heads:
        o_h = o_rows[h][0] if n_sub == 1 else jnp.concatenate(o_rows[h], axis=0)
        z_h = z[:, h * dk:(h + 1) * dk]
        outs.append(_rms(o_h, hg_ref[...]) * (z_h * jax.nn.sigmoid(z_h)))
    o_ref[0] = jnp.concatenate(outs, axis=1).astype(BF16)


def _gdn(u, z, ba, bat, hist, s0, conv_w, alog_l, dt_l, alog_s, dt_s, head_gain, *, chunk):
    b, l, conv_dim = u.shape
    width = conv_dim // 3
    dk = width // GDN_HEADS
    n_taps = conv_w.shape[0]
    rows = min(GDN_ROWS, l)
    kern = functools.partial(_gdn_kernel, rows=rows, chunk=chunk, width=width, n_taps=n_taps)

    def row_block(wd):
        return pl.BlockSpec((1, rows, wd), lambda bi, si: (bi, si, 0))

    state_spec = pl.BlockSpec((1, GDN_HEADS, dk, dk), lambda bi, si: (bi, 0, 0, 0))
    hist_spec = pl.BlockSpec((1, n_taps - 1, conv_dim), lambda bi, si: (bi, 0, 0))
    return pl.pallas_call(
        kern,
        grid=(b, l // rows),
        in_specs=[row_block(conv_dim), row_block(width), row_block(LANES),
                  pl.BlockSpec((1, 8, rows), lambda bi, si: (bi, 0, si)),
                  hist_spec, state_spec, _const_spec(conv_w.shape),
                  _const_spec((1, LANES)), _const_spec((1, LANES)),
                  _const_spec((8, 1)), _const_spec((8, 1)), _const_spec((1, dk))],
        out_specs=(row_block(width), state_spec, hist_spec),
        out_shape=(jax.ShapeDtypeStruct((b, l, width), BF16),
                   jax.ShapeDtypeStruct(s0.shape, F32),
                   jax.ShapeDtypeStruct(hist.shape, F32)),
        scratch_shapes=[pltpu.VMEM((rows + 8, conv_dim), F32)],
        compiler_params=_params("parallel", "arbitrary"),
        name="gated_delta_rule",
    )(u, z, ba, bat, hist, s0, conv_w, alog_l, dt_l, alog_s, dt_s, head_gain)


def _merge_kernel(x_ref, osb_ref, ogdn_ref, gin_ref, gout_ref, wgate_ref, wsb_ref, wgdn_ref, wout_ref, o_ref):
    x = x_ref[...]
    d = x.shape[1]
    h = _rms(x, gin_ref[...]).astype(BF16)
    gate_sb = jax.nn.sigmoid(_dot(h, wgate_ref[:, :d]))
    merged = gate_sb * _dot(osb_ref[...], wsb_ref[...])
    gate_gdn = jax.nn.sigmoid(_dot(h, wgate_ref[:, d:]))
    merged = merged + gate_gdn * _dot(ogdn_ref[...], wgdn_ref[...])
    m = _dot(merged.astype(BF16), wout_ref[...])
    o_ref[...] = x + _rms(m, gout_ref[...])


def _merge(x2, osb2, ogdn2, g_in, g_out, wgate, wsb, wgdn, wout):
    n, d = x2.shape
    tm = min(ROW_TILE, n)

    def rows(wd):
        return pl.BlockSpec((tm, wd), lambda i: (i, 0))

    return pl.pallas_call(
        _merge_kernel,
        grid=(n // tm,),
        in_specs=[rows(d), rows(osb2.shape[1]), rows(ogdn2.shape[1]),
                  _const_spec((1, d)), _const_spec((1, d)),
                  _const_spec(wgate.shape), _const_spec(wsb.shape), _const_spec(wgdn.shape),
                  _const_spec(wout.shape)],
        out_specs=rows(d),
        out_shape=jax.ShapeDtypeStruct((n, d), F32),
        compiler_params=_params("parallel"),
        name="branch_merge",
    )(x2, osb2, ogdn2, g_in, g_out, wgate, wsb, wgdn, wout)


def _layer_weights(l, norm_gains, w_in_bf, conv_w, gdn_a_log, gdn_dt_bias, gdn_norm_gain,
                   w_branch_sb, w_branch_gdn, w_out):
    d = w_in_bf.shape[1]
    sb_w = w_branch_sb.shape[1]
    gdn_w = w_branch_gdn.shape[1]
    conv_dim = conv_w.shape[2]
    n_a = 3 * sb_w + conv_dim + gdn_w
    wi = w_in_bf[l]
    wba = jnp.zeros((d, LANES), BF16).at[:, :2 * GDN_HEADS].set(wi[:, n_a:n_a + 2 * GDN_HEADS])

    def lane_vec(p):
        return jnp.zeros((1, LANES), F32).at[0, GDN_HEADS:2 * GDN_HEADS].set(p)

    def sublane_vec(p):
        return jnp.zeros((8, 1), F32).at[GDN_HEADS:2 * GDN_HEADS, 0].set(p)

    return dict(
        gains=[norm_gains[l, i][None, :] for i in range(6)],
        wa=wi[:, :n_a],
        wba=wba,
        wgate=wi[:, n_a + 2 * GDN_HEADS:],
        wsb=w_branch_sb[l].astype(BF16), wgdn=w_branch_gdn[l].astype(BF16), wout=w_out[l].astype(BF16),
        conv_w=conv_w[l],
        alog_l=lane_vec(gdn_a_log[l]), dt_l=lane_vec(gdn_dt_bias[l]),
        alog_s=sublane_vec(gdn_a_log[l]), dt_s=sublane_vec(gdn_dt_bias[l]),
        head_gain=gdn_norm_gain[l][None, :],
        dims=(sb_w, conv_dim, gdn_w),
    )


def _run_group(x, weights, ffn_w, past_k, past_v, conv_hist, s0, chunk):
    b, l, d = x.shape
    n = b * l
    s_list, c_list = [], []
    kv_stacks = ()
    for li, w in enumerate(weights):
        sb_w, conv_dim, gdn_w = w["dims"]
        dh = sb_w // SB_HEADS
        tq = min(SB_QUERY_TILE, l)
        td = min(SB_KEY_TILE, l)
        g = w["gains"]
        x2 = _ffn(x.reshape(n, d), g[0], g[1], ffn_w[0], ffn_w[1], li)
        q, k_stack, v_stack, kt, vb, u, z, ba, bat = _inproj(
            x2.reshape(b, l, d), g[2], w["wa"], w["wba"], kv_stacks,
            layer=li, depth=len(weights), sb_w=sb_w, conv_dim=conv_dim, gdn_w=gdn_w, sb_tile=td)
        kv_stacks = (k_stack, v_stack)
        if past_k is None:
            o_sb, _ = _sb_attention(q, kt, vb, kt, vb, tq=tq, td=td, tk=td, causal_past=True)
        else:
            p = past_k.shape[2]
            tk = min(SB_KEY_TILE, p)

            def past_layout(pk, pv):
                keys = pk.shape[1]
                ktp = pk.astype(BF16).reshape(b, keys // tk, tk, sb_w // SB_GROUP_LANES, SB_GROUP_LANES)
                return ktp.transpose(0, 3, 1, 4, 2), pv.astype(BF16).reshape(b, keys, sb_w)

            def attend(pk, pv):
                return _sb_attention(q, kt, vb, *past_layout(pk, pv), tq=tq, td=td, tk=tk,
                                     causal_past=False)

            o_sb, carry_min = attend(past_k[li][:, p - tk:], past_v[li][:, p - tk:])
            if p > tk:
                o_sb = lax.cond(jnp.min(carry_min) >= SB_ZERO_WEIGHT_LOG, lambda: o_sb,
                                lambda: attend(past_k[li], past_v[li])[0])
        o_gdn, s_fin, conv_new = _gdn(u, z, ba, bat, conv_hist[li], s0[li], w["conv_w"],
                                      w["alog_l"], w["dt_l"], w["alog_s"], w["dt_s"], w["head_gain"],
                                      chunk=chunk)
        x2 = _merge(x2, o_sb.reshape(n, sb_w), o_gdn.reshape(n, gdn_w), g[2], g[3],
                    w["wgate"], w["wsb"], w["wgdn"], w["wout"])
        x2 = _ffn(x2, g[4], g[5], ffn_w[2], ffn_w[3], li)
        x = x2.reshape(b, l, d)
        s_list.append(s_fin)
        c_list.append(conv_new)
    k_all, v_all = (t.reshape(len(weights), b, l, SB_HEADS, dh) for t in kv_stacks)
    return x, k_all, v_all, jnp.stack(s_list), jnp.stack(c_list)


def kernel(x_prompt, x_sample, cache_sb_k, cache_sb_v, state_gdn, state_conv, norm_gains,
           w_ffn1_up, w_ffn1_down, w_in, conv_w, gdn_a_log, gdn_dt_bias, gdn_norm_gain,
           w_branch_sb, w_branch_gdn, w_out, w_ffn2_up, w_ffn2_down):
    depth = w_in.shape[0]
    ffn_w = tuple(_to_bf16(w) for w in (w_ffn1_up, w_ffn1_down, w_ffn2_up, w_ffn2_down))
    w_in_bf = _to_bf16(w_in)
    weights = [_layer_weights(l, norm_gains, w_in_bf, conv_w, gdn_a_log, gdn_dt_bias, gdn_norm_gain,
                              w_branch_sb, w_branch_gdn, w_out) for l in range(depth)]
    bp = x_prompt.shape[0]
    chunk_prompt = 64
    zero_conv = jnp.zeros((depth, bp) + state_conv.shape[2:], state_conv.dtype)
    zero_state = jnp.zeros((depth, bp) + state_gdn.shape[2:], state_gdn.dtype)
    y_p, pk, pv, ps, pc = _run_group(x_prompt, weights, ffn_w, None, None, zero_conv, zero_state, chunk_prompt)
    y_s, sk, sv, ss, sc = _run_group(x_sample, weights, ffn_w, cache_sb_k, cache_sb_v, state_conv, state_gdn,
                                     x_sample.shape[1])
    return (y_p, y_s, pk, pv, ps, pc, sk, sv, ss, sc)
```

```python
import functools

import jax
import jax.numpy as jnp
from jax import lax
from jax.experimental import pallas as pl
from jax.experimental.pallas import tpu as pltpu

F32 = jnp.float32
BF16 = jnp.bfloat16

NORM_EPS = 1e-6
LOG2E = 1.4426950408889634
SB_HEADS = 8
GDN_HEADS = 4
LANES = 128
VMEM_LIMIT_BYTES = 56 * 1024 * 1024

ROW_TILE = 512
CAST_ROWS = 256
FF_CHUNK = 256
GDN_ROWS = 512
SB_KEY_TILE = 256
SB_QUERY_TILE = 256
SB_GROUP_LANES = 512
SB_ZERO_WEIGHT_LOG = 110.0
SB_MASKED_SCORE = -1e30
SB_NO_PAST_CARRY = 1e30


def _params(*sem):
    return pltpu.CompilerParams(dimension_semantics=sem, vmem_limit_bytes=VMEM_LIMIT_BYTES)


def _const_spec(shape):
    zeros = (0,) * len(shape)
    return pl.BlockSpec(shape, lambda *_: zeros)


def _rms(x, gain):
    ms = jnp.mean(x * x, axis=-1, keepdims=True)
    return x * lax.rsqrt(ms + NORM_EPS) * gain


def _dot(a, b):
    return jnp.dot(a, b, preferred_element_type=F32)


def _dot_nt(a, b):
    return lax.dot_general(a, b, (((1,), (1,)), ((), ())), preferred_element_type=F32)


def _dot_tn(a, b):
    return lax.dot_general(a, b, (((0,), (0,)), ((), ())), preferred_element_type=F32)


def _split3(x):
    hi = x.astype(BF16)
    r = x - hi.astype(F32)
    mid = r.astype(BF16)
    lo = (r - mid.astype(F32)).astype(BF16)
    return hi, mid, lo


def _dot_f32_exactrhs(a, b_bf16):
    hi, mid, lo = _split3(a)
    return _dot(hi, b_bf16) + _dot(mid, b_bf16) + _dot(lo, b_bf16)


def _dot_exactlhs_f32(a_bf16, b):
    hi, mid, lo = _split3(b)
    return _dot(a_bf16, hi) + _dot(a_bf16, mid) + _dot(a_bf16, lo)


def _ffn_kernel(x_ref, gin_ref, gout_ref, wup_ref, wd_ref, o_ref, act_ref):
    x = x_ref[...]
    h = _rms(x, gin_ref[...]).astype(BF16)
    d_ff = wd_ref.shape[1]
    for c in range(d_ff // FF_CHUNK):
        sl = slice(c * FF_CHUNK, (c + 1) * FF_CHUNK)
        g = _dot(h, wup_ref[0, :, sl])
        u = _dot(h, wup_ref[0, :, d_ff + c * FF_CHUNK:d_ff + (c + 1) * FF_CHUNK])
        act_ref[:, sl] = (g * jax.nn.sigmoid(g) * u).astype(BF16)
    y = _dot(act_ref[...], wd_ref[0])
    o_ref[...] = x + 0.5 * _rms(y, gout_ref[...])


def _ffn(x2, g_in, g_out, w_up, w_down, layer):
    n, d = x2.shape
    d_ff = w_down.shape[1]
    tm = min(ROW_TILE, n)
    row = pl.BlockSpec((tm, d), lambda i: (i, 0))
    return pl.pallas_call(
        _ffn_kernel,
        grid=(n // tm,),
        in_specs=[row, _const_spec((1, d)), _const_spec((1, d)),
                  pl.BlockSpec((1, d, 2 * d_ff), lambda i: (layer, 0, 0)),
                  pl.BlockSpec((1, d_ff, d), lambda i: (layer, 0, 0))],
        out_specs=row,
        out_shape=jax.ShapeDtypeStruct((n, d), F32),
        scratch_shapes=[pltpu.VMEM((tm, d_ff), BF16)],
        compiler_params=_params("parallel"),
        name="ffn_half_step",
    )(x2, g_in, g_out, w_up, w_down)


def _cast_kernel(w_ref, o_ref):
    o_ref[...] = w_ref[...].astype(o_ref.dtype)


def _to_bf16(w):
    depth, rows, cols = w.shape
    tr = min(rows, CAST_ROWS)
    assert rows % tr == 0
    spec = pl.BlockSpec((1, tr, cols), lambda li, ri: (li, ri, 0))
    return pl.pallas_call(
        _cast_kernel,
        grid=(depth, rows // tr),
        in_specs=[spec],
        out_specs=spec,
        out_shape=jax.ShapeDtypeStruct(w.shape, BF16),
        compiler_params=_params("parallel", "parallel"),
        name="weights_to_bf16",
    )(w)


def _inproj_kernel(x_ref, g_ref, wa_ref, wba_ref, *refs, sb_w, conv_dim, gdn_w, sb_tile, q_scale):
    q_ref, k_ref, v_ref, kt_ref, vb_ref, u_ref, z_ref, ba_ref, bat_ref = refs[-9:]
    h = _rms(x_ref[0], g_ref[...]).astype(BF16)
    tm = h.shape[0]

    def proj(lo, width):
        return _dot(h, wa_ref[:, lo:lo + width])

    q_ref[0] = (proj(0, sb_w) * q_scale).astype(BF16)
    k = proj(sb_w, sb_w)
    k_ref[0, 0] = k
    for later in range(1, k_ref.shape[0]):
        k_ref[later, 0] = jnp.zeros_like(k)
        v_ref[later, 0] = jnp.zeros_like(k)
    kt = k.T.astype(BF16)
    for hp in range(sb_w // SB_GROUP_LANES):
        for s in range(tm // sb_tile):
            kt_ref[0, hp, s] = kt[hp * SB_GROUP_LANES:(hp + 1) * SB_GROUP_LANES,
                                  s * sb_tile:(s + 1) * sb_tile]
    v = proj(2 * sb_w, sb_w)
    v_ref[0, 0] = v
    vb_ref[0] = v.astype(BF16)
    u_ref[0] = proj(3 * sb_w, conv_dim)
    z_ref[0] = proj(3 * sb_w + conv_dim, gdn_w)
    ba = _dot(h, wba_ref[...])
    ba_ref[0] = ba
    bat_ref[0] = ba.T[:8]


def _inproj(x, gain, wa, wba, kv_stacks, *, layer, depth, sb_w, conv_dim, gdn_w, sb_tile):
    b, l, d = x.shape
    tm = min(ROW_TILE, l)
    n_hp = sb_w // SB_GROUP_LANES
    kern = functools.partial(_inproj_kernel, sb_w=sb_w, conv_dim=conv_dim, gdn_w=gdn_w,
                             sb_tile=sb_tile,
                             q_scale=float((sb_w // SB_HEADS) ** -0.5))

    def rows(width):
        return pl.BlockSpec((1, tm, width), lambda bi, ti: (bi, ti, 0))

    out_shape = (
        jax.ShapeDtypeStruct((b, l, sb_w), BF16),
        jax.ShapeDtypeStruct((depth, b, l, sb_w), F32),
        jax.ShapeDtypeStruct((depth, b, l, sb_w), F32),
        jax.ShapeDtypeStruct((b, n_hp, l // sb_tile, SB_GROUP_LANES, sb_tile), BF16),
        jax.ShapeDtypeStruct((b, l, sb_w), BF16),
        jax.ShapeDtypeStruct((b, l, conv_dim), F32),
        jax.ShapeDtypeStruct((b, l, gdn_w), F32),
        jax.ShapeDtypeStruct((b, l, LANES), F32),
        jax.ShapeDtypeStruct((b, 8, l), F32),
    )
    if kv_stacks:
        layer_rows = pl.BlockSpec((1, 1, tm, sb_w), lambda bi, ti: (layer, bi, ti, 0))
    else:
        assert layer == 0
        layer_rows = pl.BlockSpec((depth, 1, tm, sb_w), lambda bi, ti: (0, bi, ti, 0))
    out_specs = (
        rows(sb_w), layer_rows, layer_rows,
        pl.BlockSpec((1, n_hp, tm // sb_tile, SB_GROUP_LANES, sb_tile), lambda bi, ti: (bi, 0, ti, 0, 0)),
        rows(sb_w), rows(conv_dim), rows(gdn_w), rows(LANES),
        pl.BlockSpec((1, 8, tm), lambda bi, ti: (bi, 0, ti)),
    )
    return pl.pallas_call(
        kern,
        grid=(b, l // tm),
        in_specs=[rows(d), _const_spec((1, d)), _const_spec(wa.shape), _const_spec(wba.shape)]
        + [pl.BlockSpec(memory_space=pl.ANY)] * len(kv_stacks),
        out_specs=out_specs,
        out_shape=out_shape,
        input_output_aliases={4 + i: 1 + i for i in range(len(kv_stacks))},
        compiler_params=_params("parallel", "parallel"),
        name="in_projection",
    )(x, gain, wa, wba, *kv_stacks)


def _sb_kernel(q_ref, ktd_ref, vd_ref, ktp_ref, vp_ref, o_ref, carry_ref, *, tq, td, nd, tk, heads, dh,
               causal_past, n_past_static):
    qi = pl.program_id(2)

    def later_key_matrix(n):
        r = lax.broadcasted_iota(jnp.int32, (n, n), 0)
        c = lax.broadcasted_iota(jnp.int32, (n, n), 1)
        return (r > c).astype(BF16)

    def sweep(blocks, st):
        pairs = [(blk, hh) for blk in blocks for hh in range(heads)]
        scores = []
        for blk, hh in pairs:
            lanes = slice(hh * dh, (hh + 1) * dh)
            z = _dot(q_all[:, lanes], blk["kt"][lanes, :])
            if blk["visible"] is not None:
                z = jnp.where(blk["visible"], z, SB_MASKED_SCORE)
            nlk = jnp.maximum(z, 0.0) + jnp.log(1.0 + jnp.exp2(jnp.abs(z) * (-LOG2E)))
            scores.append((z - nlk, nlk.astype(BF16), jnp.sum(nlk, axis=1, keepdims=True)))
        csums = [_dot(nlk_b, blk["u"]) for (blk, _), (_, nlk_b, _) in zip(pairs, scores)]
        st = list(st)
        for (blk, hh), (log_beta, _, row_sum), csum in zip(pairs, scores, csums):
            lanes = slice(hh * dh, (hh + 1) * dh)
            carry, acc = st[2 * hh], st[2 * hh + 1]
            if carry is not None and blk.get("carry_bias") is not None:
                carry = carry + blk["carry_bias"]
            log_a = log_beta - csum if carry is None else log_beta - csum - carry
            pv = _dot(jnp.exp(log_a).astype(BF16), blk["v"][:, lanes])
            st[2 * hh + 1] = pv if acc is None else acc + pv
            st[2 * hh] = row_sum if carry is None else carry + row_sum
        return st

    q_all = q_ref[0]
    state = [None] * (2 * heads)

    u_diag = later_key_matrix(td)
    u_past = u_diag if tk == td else later_key_matrix(tk)
    rows_i = lax.broadcasted_iota(jnp.int32, (tq, td), 0)
    cols_i = lax.broadcasted_iota(jnp.int32, (tq, td), 1)
    n_past = qi * (tq // tk) if causal_past else n_past_static

    def past_block(j, carry_bias=None):
        row0 = pl.multiple_of(j * tk, tk)
        return dict(kt=ktp_ref[0, 0, j], v=vp_ref[0, pl.ds(row0, tk), :], visible=None, u=u_past,
                    carry_bias=carry_bias)

    first = [dict(kt=ktd_ref[0, 0, sd], v=vd_ref[0, sd * td:(sd + 1) * td, :],
                  visible=cols_i + sd * td < rows_i, u=u_diag) for sd in reversed(range(nd))]
    no_past_bias = jnp.where(n_past > 0, 0.0, SB_NO_PAST_CARRY) if causal_past else None
    first.append(past_block(jnp.maximum(n_past - 1, 0), no_past_bias))
    state = sweep(first, state)

    def min_carry(st):
        m = st[0]
        for hh in range(1, heads):
            m = jnp.minimum(m, st[2 * hh])
        return jnp.min(m)

    def cond(loop):
        i, smallest, _ = loop
        return jnp.logical_and(i < n_past, smallest < SB_ZERO_WEIGHT_LOG)

    def body(loop):
        i, _, st = loop
        st = sweep([past_block(n_past - 1 - i)], st)
        return i + 1, min_carry(st), tuple(st)

    _, smallest, st = lax.while_loop(cond, body, (jnp.int32(1), min_carry(state), tuple(state)))
    o_ref[0] = jnp.concatenate([st[2 * hh + 1] for hh in range(heads)], axis=1).astype(BF16)
    carry_ref[...] = jnp.full(carry_ref.shape, smallest, F32)


def _sb_attention(q, kt_diag, v_diag, kt_past, v_past, *, tq, td, tk, causal_past):
    b, l, w = q.shape
    n_g, hw = kt_diag.shape[1], kt_diag.shape[3]
    n_past_blocks = kt_past.shape[2]
    assert tq % td == 0 and (not causal_past or tq % tk == 0)
    p = v_past.shape[1]
    dh = w // SB_HEADS
    kern = functools.partial(_sb_kernel, tq=tq, td=td, nd=tq // td, tk=tk, heads=hw // dh, dh=dh,
                             causal_past=causal_past, n_past_static=n_past_blocks)
    return pl.pallas_call(
        kern,
        grid=(b, n_g, l // tq),
        in_specs=[
            pl.BlockSpec((1, tq, hw), lambda bi, hp, qi: (bi, qi, hp)),
            pl.BlockSpec((1, 1, tq // td, hw, td), lambda bi, hp, qi: (bi, hp, qi, 0, 0)),
            pl.BlockSpec((1, tq, hw), lambda bi, hp, qi: (bi, qi, hp)),
            pl.BlockSpec((1, 1, n_past_blocks, hw, tk), lambda bi, hp, qi: (bi, hp, 0, 0, 0),
                         pipeline_mode=pl.Buffered(1)),
            pl.BlockSpec((1, p, hw), lambda bi, hp, qi: (bi, 0, hp), pipeline_mode=pl.Buffered(1)),
        ],
        out_specs=(pl.BlockSpec((1, tq, hw), lambda bi, hp, qi: (bi, qi, hp)),
                   pl.BlockSpec((1, 1, 1, 8, LANES), lambda bi, hp, qi: (bi, hp, qi, 0, 0))),
        out_shape=(jax.ShapeDtypeStruct((b, l, w), BF16),
                   jax.ShapeDtypeStruct((b, n_g, l // tq, 8, LANES), F32)),
        compiler_params=_params("parallel", "parallel", "arbitrary"),
        name="stick_breaking_attention",
    )(q, kt_diag, v_diag, kt_past, v_past)


def _gdn_kernel(u_ref, z_ref, ba_ref, bat_ref, hist_ref, s0_ref, cw_ref,
                alog_l_ref, dt_l_ref, alog_s_ref, dt_s_ref, hg_ref,
                o_ref, s_ref, cnew_ref, ext_ref, *, rows, chunk, width, n_taps):
    step = pl.program_id(1)
    n_steps = pl.num_programs(1)
    dk = width // GDN_HEADS
    n_sub = rows // chunk
    pad = 8
    n_hist = n_taps - 1

    @pl.when(step == 0)
    def _():
        ext_ref[0:pad, :] = jnp.zeros((pad, ext_ref.shape[1]), F32)
        ext_ref[pad - n_hist:pad, :] = hist_ref[0]
        s_ref[...] = s0_ref[...]

    ext_ref[pad:pad + rows, :] = u_ref[0]
    y = ext_ref[pad:pad + rows, :] * cw_ref[n_hist:n_taps, :]
    for i in reversed(range(n_hist)):
        y = y + ext_ref[pad - n_hist + i:pad - n_hist + i + rows, :] * cw_ref[i:i + 1, :]
    new_tail = ext_ref[rows:rows + pad, :]
    ext_ref[0:pad, :] = new_tail

    @pl.when(step == n_steps - 1)
    def _():
        cnew_ref[0] = new_tail[pad - n_hist:pad, :]

    qkv = y * jax.nn.sigmoid(y)

    heads = range(GDN_HEADS)
    qs, ks, vs = [], [], []
    for h in heads:
        q = qkv[:, h * dk:(h + 1) * dk]
        k = qkv[:, width + h * dk:width + (h + 1) * dk]
        qs.append(q * lax.rsqrt(jnp.sum(q * q, axis=-1, keepdims=True) + NORM_EPS) * float(dk ** -0.5))
        ks.append(k * lax.rsqrt(jnp.sum(k * k, axis=-1, keepdims=True) + NORM_EPS))
        vs.append(qkv[:, 2 * width + h * dk:2 * width + (h + 1) * dk])

    r_i = lax.broadcasted_iota(jnp.int32, (chunk, chunk), 0)
    c_i = lax.broadcasted_iota(jnp.int32, (chunk, chunk), 1)
    incl = r_i >= c_i
    strict = r_i > c_i
    eye = (r_i == c_i).astype(F32)
    rr = lax.broadcasted_iota(jnp.int32, (rows, rows), 0)
    cc = lax.broadcasted_iota(jnp.int32, (rows, rows), 1)
    same_chunk = (rr // chunk) == (cc // chunk)
    lower_incl = (same_chunk & (rr >= cc)).astype(BF16)
    upper_incl = (same_chunk & (rr <= cc)).astype(BF16)

    def softplus(t):
        return jnp.maximum(t, 0.0) + jnp.log(1.0 + jnp.exp(-jnp.abs(t)))

    ba = ba_ref[0]
    g_cols = -jnp.exp(alog_l_ref[...]) * softplus(ba + dt_l_ref[...])
    gcum_cols = _dot_exactlhs_f32(lower_incl, g_cols)
    beta_cols = jax.nn.sigmoid(ba)
    bat = bat_ref[0]
    g_rows = -jnp.exp(alog_s_ref[...]) * softplus(bat + dt_s_ref[...])
    gcum_rows = _dot_f32_exactrhs(g_rows, upper_incl)

    pairs = [(ci, h) for ci in range(n_sub) for h in heads]

    pre = {}
    for ci, h in pairs:
        rs = slice(ci * chunk, (ci + 1) * chunk)
        gc_col = gcum_cols[rs, GDN_HEADS + h:GDN_HEADS + h + 1]
        gc_row = gcum_rows[GDN_HEADS + h:GDN_HEADS + h + 1, rs]
        gc_last = gc_row[:, chunk - 1:chunk]
        beta = beta_cols[rs, h:h + 1]
        gamma = jnp.where(incl, jnp.exp(gc_col - gc_row), 0.0)
        decay_in = jnp.exp(gc_col)
        q, k, v = qs[h][rs], ks[h][rs], vs[h][rs]
        kb = k * beta
        kq = _dot_nt(jnp.concatenate([kb, q], axis=0).astype(BF16), k.astype(BF16))
        pre[ci, h] = dict(
            n=jnp.where(strict, -(kq[:chunk] * gamma), 0.0),
            qk=(kq[chunk:] * gamma).astype(BF16),
            rhs=jnp.concatenate([v * beta, kb * decay_in], axis=1).astype(BF16),
            q_dec=(q * decay_in).astype(BF16),
            k_end=(k * jnp.exp(gc_last - gc_col)).astype(BF16),
            chunk_decay=jnp.exp(gc_last))

    n_rounds = max(1, (chunk - 1).bit_length())
    m_pow = {p: pre[p]["n"] for p in pairs}
    t_inv = {p: eye + pre[p]["n"] for p in pairs}
    for i in range(n_rounds):
        last = i == n_rounds - 1
        for p in pairs:
            m_bf = m_pow[p].astype(BF16)
            if i == 0:
                m_pow[p] = _dot(m_bf, m_bf)
            elif last:
                t_inv[p] = t_inv[p] + _dot(t_inv[p].astype(BF16), m_bf)
            else:
                both = _dot(jnp.concatenate([m_bf, t_inv[p].astype(BF16)], axis=0), m_bf)
                m_pow[p] = both[:chunk]
                t_inv[p] = t_inv[p] + both[chunk:]

    sol = {p: _dot(t_inv[p].astype(BF16), pre[p]["rhs"]) for p in pairs}

    state = [s_ref[0, h] for h in heads]
    o_rows = [[None] * n_sub for _ in heads]
    for ci in range(n_sub):
        ws = [_dot(jnp.concatenate([sol[ci, h][:, dk:].astype(BF16), pre[ci, h]["q_dec"]], axis=0),
                   state[h].astype(BF16)) for h in heads]
        v_new = [(sol[ci, h][:, :dk] - ws[h][:chunk]).astype(BF16) for h in heads]
        for h in heads:
            o_rows[h][ci] = ws[h][chunk:] + _dot(pre[ci, h]["qk"], v_new[h])
        state = [state[h] * pre[ci, h]["chunk_decay"] + _dot_tn(pre[ci, h]["k_end"], v_new[h])
                 for h in heads]
    for h in heads:
        s_ref[0, h] = state[h]

    z = z_ref[0]
    outs = []
    for h in heads:
        o_h = o_rows[h][0] if n_sub == 1 else jnp.concatenate(o_rows[h], axis=0)
        z_h = z[:, h * dk:(h + 1) * dk]
        outs.append(_rms(o_h, hg_ref[...]) * (z_h * jax.nn.sigmoid(z_h)))
    o_ref[0] = jnp.concatenate(outs, axis=1).astype(BF16)


def _gdn(u, z, ba, bat, hist, s0, conv_w, alog_l, dt_l, alog_s, dt_s, head_gain, *, chunk):
    b, l, conv_dim = u.shape
    width = conv_dim // 3
    dk = width // GDN_HEADS
    n_taps = conv_w.shape[0]
    rows = min(GDN_ROWS, l)
    kern = functools.partial(_gdn_kernel, rows=rows, chunk=chunk, width=width, n_taps=n_taps)

    def row_block(wd):
        return pl.BlockSpec((1, rows, wd), lambda bi, si: (bi, si, 0))

    state_spec = pl.BlockSpec((1, GDN_HEADS, dk, dk), lambda bi, si: (bi, 0, 0, 0))
    hist_spec = pl.BlockSpec((1, n_taps - 1, conv_dim), lambda bi, si: (bi, 0, 0))
    return pl.pallas_call(
        kern,
        grid=(b, l // rows),
        in_specs=[row_block(conv_dim), row_block(width), row_block(LANES),
                  pl.BlockSpec((1, 8, rows), lambda bi, si: (bi, 0, si)),
                  hist_spec, state_spec, _const_spec(conv_w.shape),
                  _const_spec((1, LANES)), _const_spec((1, LANES)),
                  _const_spec((8, 1)), _const_spec((8, 1)), _const_spec((1, dk))],
        out_specs=(row_block(width), state_spec, hist_spec),
        out_shape=(jax.ShapeDtypeStruct((b, l, width), BF16),
                   jax.ShapeDtypeStruct(s0.shape, F32),
                   jax.ShapeDtypeStruct(hist.shape, F32)),
        scratch_shapes=[pltpu.VMEM((rows + 8, conv_dim), F32)],
        compiler_params=_params("parallel", "arbitrary"),
        name="gated_delta_rule",
    )(u, z, ba, bat, hist, s0, conv_w, alog_l, dt_l, alog_s, dt_s, head_gain)


def _merge_kernel(x_ref, osb_ref, ogdn_ref, gin_ref, gout_ref, wgate_ref, wsb_ref, wgdn_ref, wout_ref, o_ref):
    x = x_ref[...]
    d = x.shape[1]
    h = _rms(x, gin_ref[...]).astype(BF16)
    gate_sb = jax.nn.sigmoid(_dot(h, wgate_ref[:, :d]))
    merged = gate_sb * _dot(osb_ref[...], wsb_ref[...])
    gate_gdn = jax.nn.sigmoid(_dot(h, wgate_ref[:, d:]))
    merged = merged + gate_gdn * _dot(ogdn_ref[...], wgdn_ref[...])
    m = _dot(merged.astype(BF16), wout_ref[...])
    o_ref[...] = x + _rms(m, gout_ref[...])


def _merge(x2, osb2, ogdn2, g_in, g_out, wgate, wsb, wgdn, wout):
    n, d = x2.shape
    tm = min(ROW_TILE, n)

    def rows(wd):
        return pl.BlockSpec((tm, wd), lambda i: (i, 0))

    return pl.pallas_call(
        _merge_kernel,
        grid=(n // tm,),
        in_specs=[rows(d), rows(osb2.shape[1]), rows(ogdn2.shape[1]),
                  _const_spec((1, d)), _const_spec((1, d)),
                  _const_spec(wgate.shape), _const_spec(wsb.shape), _const_spec(wgdn.shape),
                  _const_spec(wout.shape)],
        out_specs=rows(d),
        out_shape=jax.ShapeDtypeStruct((n, d), F32),
        compiler_params=_params("parallel"),
        name="branch_merge",
    )(x2, osb2, ogdn2, g_in, g_out, wgate, wsb, wgdn, wout)


def _layer_weights(l, norm_gains, w_in_bf, conv_w, gdn_a_log, gdn_dt_bias, gdn_norm_gain,
                   w_branch_sb, w_branch_gdn, w_out):
    d = w_in_bf.shape[1]
    sb_w = w_branch_sb.shape[1]
    gdn_w = w_branch_gdn.shape[1]
    conv_dim = conv_w.shape[2]
    n_a = 3 * sb_w + conv_dim + gdn_w
    wi = w_in_bf[l]
    wba = jnp.zeros((d, LANES), BF16).at[:, :2 * GDN_HEADS].set(wi[:, n_a:n_a + 2 * GDN_HEADS])

    def lane_vec(p):
        return jnp.zeros((1, LANES), F32).at[0, GDN_HEADS:2 * GDN_HEADS].set(p)

    def sublane_vec(p):
        return jnp.zeros((8, 1), F32).at[GDN_HEADS:2 * GDN_HEADS, 0].set(p)

    return dict(
        gains=[norm_gains[l, i][None, :] for i in range(6)],
        wa=wi[:, :n_a],
        wba=wba,
        wgate=wi[:, n_a + 2 * GDN_HEADS:],
        wsb=w_branch_sb[l].astype(BF16), wgdn=w_branch_gdn[l].astype(BF16), wout=w_out[l].astype(BF16),
        conv_w=conv_w[l],
        alog_l=lane_vec(gdn_a_log[l]), dt_l=lane_vec(gdn_dt_bias[l]),
        alog_s=sublane_vec(gdn_a_log[l]), dt_s=sublane_vec(gdn_dt_bias[l]),
        head_gain=gdn_norm_gain[l][None, :],
        dims=(sb_w, conv_dim, gdn_w),
    )


def _run_group(x, weights, ffn_w, past_k, past_v, conv_hist, s0, chunk):
    b, l, d = x.shape
    n = b * l
    s_list, c_list = [], []
    kv_stacks = ()
    for li, w in enumerate(weights):
        sb_w, conv_dim, gdn_w = w["dims"]
        dh = sb_w // SB_HEADS
        tq = min(SB_QUERY_TILE, l)
        td = min(SB_KEY_TILE, l)
        g = w["gains"]
        x2 = _ffn(x.reshape(n, d), g[0], g[1], ffn_w[0], ffn_w[1], li)
        q, k_stack, v_stack, kt, vb, u, z, ba, bat = _inproj(
            x2.reshape(b, l, d), g[2], w["wa"], w["wba"], kv_stacks,
            layer=li, depth=len(weights), sb_w=sb_w, conv_dim=conv_dim, gdn_w=gdn_w, sb_tile=td)
        kv_stacks = (k_stack, v_stack)
        if past_k is None:
            o_sb, _ = _sb_attention(q, kt, vb, kt, vb, tq=tq, td=td, tk=td, causal_past=True)
        else:
            p = past_k.shape[2]
            tk = min(SB_KEY_TILE, p)

            def past_layout(pk, pv):
                keys = pk.shape[1]
                ktp = pk.astype(BF16).reshape(b, keys // tk, tk, sb_w // SB_GROUP_LANES, SB_GROUP_LANES)
                return ktp.transpose(0, 3, 1, 4, 2), pv.astype(BF16).reshape(b, keys, sb_w)

            def attend(pk, pv):
                return _sb_attention(q, kt, vb, *past_layout(pk, pv), tq=tq, td=td, tk=tk,
                                     causal_past=False)

            o_sb, carry_min = attend(*lax.optimization_barrier(
                (past_k[li][:, p - tk:], past_v[li][:, p - tk:])))
            if p > tk:
                o_sb = lax.cond(jnp.min(carry_min) >= SB_ZERO_WEIGHT_LOG, lambda: o_sb,
                                lambda: attend(past_k[li], past_v[li])[0])
        o_gdn, s_fin, conv_new = _gdn(u, z, ba, bat, conv_hist[li], s0[li], w["conv_w"],
                                      w["alog_l"], w["dt_l"], w["alog_s"], w["dt_s"], w["head_gain"],
                                      chunk=chunk)
        x2 = _merge(x2, o_sb.reshape(n, sb_w), o_gdn.reshape(n, gdn_w), g[2], g[3],
                    w["wgate"], w["wsb"], w["wgdn"], w["wout"])
        x2 = _ffn(x2, g[4], g[5], ffn_w[2], ffn_w[3], li)
        x = x2.reshape(b, l, d)
        s_list.append(s_fin)
        c_list.append(conv_new)
    k_all, v_all = (t.reshape(len(weights), b, l, SB_HEADS, dh) for t in kv_stacks)
    return x, k_all, v_all, jnp.stack(s_list), jnp.stack(c_list)


def kernel(x_prompt, x_sample, cache_sb_k, cache_sb_v, state_gdn, state_conv, norm_gains,
           w_ffn1_up, w_ffn1_down, w_in, conv_w, gdn_a_log, gdn_dt_bias, gdn_norm_gain,
           w_branch_sb, w_branch_gdn, w_out, w_ffn2_up, w_ffn2_down):
    depth = w_in.shape[0]
    ffn_w = tuple(_to_bf16(w) for w in (w_ffn1_up, w_ffn1_down, w_ffn2_up, w_ffn2_down))
    w_in_bf = _to_bf16(w_in)
    weights = [_layer_weights(l, norm_gains, w_in_bf, conv_w, gdn_a_log, gdn_dt_bias, gdn_norm_gain,
                              w_branch_sb, w_branch_gdn, w_out) for l in range(depth)]
    bp = x_prompt.shape[0]
    chunk_prompt = 64
    zero_conv = jnp.zeros((depth, bp) + state_conv.shape[2:], state_conv.dtype)
    zero_state = jnp.zeros((depth, bp) + state_gdn.shape[2:], state_gdn.dtype)
    y_p, pk, pv, ps, pc = _run_group(x_prompt, weights, ffn_w, None, None, zero_conv, zero_state, chunk_prompt)
    y_s, sk, sv, ss, sc = _run_group(x_sample, weights, ffn_w, cache_sb_k, cache_sb_v, state_conv, state_gdn,
                                     x_sample.shape[1])
    return (y_p, y_s, pk, pv, ps, pc, sk, sv, ss, sc)
```

```python
import functools

import jax
import jax.numpy as jnp
from jax import lax
from jax.experimental import pallas as pl
from jax.experimental.pallas import tpu as pltpu

F32 = jnp.float32
BF16 = jnp.bfloat16

NORM_EPS = 1e-6
LOG2E = 1.4426950408889634
SB_HEADS = 8
GDN_HEADS = 4
LANES = 128
VMEM_LIMIT_BYTES = 56 * 1024 * 1024

ROW_TILE = 512
CAST_ROWS = 256
FF_CHUNK = 256
GDN_ROWS = 512
SB_KEY_TILE = 256
SB_QUERY_TILE = 256
SB_GROUP_LANES = 512
SB_ZERO_WEIGHT_LOG = 110.0
SB_MASKED_SCORE = -1e30
SB_NO_PAST_CARRY = 1e30


def _params(*sem):
    return pltpu.CompilerParams(dimension_semantics=sem, vmem_limit_bytes=VMEM_LIMIT_BYTES)


def _const_spec(shape):
    zeros = (0,) * len(shape)
    return pl.BlockSpec(shape, lambda *_: zeros)


def _rms(x, gain):
    ms = jnp.mean(x * x, axis=-1, keepdims=True)
    return x * lax.rsqrt(ms + NORM_EPS) * gain


def _dot(a, b):
    return jnp.dot(a, b, preferred_element_type=F32)


def _dot_nt(a, b):
    return lax.dot_general(a, b, (((1,), (1,)), ((), ())), preferred_element_type=F32)


def _dot_tn(a, b):
    return lax.dot_general(a, b, (((0,), (0,)), ((), ())), preferred_element_type=F32)


def _split3(x):
    hi = x.astype(BF16)
    r = x - hi.astype(F32)
    mid = r.astype(BF16)
    lo = (r - mid.astype(F32)).astype(BF16)
    return hi, mid, lo


def _dot_f32_exactrhs(a, b_bf16):
    hi, mid, lo = _split3(a)
    return _dot(hi, b_bf16) + _dot(mid, b_bf16) + _dot(lo, b_bf16)


def _dot_exactlhs_f32(a_bf16, b):
    hi, mid, lo = _split3(b)
    return _dot(a_bf16, hi) + _dot(a_bf16, mid) + _dot(a_bf16, lo)


def _ffn_kernel(x_ref, gin_ref, gout_ref, wup_ref, wd_ref, o_ref, act_ref):
    x = x_ref[...]
    h = _rms(x, gin_ref[...]).astype(BF16)
    d_ff = wd_ref.shape[1]
    for c in range(d_ff // FF_CHUNK):
        sl = slice(c * FF_CHUNK, (c + 1) * FF_CHUNK)
        g = _dot(h, wup_ref[0, :, sl])
        u = _dot(h, wup_ref[0, :, d_ff + c * FF_CHUNK:d_ff + (c + 1) * FF_CHUNK])
        act_ref[:, sl] = (g * jax.nn.sigmoid(g) * u).astype(BF16)
    y = _dot(act_ref[...], wd_ref[0])
    o_ref[...] = x + 0.5 * _rms(y, gout_ref[...])


def _ffn(x2, g_in, g_out, w_up, w_down, layer):
    n, d = x2.shape
    d_ff = w_down.shape[1]
    tm = min(ROW_TILE, n)
    row = pl.BlockSpec((tm, d), lambda i: (i, 0))
    return pl.pallas_call(
        _ffn_kernel,
        grid=(n // tm,),
        in_specs=[row, _const_spec((1, d)), _const_spec((1, d)),
                  pl.BlockSpec((1, d, 2 * d_ff), lambda i: (layer, 0, 0)),
                  pl.BlockSpec((1, d_ff, d), lambda i: (layer, 0, 0))],
        out_specs=row,
        out_shape=jax.ShapeDtypeStruct((n, d), F32),
        scratch_shapes=[pltpu.VMEM((tm, d_ff), BF16)],
        compiler_params=_params("parallel"),
        name="ffn_half_step",
    )(x2, g_in, g_out, w_up, w_down)


def _cast_kernel(w_ref, o_ref):
    o_ref[...] = w_ref[...].astype(o_ref.dtype)


def _to_bf16(w):
    depth, rows, cols = w.shape
    tr = min(rows, CAST_ROWS)
    assert rows % tr == 0
    spec = pl.BlockSpec((1, tr, cols), lambda li, ri: (li, ri, 0))
    return pl.pallas_call(
        _cast_kernel,
        grid=(depth, rows // tr),
        in_specs=[spec],
        out_specs=spec,
        out_shape=jax.ShapeDtypeStruct(w.shape, BF16),
        compiler_params=_params("parallel", "parallel"),
        name="weights_to_bf16",
    )(w)


def _inproj_kernel(x_ref, g_ref, wa_ref, wba_ref, *refs, sb_w, conv_dim, gdn_w, sb_tile, q_scale):
    q_ref, k_ref, v_ref, kt_ref, vb_ref, u_ref, z_ref, ba_ref, bat_ref = refs[-9:]
    h = _rms(x_ref[0], g_ref[...]).astype(BF16)
    tm = h.shape[0]

    def proj(lo, width):
        return _dot(h, wa_ref[:, lo:lo + width])

    q_ref[0] = (proj(0, sb_w) * q_scale).astype(BF16)
    k = proj(sb_w, sb_w)
    k_ref[0, 0] = k
    for later in range(1, k_ref.shape[0]):
        k_ref[later, 0] = jnp.zeros_like(k)
        v_ref[later, 0] = jnp.zeros_like(k)
    kt = k.T.astype(BF16)
    for hp in range(sb_w // SB_GROUP_LANES):
        for s in range(tm // sb_tile):
            kt_ref[0, hp, s] = kt[hp * SB_GROUP_LANES:(hp + 1) * SB_GROUP_LANES,
                                  s * sb_tile:(s + 1) * sb_tile]
    v = proj(2 * sb_w, sb_w)
    v_ref[0, 0] = v
    vb_ref[0] = v.astype(BF16)
    u_ref[0] = proj(3 * sb_w, conv_dim)
    z_ref[0] = proj(3 * sb_w + conv_dim, gdn_w)
    ba = _dot(h, wba_ref[...])
    ba_ref[0] = ba
    bat_ref[0] = ba.T[:8]


def _inproj(x, gain, wa, wba, kv_stacks, *, layer, depth, sb_w, conv_dim, gdn_w, sb_tile):
    b, l, d = x.shape
    tm = min(ROW_TILE, l)
    n_hp = sb_w // SB_GROUP_LANES
    kern = functools.partial(_inproj_kernel, sb_w=sb_w, conv_dim=conv_dim, gdn_w=gdn_w,
                             sb_tile=sb_tile,
                             q_scale=float((sb_w // SB_HEADS) ** -0.5))

    def rows(width):
        return pl.BlockSpec((1, tm, width), lambda bi, ti: (bi, ti, 0))

    out_shape = (
        jax.ShapeDtypeStruct((b, l, sb_w), BF16),
        jax.ShapeDtypeStruct((depth, b, l, sb_w), F32),
        jax.ShapeDtypeStruct((depth, b, l, sb_w), F32),
        jax.ShapeDtypeStruct((b, n_hp, l // sb_tile, SB_GROUP_LANES, sb_tile), BF16),
        jax.ShapeDtypeStruct((b, l, sb_w), BF16),
        jax.ShapeDtypeStruct((b, l, conv_dim), F32),
        jax.ShapeDtypeStruct((b, l, gdn_w), F32),
        jax.ShapeDtypeStruct((b, l, LANES), F32),
        jax.ShapeDtypeStruct((b, 8, l), F32),
    )
    if kv_stacks:
        layer_rows = pl.BlockSpec((1, 1, tm, sb_w), lambda bi, ti: (layer, bi, ti, 0))
    else:
        assert layer == 0
        layer_rows = pl.BlockSpec((depth, 1, tm, sb_w), lambda bi, ti: (0, bi, ti, 0))
    out_specs = (
        rows(sb_w), layer_rows, layer_rows,
        pl.BlockSpec((1, n_hp, tm // sb_tile, SB_GROUP_LANES, sb_tile), lambda bi, ti: (bi, 0, ti, 0, 0)),
        rows(sb_w), rows(conv_dim), rows(gdn_w), rows(LANES),
        pl.BlockSpec((1, 8, tm), lambda bi, ti: (bi, 0, ti)),
    )
    return pl.pallas_call(
        kern,
        grid=(b, l // tm),
        in_specs=[rows(d), _const_spec((1, d)), _const_spec(wa.shape), _const_spec(wba.shape)]
        + [pl.BlockSpec(memory_space=pl.ANY)] * len(kv_stacks),
        out_specs=out_specs,
        out_shape=out_shape,
        input_output_aliases={4 + i: 1 + i for i in range(len(kv_stacks))},
        compiler_params=_params("parallel", "parallel"),
        name="in_projection",
    )(x, gain, wa, wba, *kv_stacks)


def _sb_kernel(q_ref, ktd_ref, vd_ref, ktp_ref, vp_ref, o_ref, carry_ref, *, tq, td, nd, tk, heads, dh,
               causal_past, n_past_static):
    qi = pl.program_id(2)

    def later_key_matrix(n):
        r = lax.broadcasted_iota(jnp.int32, (n, n), 0)
        c = lax.broadcasted_iota(jnp.int32, (n, n), 1)
        return (r > c).astype(BF16)

    def sweep(blocks, st):
        pairs = [(blk, hh) for blk in blocks for hh in range(heads)]
        scores = []
        for blk, hh in pairs:
            lanes = slice(hh * dh, (hh + 1) * dh)
            z = _dot(q_all[:, lanes], blk["kt"][lanes, :])
            if blk["visible"] is not None:
                z = jnp.where(blk["visible"], z, SB_MASKED_SCORE)
            nlk = jnp.maximum(z, 0.0) + jnp.log(1.0 + jnp.exp2(jnp.abs(z) * (-LOG2E)))
            scores.append((z - nlk, nlk.astype(BF16), jnp.sum(nlk, axis=1, keepdims=True)))
        csums = [_dot(nlk_b, blk["u"]) for (blk, _), (_, nlk_b, _) in zip(pairs, scores)]
        st = list(st)
        for (blk, hh), (log_beta, _, row_sum), csum in zip(pairs, scores, csums):
            lanes = slice(hh * dh, (hh + 1) * dh)
            carry, acc = st[2 * hh], st[2 * hh + 1]
            if carry is not None and blk.get("carry_bias") is not None:
                carry = carry + blk["carry_bias"]
            log_a = log_beta - csum if carry is None else log_beta - csum - carry
            pv = _dot(jnp.exp(log_a).astype(BF16), blk["v"][:, lanes])
            st[2 * hh + 1] = pv if acc is None else acc + pv
            st[2 * hh] = row_sum if carry is None else carry + row_sum
        return st

    q_all = q_ref[0]
    state = [None] * (2 * heads)

    u_diag = later_key_matrix(td)
    u_past = u_diag if tk == td else later_key_matrix(tk)
    rows_i = lax.broadcasted_iota(jnp.int32, (tq, td), 0)
    cols_i = lax.broadcasted_iota(jnp.int32, (tq, td), 1)
    n_past = qi * (tq // tk) if causal_past else n_past_static

    def past_block(j, carry_bias=None):
        row0 = pl.multiple_of(j * tk, tk)
        return dict(kt=ktp_ref[0, 0, j].astype(BF16), v=vp_ref[0, pl.ds(row0, tk), :].astype(BF16),
                    visible=None, u=u_past, carry_bias=carry_bias)

    first = [dict(kt=ktd_ref[0, 0, sd], v=vd_ref[0, sd * td:(sd + 1) * td, :],
                  visible=cols_i + sd * td < rows_i, u=u_diag) for sd in reversed(range(nd))]
    no_past_bias = jnp.where(n_past > 0, 0.0, SB_NO_PAST_CARRY) if causal_past else None
    first.append(past_block(jnp.maximum(n_past - 1, 0), no_past_bias))
    state = sweep(first, state)

    def min_carry(st):
        m = st[0]
        for hh in range(1, heads):
            m = jnp.minimum(m, st[2 * hh])
        return jnp.min(m)

    def cond(loop):
        i, smallest, _ = loop
        return jnp.logical_and(i < n_past, smallest < SB_ZERO_WEIGHT_LOG)

    def body(loop):
        i, _, st = loop
        st = sweep([past_block(n_past - 1 - i)], st)
        return i + 1, min_carry(st), tuple(st)

    _, smallest, st = lax.while_loop(cond, body, (jnp.int32(1), min_carry(state), tuple(state)))
    o_ref[0] = jnp.concatenate([st[2 * hh + 1] for hh in range(heads)], axis=1).astype(BF16)
    carry_ref[...] = jnp.full(carry_ref.shape, smallest, F32)


def _sb_attention(q, kt_diag, v_diag, kt_past, v_past, *, tq, td, tk, causal_past):
    b, l, w = q.shape
    n_g, hw = kt_diag.shape[1], kt_diag.shape[3]
    n_past_blocks = kt_past.shape[2]
    assert tq % td == 0 and (not causal_past or tq % tk == 0)
    p = v_past.shape[1]
    dh = w // SB_HEADS
    kern = functools.partial(_sb_kernel, tq=tq, td=td, nd=tq // td, tk=tk, heads=hw // dh, dh=dh,
                             causal_past=causal_past, n_past_static=n_past_blocks)
    return pl.pallas_call(
        kern,
        grid=(b, n_g, l // tq),
        in_specs=[
            pl.BlockSpec((1, tq, hw), lambda bi, hp, qi: (bi, qi, hp)),
            pl.BlockSpec((1, 1, tq // td, hw, td), lambda bi, hp, qi: (bi, hp, qi, 0, 0)),
            pl.BlockSpec((1, tq, hw), lambda bi, hp, qi: (bi, qi, hp)),
            pl.BlockSpec((1, 1, n_past_blocks, hw, tk), lambda bi, hp, qi: (bi, hp, 0, 0, 0),
                         pipeline_mode=pl.Buffered(1)),
            pl.BlockSpec((1, p, hw), lambda bi, hp, qi: (bi, 0, hp), pipeline_mode=pl.Buffered(1)),
        ],
        out_specs=(pl.BlockSpec((1, tq, hw), lambda bi, hp, qi: (bi, qi, hp)),
                   pl.BlockSpec((1, 1, 1, 8, LANES), lambda bi, hp, qi: (bi, hp, qi, 0, 0))),
        out_shape=(jax.ShapeDtypeStruct((b, l, w), BF16),
                   jax.ShapeDtypeStruct((b, n_g, l // tq, 8, LANES), F32)),
        compiler_params=_params("parallel", "parallel", "arbitrary"),
        name="stick_breaking_attention",
    )(q, kt_diag, v_diag, kt_past, v_past)


def _gdn_kernel(u_ref, z_ref, ba_ref, bat_ref, hist_ref, s0_ref, cw_ref,
                alog_l_ref, dt_l_ref, alog_s_ref, dt_s_ref, hg_ref,
                o_ref, s_ref, cnew_ref, ext_ref, *, rows, chunk, width, n_taps):
    step = pl.program_id(1)
    n_steps = pl.num_programs(1)
    dk = width // GDN_HEADS
    n_sub = rows // chunk
    pad = 8
    n_hist = n_taps - 1

    @pl.when(step == 0)
    def _():
        ext_ref[0:pad, :] = jnp.zeros((pad, ext_ref.shape[1]), F32)
        ext_ref[pad - n_hist:pad, :] = hist_ref[0]
        s_ref[...] = s0_ref[...]

    ext_ref[pad:pad + rows, :] = u_ref[0]
    y = ext_ref[pad:pad + rows, :] * cw_ref[n_hist:n_taps, :]
    for i in reversed(range(n_hist)):
        y = y + ext_ref[pad - n_hist + i:pad - n_hist + i + rows, :] * cw_ref[i:i + 1, :]
    new_tail = ext_ref[rows:rows + pad, :]
    ext_ref[0:pad, :] = new_tail

    @pl.when(step == n_steps - 1)
    def _():
        cnew_ref[0] = new_tail[pad - n_hist:pad, :]

    qkv = y * jax.nn.sigmoid(y)

    heads = range(GDN_HEADS)
    qs, ks, vs = [], [], []
    for h in heads:
        q = qkv[:, h * dk:(h + 1) * dk]
        k = qkv[:, width + h * dk:width + (h + 1) * dk]
        qs.append(q * lax.rsqrt(jnp.sum(q * q, axis=-1, keepdims=True) + NORM_EPS) * float(dk ** -0.5))
        ks.append(k * lax.rsqrt(jnp.sum(k * k, axis=-1, keepdims=True) + NORM_EPS))
        vs.append(qkv[:, 2 * width + h * dk:2 * width + (h + 1) * dk])

    r_i = lax.broadcasted_iota(jnp.int32, (chunk, chunk), 0)
    c_i = lax.broadcasted_iota(jnp.int32, (chunk, chunk), 1)
    incl = r_i >= c_i
    strict = r_i > c_i
    eye = (r_i == c_i).astype(F32)
    rr = lax.broadcasted_iota(jnp.int32, (rows, rows), 0)
    cc = lax.broadcasted_iota(jnp.int32, (rows, rows), 1)
    same_chunk = (rr // chunk) == (cc // chunk)
    lower_incl = (same_chunk & (rr >= cc)).astype(BF16)
    upper_incl = (same_chunk & (rr <= cc)).astype(BF16)

    def softplus(t):
        return jnp.maximum(t, 0.0) + jnp.log(1.0 + jnp.exp(-jnp.abs(t)))

    ba = ba_ref[0]
    g_cols = -jnp.exp(alog_l_ref[...]) * softplus(ba + dt_l_ref[...])
    gcum_cols = _dot_exactlhs_f32(lower_incl, g_cols)
    beta_cols = jax.nn.sigmoid(ba)
    bat = bat_ref[0]
    g_rows = -jnp.exp(alog_s_ref[...]) * softplus(bat + dt_s_ref[...])
    gcum_rows = _dot_f32_exactrhs(g_rows, upper_incl)

    pairs = [(ci, h) for ci in range(n_sub) for h in heads]

    pre = {}
    for ci, h in pairs:
        rs = slice(ci * chunk, (ci + 1) * chunk)
        gc_col = gcum_cols[rs, GDN_HEADS + h:GDN_HEADS + h + 1]
        gc_row = gcum_rows[GDN_HEADS + h:GDN_HEADS + h + 1, rs]
        gc_last = gc_row[:, chunk - 1:chunk]
        beta = beta_cols[rs, h:h + 1]
        gamma = jnp.where(incl, jnp.exp(gc_col - gc_row), 0.0)
        decay_in = jnp.exp(gc_col)
        q, k, v = qs[h][rs], ks[h][rs], vs[h][rs]
        kb = k * beta
        kq = _dot_nt(jnp.concatenate([kb, q], axis=0).astype(BF16), k.astype(BF16))
        pre[ci, h] = dict(
            n=jnp.where(strict, -(kq[:chunk] * gamma), 0.0),
            qk=(kq[chunk:] * gamma).astype(BF16),
            rhs=jnp.concatenate([v * beta, kb * decay_in], axis=1).astype(BF16),
            q_dec=(q * decay_in).astype(BF16),
            k_end=(k * jnp.exp(gc_last - gc_col)).astype(BF16),
            chunk_decay=jnp.exp(gc_last))

    n_rounds = max(1, (chunk - 1).bit_length())
    m_pow = {p: pre[p]["n"] for p in pairs}
    t_inv = {p: eye + pre[p]["n"] for p in pairs}
    for i in range(n_rounds):
        last = i == n_rounds - 1
        for p in pairs:
            m_bf = m_pow[p].astype(BF16)
            if i == 0:
                m_pow[p] = _dot(m_bf, m_bf)
            elif last:
                t_inv[p] = t_inv[p] + _dot(t_inv[p].astype(BF16), m_bf)
            else:
                both = _dot(jnp.concatenate([m_bf, t_inv[p].astype(BF16)], axis=0), m_bf)
                m_pow[p] = both[:chunk]
                t_inv[p] = t_inv[p] + both[chunk:]

    sol = {p: _dot(t_inv[p].astype(BF16), pre[p]["rhs"]) for p in pairs}

    state = [s_ref[0, h] for h in heads]
    o_rows = [[None] * n_sub for _ in heads]
    for ci in range(n_sub):
        ws = [_dot(jnp.concatenate([sol[ci, h][:, dk:].astype(BF16), pre[ci, h]["q_dec"]], axis=0),
                   state[h].astype(BF16)) for h in heads]
        v_new = [(sol[ci, h][:, :dk] - ws[h][:chunk]).astype(BF16) for h in heads]
        for h in heads:
            o_rows[h][ci] = ws[h][chunk:] + _dot(pre[ci, h]["qk"], v_new[h])
        state = [state[h] * pre[ci, h]["chunk_decay"] + _dot_tn(pre[ci, h]["k_end"], v_new[h])
                 for h in heads]
    for h in heads:
        s_ref[0, h] = state[h]

    z = z_ref[0]
    outs = []
    for h in heads:
        o_h = o_rows[h][0] if n_sub == 1 else jnp.concatenate(o_rows[h], axis=0)
        z_h = z[:, h * dk:(h + 1) * dk]
        outs.append(_rms(o_h, hg_ref[...]) * (z_h * jax.nn.sigmoid(z_h)))
    o_ref[0] = jnp.concatenate(outs, axis=1).astype(BF16)


def _gdn(u, z, ba, bat, hist, s0, conv_w, alog_l, dt_l, alog_s, dt_s, head_gain, *, chunk):
    b, l, conv_dim = u.shape
    width = conv_dim // 3
    dk = width // GDN_HEADS
    n_taps = conv_w.shape[0]
    rows = min(GDN_ROWS, l)
    kern = functools.partial(_gdn_kernel, rows=rows, chunk=chunk, width=width, n_taps=n_taps)

    def row_block(wd):
        return pl.BlockSpec((1, rows, wd), lambda bi, si: (bi, si, 0))

    state_spec = pl.BlockSpec((1, GDN_HEADS, dk, dk), lambda bi, si: (bi, 0, 0, 0))
    hist_spec = pl.BlockSpec((1, n_taps - 1, conv_dim), lambda bi, si: (bi, 0, 0))
    return pl.pallas_call(
        kern,
        grid=(b, l // rows),
        in_specs=[row_block(conv_dim), row_block(width), row_block(LANES),
                  pl.BlockSpec((1, 8, rows), lambda bi, si: (bi, 0, si)),
                  hist_spec, state_spec, _const_spec(conv_w.shape),
                  _const_spec((1, LANES)), _const_spec((1, LANES)),
                  _const_spec((8, 1)), _const_spec((8, 1)), _const_spec((1, dk))],
        out_specs=(row_block(width), state_spec, hist_spec),
        out_shape=(jax.ShapeDtypeStruct((b, l, width), BF16),
                   jax.ShapeDtypeStruct(s0.shape, F32),
                   jax.ShapeDtypeStruct(hist.shape, F32)),
        scratch_shapes=[pltpu.VMEM((rows + 8, conv_dim), F32)],
        compiler_params=_params("parallel", "arbitrary"),
        name="gated_delta_rule",
    )(u, z, ba, bat, hist, s0, conv_w, alog_l, dt_l, alog_s, dt_s, head_gain)


def _merge_kernel(x_ref, osb_ref, ogdn_ref, gin_ref, gout_ref, wgate_ref, wsb_ref, wgdn_ref, wout_ref, o_ref):
    x = x_ref[...]
    d = x.shape[1]
    h = _rms(x, gin_ref[...]).astype(BF16)
    gate_sb = jax.nn.sigmoid(_dot(h, wgate_ref[:, :d]))
    merged = gate_sb * _dot(osb_ref[...], wsb_ref[...])
    gate_gdn = jax.nn.sigmoid(_dot(h, wgate_ref[:, d:]))
    merged = merged + gate_gdn * _dot(ogdn_ref[...], wgdn_ref[...])
    m = _dot(merged.astype(BF16), wout_ref[...])
    o_ref[...] = x + _rms(m, gout_ref[...])


def _merge(x2, osb2, ogdn2, g_in, g_out, wgate, wsb, wgdn, wout):
    n, d = x2.shape
    tm = min(ROW_TILE, n)

    def rows(wd):
        return pl.BlockSpec((tm, wd), lambda i: (i, 0))

    return pl.pallas_call(
        _merge_kernel,
        grid=(n // tm,),
        in_specs=[rows(d), rows(osb2.shape[1]), rows(ogdn2.shape[1]),
                  _const_spec((1, d)), _const_spec((1, d)),
                  _const_spec(wgate.shape), _const_spec(wsb.shape), _const_spec(wgdn.shape),
                  _const_spec(wout.shape)],
        out_specs=rows(d),
        out_shape=jax.ShapeDtypeStruct((n, d), F32),
        compiler_params=_params("parallel"),
        name="branch_merge",
    )(x2, osb2, ogdn2, g_in, g_out, wgate, wsb, wgdn, wout)


def _layer_weights(l, norm_gains, w_in_bf, conv_w, gdn_a_log, gdn_dt_bias, gdn_norm_gain,
                   w_branch_sb, w_branch_gdn, w_out):
    d = w_in_bf.shape[1]
    sb_w = w_branch_sb.shape[1]
    gdn_w = w_branch_gdn.shape[1]
    conv_dim = conv_w.shape[2]
    n_a = 3 * sb_w + conv_dim + gdn_w
    wi = w_in_bf[l]
    wba = jnp.zeros((d, LANES), BF16).at[:, :2 * GDN_HEADS].set(wi[:, n_a:n_a + 2 * GDN_HEADS])

    def lane_vec(p):
        return jnp.zeros((1, LANES), F32).at[0, GDN_HEADS:2 * GDN_HEADS].set(p)

    def sublane_vec(p):
        return jnp.zeros((8, 1), F32).at[GDN_HEADS:2 * GDN_HEADS, 0].set(p)

    return dict(
        gains=[norm_gains[l, i][None, :] for i in range(6)],
        wa=wi[:, :n_a],
        wba=wba,
        wgate=wi[:, n_a + 2 * GDN_HEADS:],
        wsb=w_branch_sb[l].astype(BF16), wgdn=w_branch_gdn[l].astype(BF16), wout=w_out[l].astype(BF16),
        conv_w=conv_w[l],
        alog_l=lane_vec(gdn_a_log[l]), dt_l=lane_vec(gdn_dt_bias[l]),
        alog_s=sublane_vec(gdn_a_log[l]), dt_s=sublane_vec(gdn_dt_bias[l]),
        head_gain=gdn_norm_gain[l][None, :],
        dims=(sb_w, conv_dim, gdn_w),
    )


def _run_group(x, weights, ffn_w, past_k, past_v, conv_hist, s0, chunk):
    b, l, d = x.shape
    n = b * l
    s_list, c_list = [], []
    kv_stacks = ()
    for li, w in enumerate(weights):
        sb_w, conv_dim, gdn_w = w["dims"]
        dh = sb_w // SB_HEADS
        tq = min(SB_QUERY_TILE, l)
        td = min(SB_KEY_TILE, l)
        g = w["gains"]
        x2 = _ffn(x.reshape(n, d), g[0], g[1], ffn_w[0], ffn_w[1], li)
        q, k_stack, v_stack, kt, vb, u, z, ba, bat = _inproj(
            x2.reshape(b, l, d), g[2], w["wa"], w["wba"], kv_stacks,
            layer=li, depth=len(weights), sb_w=sb_w, conv_dim=conv_dim, gdn_w=gdn_w, sb_tile=td)
        kv_stacks = (k_stack, v_stack)
        if past_k is None:
            o_sb, _ = _sb_attention(q, kt, vb, kt, vb, tq=tq, td=td, tk=td, causal_past=True)
        else:
            p = past_k.shape[2]
            tk = min(SB_KEY_TILE, p)

            def attend(pk, pv):
                keys = pk.shape[1]
                ktp = pk.reshape(b, keys // tk, tk, sb_w // SB_GROUP_LANES, SB_GROUP_LANES)
                return _sb_attention(q, kt, vb, ktp.transpose(0, 3, 1, 4, 2), pv.reshape(b, keys, sb_w),
                                     tq=tq, td=td, tk=tk, causal_past=False)

            newest = lax.optimization_barrier((past_k[li][:, p - tk:], past_v[li][:, p - tk:]))
            o_sb, carry_min = attend(*(t.astype(BF16) for t in newest))
            if p > tk:
                o_sb = lax.cond(jnp.min(carry_min) >= SB_ZERO_WEIGHT_LOG, lambda: o_sb,
                                lambda: attend(past_k[li], past_v[li])[0])
        o_gdn, s_fin, conv_new = _gdn(u, z, ba, bat, conv_hist[li], s0[li], w["conv_w"],
                                      w["alog_l"], w["dt_l"], w["alog_s"], w["dt_s"], w["head_gain"],
                                      chunk=chunk)
        x2 = _merge(x2, o_sb.reshape(n, sb_w), o_gdn.reshape(n, gdn_w), g[2], g[3],
                    w["wgate"], w["wsb"], w["wgdn"], w["wout"])
        x2 = _ffn(x2, g[4], g[5], ffn_w[2], ffn_w[3], li)
        x = x2.reshape(b, l, d)
        s_list.append(s_fin)
        c_list.append(conv_new)
    k_all, v_all = (t.reshape(len(weights), b, l, SB_HEADS, dh) for t in kv_stacks)
    return x, k_all, v_all, jnp.stack(s_list), jnp.stack(c_list)


def kernel(x_prompt, x_sample, cache_sb_k, cache_sb_v, state_gdn, state_conv, norm_gains,
           w_ffn1_up, w_ffn1_down, w_in, conv_w, gdn_a_log, gdn_dt_bias, gdn_norm_gain,
           w_branch_sb, w_branch_gdn, w_out, w_ffn2_up, w_ffn2_down):
    depth = w_in.shape[0]
    ffn_w = tuple(_to_bf16(w) for w in (w_ffn1_up, w_ffn1_down, w_ffn2_up, w_ffn2_down))
    w_in_bf = w_in.astype(BF16)
    weights = [_layer_weights(l, norm_gains, w_in_bf, conv_w, gdn_a_log, gdn_dt_bias, gdn_norm_gain,
                              w_branch_sb, w_branch_gdn, w_out) for l in range(depth)]
    bp = x_prompt.shape[0]
    chunk_prompt = 64
    zero_conv = jnp.zeros((depth, bp) + state_conv.shape[2:], state_conv.dtype)
    zero_state = jnp.zeros((depth, bp) + state_gdn.shape[2:], state_gdn.dtype)
    y_p, pk, pv, ps, pc = _run_group(x_prompt, weights, ffn_w, None, None, zero_conv, zero_state, chunk_prompt)
    y_s, sk, sv, ss, sc = _run_group(x_sample, weights, ffn_w, cache_sb_k, cache_sb_v, state_conv, state_gdn,
                                     x_sample.shape[1])
    return (y_p, y_s, pk, pv, ps, pc, sk, sv, ss, sc)
```

```python
import functools

import jax
import jax.numpy as jnp
from jax import lax
from jax.experimental import pallas as pl
from jax.experimental.pallas import tpu as pltpu

F32 = jnp.float32
BF16 = jnp.bfloat16

NORM_EPS = 1e-6
LOG2E = 1.4426950408889634
SB_HEADS = 8
GDN_HEADS = 4
LANES = 128
VMEM_LIMIT_BYTES = 56 * 1024 * 1024

ROW_TILE = 512
CAST_ROWS = 256
FF_CHUNK = 256
GDN_ROWS = 512
SB_KEY_TILE = 256
SB_QUERY_TILE = 256
SB_GROUP_LANES = 512
SB_ZERO_WEIGHT_LOG = 110.0
SB_MASKED_SCORE = -1e30
SB_NO_PAST_CARRY = 1e30


def _params(*sem):
    return pltpu.CompilerParams(dimension_semantics=sem, vmem_limit_bytes=VMEM_LIMIT_BYTES)


def _const_spec(shape):
    zeros = (0,) * len(shape)
    return pl.BlockSpec(shape, lambda *_: zeros)


def _rms(x, gain):
    ms = jnp.mean(x * x, axis=-1, keepdims=True)
    return x * lax.rsqrt(ms + NORM_EPS) * gain


def _dot(a, b):
    return jnp.dot(a, b, preferred_element_type=F32)


def _dot_nt(a, b):
    return lax.dot_general(a, b, (((1,), (1,)), ((), ())), preferred_element_type=F32)


def _dot_tn(a, b):
    return lax.dot_general(a, b, (((0,), (0,)), ((), ())), preferred_element_type=F32)


def _split3(x):
    hi = x.astype(BF16)
    r = x - hi.astype(F32)
    mid = r.astype(BF16)
    lo = (r - mid.astype(F32)).astype(BF16)
    return hi, mid, lo


def _dot_f32_exactrhs(a, b_bf16):
    hi, mid, lo = _split3(a)
    return _dot(hi, b_bf16) + _dot(mid, b_bf16) + _dot(lo, b_bf16)


def _dot_exactlhs_f32(a_bf16, b):
    hi, mid, lo = _split3(b)
    return _dot(a_bf16, hi) + _dot(a_bf16, mid) + _dot(a_bf16, lo)


def _ffn_kernel(x_ref, gin_ref, gout_ref, wup_ref, wd_ref, o_ref, act_ref):
    x = x_ref[...]
    h = _rms(x, gin_ref[...]).astype(BF16)
    d_ff = wd_ref.shape[1]
    for c in range(d_ff // FF_CHUNK):
        sl = slice(c * FF_CHUNK, (c + 1) * FF_CHUNK)
        g = _dot(h, wup_ref[0, :, sl])
        u = _dot(h, wup_ref[0, :, d_ff + c * FF_CHUNK:d_ff + (c + 1) * FF_CHUNK])
        act_ref[:, sl] = (g * jax.nn.sigmoid(g) * u).astype(BF16)
    y = _dot(act_ref[...], wd_ref[0])
    o_ref[...] = x + 0.5 * _rms(y, gout_ref[...])


def _ffn(x2, g_in, g_out, w_up, w_down, layer):
    n, d = x2.shape
    d_ff = w_down.shape[1]
    tm = min(ROW_TILE, n)
    row = pl.BlockSpec((tm, d), lambda i: (i, 0))
    return pl.pallas_call(
        _ffn_kernel,
        grid=(n // tm,),
        in_specs=[row, _const_spec((1, d)), _const_spec((1, d)),
                  pl.BlockSpec((1, d, 2 * d_ff), lambda i: (layer, 0, 0)),
                  pl.BlockSpec((1, d_ff, d), lambda i: (layer, 0, 0))],
        out_specs=row,
        out_shape=jax.ShapeDtypeStruct((n, d), F32),
        scratch_shapes=[pltpu.VMEM((tm, d_ff), BF16)],
        compiler_params=_params("parallel"),
        name="ffn_half_step",
    )(x2, g_in, g_out, w_up, w_down)


def _cast_kernel(w_ref, o_ref):
    o_ref[...] = w_ref[...].astype(o_ref.dtype)


def _to_bf16(w):
    depth, rows, cols = w.shape
    tr = min(rows, CAST_ROWS)
    assert rows % tr == 0
    spec = pl.BlockSpec((1, tr, cols), lambda li, ri: (li, ri, 0))
    return pl.pallas_call(
        _cast_kernel,
        grid=(depth, rows // tr),
        in_specs=[spec],
        out_specs=spec,
        out_shape=jax.ShapeDtypeStruct(w.shape, BF16),
        compiler_params=_params("parallel", "parallel"),
        name="weights_to_bf16",
    )(w)


def _inproj_kernel(x_ref, g_ref, wa_ref, wba_ref, *refs, sb_w, conv_dim, gdn_w, sb_tile, q_scale):
    q_ref, k_ref, v_ref, kt_ref, vb_ref, u_ref, z_ref, ba_ref, bat_ref = refs[-9:]
    h = _rms(x_ref[0], g_ref[...]).astype(BF16)
    tm = h.shape[0]

    def proj(lo, width):
        return _dot(h, wa_ref[:, lo:lo + width])

    q_ref[0] = (proj(0, sb_w) * q_scale).astype(BF16)
    k = proj(sb_w, sb_w)
    k_ref[0, 0] = k
    for later in range(1, k_ref.shape[0]):
        k_ref[later, 0] = jnp.zeros_like(k)
        v_ref[later, 0] = jnp.zeros_like(k)
    kt = k.T.astype(BF16)
    for hp in range(sb_w // SB_GROUP_LANES):
        for s in range(tm // sb_tile):
            kt_ref[0, hp, s] = kt[hp * SB_GROUP_LANES:(hp + 1) * SB_GROUP_LANES,
                                  s * sb_tile:(s + 1) * sb_tile]
    v = proj(2 * sb_w, sb_w)
    v_ref[0, 0] = v
    vb_ref[0] = v.astype(BF16)
    u_ref[0] = proj(3 * sb_w, conv_dim)
    z_ref[0] = proj(3 * sb_w + conv_dim, gdn_w)
    ba = _dot(h, wba_ref[...])
    ba_ref[0] = ba
    bat_ref[0] = ba.T[:8]


def _inproj(x, gain, wa, wba, kv_stacks, *, layer, depth, sb_w, conv_dim, gdn_w, sb_tile):
    b, l, d = x.shape
    tm = min(ROW_TILE, l)
    n_hp = sb_w // SB_GROUP_LANES
    kern = functools.partial(_inproj_kernel, sb_w=sb_w, conv_dim=conv_dim, gdn_w=gdn_w,
                             sb_tile=sb_tile,
                             q_scale=float((sb_w // SB_HEADS) ** -0.5))

    def rows(width):
        return pl.BlockSpec((1, tm, width), lambda bi, ti: (bi, ti, 0))

    out_shape = (
        jax.ShapeDtypeStruct((b, l, sb_w), BF16),
        jax.ShapeDtypeStruct((depth, b, l, sb_w), F32),
        jax.ShapeDtypeStruct((depth, b, l, sb_w), F32),
        jax.ShapeDtypeStruct((b, n_hp, l // sb_tile, SB_GROUP_LANES, sb_tile), BF16),
        jax.ShapeDtypeStruct((b, l, sb_w), BF16),
        jax.ShapeDtypeStruct((b, l, conv_dim), F32),
        jax.ShapeDtypeStruct((b, l, gdn_w), F32),
        jax.ShapeDtypeStruct((b, l, LANES), F32),
        jax.ShapeDtypeStruct((b, 8, l), F32),
    )
    if kv_stacks:
        layer_rows = pl.BlockSpec((1, 1, tm, sb_w), lambda bi, ti: (layer, bi, ti, 0))
    else:
        assert layer == 0
        layer_rows = pl.BlockSpec((depth, 1, tm, sb_w), lambda bi, ti: (0, bi, ti, 0))
    out_specs = (
        rows(sb_w), layer_rows, layer_rows,
        pl.BlockSpec((1, n_hp, tm // sb_tile, SB_GROUP_LANES, sb_tile), lambda bi, ti: (bi, 0, ti, 0, 0)),
        rows(sb_w), rows(conv_dim), rows(gdn_w), rows(LANES),
        pl.BlockSpec((1, 8, tm), lambda bi, ti: (bi, 0, ti)),
    )
    return pl.pallas_call(
        kern,
        grid=(b, l // tm),
        in_specs=[rows(d), _const_spec((1, d)), _const_spec(wa.shape), _const_spec(wba.shape)]
        + [pl.BlockSpec(memory_space=pl.ANY)] * len(kv_stacks),
        out_specs=out_specs,
        out_shape=out_shape,
        input_output_aliases={4 + i: 1 + i for i in range(len(kv_stacks))},
        compiler_params=_params("parallel", "parallel"),
        name="in_projection",
    )(x, gain, wa, wba, *kv_stacks)


def _sb_kernel(q_ref, ktd_ref, vd_ref, ktp_ref, vp_ref, o_ref, carry_ref, *, tq, td, nd, tk, heads, dh,
               causal_past, n_past_static, past_keys_on_rows):
    qi = pl.program_id(2)

    def later_key_matrix(n):
        r = lax.broadcasted_iota(jnp.int32, (n, n), 0)
        c = lax.broadcasted_iota(jnp.int32, (n, n), 1)
        return (r > c).astype(BF16)

    def sweep(blocks, st):
        pairs = [(blk, hh) for blk in blocks for hh in range(heads)]
        scores = []
        for blk, hh in pairs:
            lanes = slice(hh * dh, (hh + 1) * dh)
            if "k_rows" in blk:
                z = _dot_nt(q_all[:, lanes], blk["k_rows"][:, lanes])
            else:
                z = _dot(q_all[:, lanes], blk["kt"][lanes, :])
            if blk["visible"] is not None:
                z = jnp.where(blk["visible"], z, SB_MASKED_SCORE)
            nlk = jnp.maximum(z, 0.0) + jnp.log(1.0 + jnp.exp2(jnp.abs(z) * (-LOG2E)))
            scores.append((z - nlk, nlk.astype(BF16), jnp.sum(nlk, axis=1, keepdims=True)))
        csums = [_dot(nlk_b, blk["u"]) for (blk, _), (_, nlk_b, _) in zip(pairs, scores)]
        st = list(st)
        for (blk, hh), (log_beta, _, row_sum), csum in zip(pairs, scores, csums):
            lanes = slice(hh * dh, (hh + 1) * dh)
            carry, acc = st[2 * hh], st[2 * hh + 1]
            if carry is not None and blk.get("carry_bias") is not None:
                carry = carry + blk["carry_bias"]
            log_a = log_beta - csum if carry is None else log_beta - csum - carry
            pv = _dot(jnp.exp(log_a).astype(BF16), blk["v"][:, lanes])
            st[2 * hh + 1] = pv if acc is None else acc + pv
            st[2 * hh] = row_sum if carry is None else carry + row_sum
        return st

    q_all = q_ref[0]
    state = [None] * (2 * heads)

    u_diag = later_key_matrix(td)
    u_past = u_diag if tk == td else later_key_matrix(tk)
    rows_i = lax.broadcasted_iota(jnp.int32, (tq, td), 0)
    cols_i = lax.broadcasted_iota(jnp.int32, (tq, td), 1)
    n_past = qi * (tq // tk) if causal_past else n_past_static

    def past_block(j, carry_bias=None):
        row0 = pl.multiple_of(j * tk, tk)
        blk = dict(v=vp_ref[0, pl.ds(row0, tk), :].astype(BF16), visible=None, u=u_past,
                   carry_bias=carry_bias)
        if past_keys_on_rows:
            blk["k_rows"] = ktp_ref[0, pl.ds(row0, tk), :].astype(BF16)
        else:
            blk["kt"] = ktp_ref[0, 0, j]
        return blk

    first = [dict(kt=ktd_ref[0, 0, sd], v=vd_ref[0, sd * td:(sd + 1) * td, :],
                  visible=cols_i + sd * td < rows_i, u=u_diag) for sd in reversed(range(nd))]
    no_past_bias = jnp.where(n_past > 0, 0.0, SB_NO_PAST_CARRY) if causal_past else None
    first.append(past_block(jnp.maximum(n_past - 1, 0), no_past_bias))
    state = sweep(first, state)

    def min_carry(st):
        m = st[0]
        for hh in range(1, heads):
            m = jnp.minimum(m, st[2 * hh])
        return jnp.min(m)

    def cond(loop):
        i, smallest, _ = loop
        return jnp.logical_and(i < n_past, smallest < SB_ZERO_WEIGHT_LOG)

    def body(loop):
        i, _, st = loop
        st = sweep([past_block(n_past - 1 - i)], st)
        return i + 1, min_carry(st), tuple(st)

    _, smallest, st = lax.while_loop(cond, body, (jnp.int32(1), min_carry(state), tuple(state)))
    o_ref[0] = jnp.concatenate([st[2 * hh + 1] for hh in range(heads)], axis=1).astype(BF16)
    carry_ref[...] = jnp.full(carry_ref.shape, smallest, F32)


def _sb_attention(q, kt_diag, v_diag, k_past, v_past, *, tq, td, tk, causal_past):
    b, l, w = q.shape
    n_g, hw = kt_diag.shape[1], kt_diag.shape[3]
    p = v_past.shape[1]
    n_past_blocks = p // tk
    keys_on_rows = k_past.ndim == 3
    assert tq % td == 0 and (not causal_past or tq % tk == 0)
    dh = w // SB_HEADS
    kern = functools.partial(_sb_kernel, tq=tq, td=td, nd=tq // td, tk=tk, heads=hw // dh, dh=dh,
                             causal_past=causal_past, n_past_static=n_past_blocks,
                             past_keys_on_rows=keys_on_rows)
    past_rows = pl.BlockSpec((1, p, hw), lambda bi, hp, qi: (bi, 0, hp), pipeline_mode=pl.Buffered(1))
    return pl.pallas_call(
        kern,
        grid=(b, n_g, l // tq),
        in_specs=[
            pl.BlockSpec((1, tq, hw), lambda bi, hp, qi: (bi, qi, hp)),
            pl.BlockSpec((1, 1, tq // td, hw, td), lambda bi, hp, qi: (bi, hp, qi, 0, 0)),
            pl.BlockSpec((1, tq, hw), lambda bi, hp, qi: (bi, qi, hp)),
            past_rows if keys_on_rows else
            pl.BlockSpec((1, 1, n_past_blocks, hw, tk), lambda bi, hp, qi: (bi, hp, 0, 0, 0),
                         pipeline_mode=pl.Buffered(1)),
            past_rows,
        ],
        out_specs=(pl.BlockSpec((1, tq, hw), lambda bi, hp, qi: (bi, qi, hp)),
                   pl.BlockSpec((1, 1, 1, 8, LANES), lambda bi, hp, qi: (bi, hp, qi, 0, 0))),
        out_shape=(jax.ShapeDtypeStruct((b, l, w), BF16),
                   jax.ShapeDtypeStruct((b, n_g, l // tq, 8, LANES), F32)),
        compiler_params=_params("parallel", "parallel", "arbitrary"),
        name="stick_breaking_attention",
    )(q, kt_diag, v_diag, k_past, v_past)


def _gdn_kernel(u_ref, z_ref, ba_ref, bat_ref, hist_ref, s0_ref, cw_ref,
                alog_l_ref, dt_l_ref, alog_s_ref, dt_s_ref, hg_ref,
                o_ref, s_ref, cnew_ref, ext_ref, *, rows, chunk, width, n_taps):
    step = pl.program_id(1)
    n_steps = pl.num_programs(1)
    dk = width // GDN_HEADS
    n_sub = rows // chunk
    pad = 8
    n_hist = n_taps - 1

    @pl.when(step == 0)
    def _():
        ext_ref[0:pad, :] = jnp.zeros((pad, ext_ref.shape[1]), F32)
        ext_ref[pad - n_hist:pad, :] = hist_ref[0]
        s_ref[...] = s0_ref[...]

    ext_ref[pad:pad + rows, :] = u_ref[0]
    y = ext_ref[pad:pad + rows, :] * cw_ref[n_hist:n_taps, :]
    for i in reversed(range(n_hist)):
        y = y + ext_ref[pad - n_hist + i:pad - n_hist + i + rows, :] * cw_ref[i:i + 1, :]
    new_tail = ext_ref[rows:rows + pad, :]
    ext_ref[0:pad, :] = new_tail

    @pl.when(step == n_steps - 1)
    def _():
        cnew_ref[0] = new_tail[pad - n_hist:pad, :]

    qkv = y * jax.nn.sigmoid(y)

    heads = range(GDN_HEADS)
    qs, ks, vs = [], [], []
    for h in heads:
        q = qkv[:, h * dk:(h + 1) * dk]
        k = qkv[:, width + h * dk:width + (h + 1) * dk]
        qs.append(q * lax.rsqrt(jnp.sum(q * q, axis=-1, keepdims=True) + NORM_EPS) * float(dk ** -0.5))
        ks.append(k * lax.rsqrt(jnp.sum(k * k, axis=-1, keepdims=True) + NORM_EPS))
        vs.append(qkv[:, 2 * width + h * dk:2 * width + (h + 1) * dk])

    r_i = lax.broadcasted_iota(jnp.int32, (chunk, chunk), 0)
    c_i = lax.broadcasted_iota(jnp.int32, (chunk, chunk), 1)
    incl = r_i >= c_i
    strict = r_i > c_i
    eye = (r_i == c_i).astype(F32)
    rr = lax.broadcasted_iota(jnp.int32, (rows, rows), 0)
    cc = lax.broadcasted_iota(jnp.int32, (rows, rows), 1)
    same_chunk = (rr // chunk) == (cc // chunk)
    lower_incl = (same_chunk & (rr >= cc)).astype(BF16)
    upper_incl = (same_chunk & (rr <= cc)).astype(BF16)

    def softplus(t):
        return jnp.maximum(t, 0.0) + jnp.log(1.0 + jnp.exp(-jnp.abs(t)))

    ba = ba_ref[0]
    g_cols = -jnp.exp(alog_l_ref[...]) * softplus(ba + dt_l_ref[...])
    gcum_cols = _dot_exactlhs_f32(lower_incl, g_cols)
    beta_cols = jax.nn.sigmoid(ba)
    bat = bat_ref[0]
    g_rows = -jnp.exp(alog_s_ref[...]) * softplus(bat + dt_s_ref[...])
    gcum_rows = _dot_f32_exactrhs(g_rows, upper_incl)

    pairs = [(ci, h) for ci in range(n_sub) for h in heads]

    pre = {}
    for ci, h in pairs:
        rs = slice(ci * chunk, (ci + 1) * chunk)
        gc_col = gcum_cols[rs, GDN_HEADS + h:GDN_HEADS + h + 1]
        gc_row = gcum_rows[GDN_HEADS + h:GDN_HEADS + h + 1, rs]
        gc_last = gc_row[:, chunk - 1:chunk]
        beta = beta_cols[rs, h:h + 1]
        gamma = jnp.where(incl, jnp.exp(gc_col - gc_row), 0.0)
        decay_in = jnp.exp(gc_col)
        q, k, v = qs[h][rs], ks[h][rs], vs[h][rs]
        kb = k * beta
        kq = _dot_nt(jnp.concatenate([kb, q], axis=0).astype(BF16), k.astype(BF16))
        pre[ci, h] = dict(
            n=jnp.where(strict, -(kq[:chunk] * gamma), 0.0),
            qk=(kq[chunk:] * gamma).astype(BF16),
            rhs=jnp.concatenate([v * beta, kb * decay_in], axis=1).astype(BF16),
            q_dec=(q * decay_in).astype(BF16),
            k_end=(k * jnp.exp(gc_last - gc_col)).astype(BF16),
            chunk_decay=jnp.exp(gc_last))

    n_rounds = max(1, (chunk - 1).bit_length())
    m_pow = {p: pre[p]["n"] for p in pairs}
    t_inv = {p: eye + pre[p]["n"] for p in pairs}
    for i in range(n_rounds):
        last = i == n_rounds - 1
        for p in pairs:
            m_bf = m_pow[p].astype(BF16)
            if i == 0:
                m_pow[p] = _dot(m_bf, m_bf)
            elif last:
                t_inv[p] = t_inv[p] + _dot(t_inv[p].astype(BF16), m_bf)
            else:
                both = _dot(jnp.concatenate([m_bf, t_inv[p].astype(BF16)], axis=0), m_bf)
                m_pow[p] = both[:chunk]
                t_inv[p] = t_inv[p] + both[chunk:]

    sol = {p: _dot(t_inv[p].astype(BF16), pre[p]["rhs"]) for p in pairs}

    state = [s_ref[0, h] for h in heads]
    o_rows = [[None] * n_sub for _ in heads]
    for ci in range(n_sub):
        ws = [_dot(jnp.concatenate([sol[ci, h][:, dk:].astype(BF16), pre[ci, h]["q_dec"]], axis=0),
                   state[h].astype(BF16)) for h in heads]
        v_new = [(sol[ci, h][:, :dk] - ws[h][:chunk]).astype(BF16) for h in heads]
        for h in heads:
            o_rows[h][ci] = ws[h][chunk:] + _dot(pre[ci, h]["qk"], v_new[h])
        state = [state[h] * pre[ci, h]["chunk_decay"] + _dot_tn(pre[ci, h]["k_end"], v_new[h])
                 for h in heads]
    for h in heads:
        s_ref[0, h] = state[h]

    z = z_ref[0]
    outs = []
    for h in heads:
        o_h = o_rows[h][0] if n_sub == 1 else jnp.concatenate(o_rows[h], axis=0)
        z_h = z[:, h * dk:(h + 1) * dk]
        outs.append(_rms(o_h, hg_ref[...]) * (z_h * jax.nn.sigmoid(z_h)))
    o_ref[0] = jnp.concatenate(outs, axis=1).astype(BF16)


def _gdn(u, z, ba, bat, hist, s0, conv_w, alog_l, dt_l, alog_s, dt_s, head_gain, *, chunk):
    b, l, conv_dim = u.shape
    width = conv_dim // 3
    dk = width // GDN_HEADS
    n_taps = conv_w.shape[0]
    rows = min(GDN_ROWS, l)
    kern = functools.partial(_gdn_kernel, rows=rows, chunk=chunk, width=width, n_taps=n_taps)

    def row_block(wd):
        return pl.BlockSpec((1, rows, wd), lambda bi, si: (bi, si, 0))

    state_spec = pl.BlockSpec((1, GDN_HEADS, dk, dk), lambda bi, si: (bi, 0, 0, 0))
    hist_spec = pl.BlockSpec((1, n_taps - 1, conv_dim), lambda bi, si: (bi, 0, 0))
    return pl.pallas_call(
        kern,
        grid=(b, l // rows),
        in_specs=[row_block(conv_dim), row_block(width), row_block(LANES),
                  pl.BlockSpec((1, 8, rows), lambda bi, si: (bi, 0, si)),
                  hist_spec, state_spec, _const_spec(conv_w.shape),
                  _const_spec((1, LANES)), _const_spec((1, LANES)),
                  _const_spec((8, 1)), _const_spec((8, 1)), _const_spec((1, dk))],
        out_specs=(row_block(width), state_spec, hist_spec),
        out_shape=(jax.ShapeDtypeStruct((b, l, width), BF16),
                   jax.ShapeDtypeStruct(s0.shape, F32),
                   jax.ShapeDtypeStruct(hist.shape, F32)),
        scratch_shapes=[pltpu.VMEM((rows + 8, conv_dim), F32)],
        compiler_params=_params("parallel", "arbitrary"),
        name="gated_delta_rule",
    )(u, z, ba, bat, hist, s0, conv_w, alog_l, dt_l, alog_s, dt_s, head_gain)


def _merge_kernel(x_ref, osb_ref, ogdn_ref, gin_ref, gout_ref, wgate_ref, wsb_ref, wgdn_ref, wout_ref, o_ref):
    x = x_ref[...]
    d = x.shape[1]
    h = _rms(x, gin_ref[...]).astype(BF16)
    gate_sb = jax.nn.sigmoid(_dot(h, wgate_ref[:, :d]))
    merged = gate_sb * _dot(osb_ref[...], wsb_ref[...])
    gate_gdn = jax.nn.sigmoid(_dot(h, wgate_ref[:, d:]))
    merged = merged + gate_gdn * _dot(ogdn_ref[...], wgdn_ref[...])
    m = _dot(merged.astype(BF16), wout_ref[...])
    o_ref[...] = x + _rms(m, gout_ref[...])


def _merge(x2, osb2, ogdn2, g_in, g_out, wgate, wsb, wgdn, wout):
    n, d = x2.shape
    tm = min(ROW_TILE, n)

    def rows(wd):
        return pl.BlockSpec((tm, wd), lambda i: (i, 0))

    return pl.pallas_call(
        _merge_kernel,
        grid=(n // tm,),
        in_specs=[rows(d), rows(osb2.shape[1]), rows(ogdn2.shape[1]),
                  _const_spec((1, d)), _const_spec((1, d)),
                  _const_spec(wgate.shape), _const_spec(wsb.shape), _const_spec(wgdn.shape),
                  _const_spec(wout.shape)],
        out_specs=rows(d),
        out_shape=jax.ShapeDtypeStruct((n, d), F32),
        compiler_params=_params("parallel"),
        name="branch_merge",
    )(x2, osb2, ogdn2, g_in, g_out, wgate, wsb, wgdn, wout)


def _layer_weights(l, norm_gains, w_in_bf, conv_w, gdn_a_log, gdn_dt_bias, gdn_norm_gain,
                   w_branch_sb, w_branch_gdn, w_out):
    d = w_in_bf.shape[1]
    sb_w = w_branch_sb.shape[1]
    gdn_w = w_branch_gdn.shape[1]
    conv_dim = conv_w.shape[2]
    n_a = 3 * sb_w + conv_dim + gdn_w
    wi = w_in_bf[l]
    wba = jnp.zeros((d, LANES), BF16).at[:, :2 * GDN_HEADS].set(wi[:, n_a:n_a + 2 * GDN_HEADS])

    def lane_vec(p):
        return jnp.zeros((1, LANES), F32).at[0, GDN_HEADS:2 * GDN_HEADS].set(p)

    def sublane_vec(p):
        return jnp.zeros((8, 1), F32).at[GDN_HEADS:2 * GDN_HEADS, 0].set(p)

    return dict(
        gains=[norm_gains[l, i][None, :] for i in range(6)],
        wa=wi[:, :n_a],
        wba=wba,
        wgate=wi[:, n_a + 2 * GDN_HEADS:],
        wsb=w_branch_sb[l].astype(BF16), wgdn=w_branch_gdn[l].astype(BF16), wout=w_out[l].astype(BF16),
        conv_w=conv_w[l],
        alog_l=lane_vec(gdn_a_log[l]), dt_l=lane_vec(gdn_dt_bias[l]),
        alog_s=sublane_vec(gdn_a_log[l]), dt_s=sublane_vec(gdn_dt_bias[l]),
        head_gain=gdn_norm_gain[l][None, :],
        dims=(sb_w, conv_dim, gdn_w),
    )


def _run_group(x, weights, ffn_w, past_k, past_v, conv_hist, s0, chunk):
    b, l, d = x.shape
    n = b * l
    s_list, c_list = [], []
    kv_stacks = ()
    for li, w in enumerate(weights):
        sb_w, conv_dim, gdn_w = w["dims"]
        dh = sb_w // SB_HEADS
        tq = min(SB_QUERY_TILE, l)
        td = min(SB_KEY_TILE, l)
        g = w["gains"]
        x2 = _ffn(x.reshape(n, d), g[0], g[1], ffn_w[0], ffn_w[1], li)
        q, k_stack, v_stack, kt, vb, u, z, ba, bat = _inproj(
            x2.reshape(b, l, d), g[2], w["wa"], w["wba"], kv_stacks,
            layer=li, depth=len(weights), sb_w=sb_w, conv_dim=conv_dim, gdn_w=gdn_w, sb_tile=td)
        kv_stacks = (k_stack, v_stack)
        if past_k is None:
            o_sb, _ = _sb_attention(q, kt, vb, kt, vb, tq=tq, td=td, tk=td, causal_past=True)
        else:
            p = past_k.shape[2]
            tk = min(SB_KEY_TILE, p)

            def attend(pk, pv):
                keys = pk.shape[1]
                return _sb_attention(q, kt, vb, pk.reshape(b, keys, sb_w), pv.reshape(b, keys, sb_w),
                                     tq=tq, td=td, tk=tk, causal_past=False)

            newest = lax.optimization_barrier((past_k[li][:, p - tk:], past_v[li][:, p - tk:]))
            o_sb, carry_min = attend(*(t.astype(BF16) for t in newest))
            if p > tk:
                o_sb = lax.cond(jnp.min(carry_min) >= SB_ZERO_WEIGHT_LOG, lambda: o_sb,
                                lambda: attend(past_k[li], past_v[li])[0])
        o_gdn, s_fin, conv_new = _gdn(u, z, ba, bat, conv_hist[li], s0[li], w["conv_w"],
                                      w["alog_l"], w["dt_l"], w["alog_s"], w["dt_s"], w["head_gain"],
                                      chunk=chunk)
        x2 = _merge(x2, o_sb.reshape(n, sb_w), o_gdn.reshape(n, gdn_w), g[2], g[3],
                    w["wgate"], w["wsb"], w["wgdn"], w["wout"])
        x2 = _ffn(x2, g[4], g[5], ffn_w[2], ffn_w[3], li)
        x = x2.reshape(b, l, d)
        s_list.append(s_fin)
        c_list.append(conv_new)
    k_all, v_all = (t.reshape(len(weights), b, l, SB_HEADS, dh) for t in kv_stacks)
    return x, k_all, v_all, jnp.stack(s_list), jnp.stack(c_list)


def kernel(x_prompt, x_sample, cache_sb_k, cache_sb_v, state_gdn, state_conv, norm_gains,
           w_ffn1_up, w_ffn1_down, w_in, conv_w, gdn_a_log, gdn_dt_bias, gdn_norm_gain,
           w_branch_sb, w_branch_gdn, w_out, w_ffn2_up, w_ffn2_down):
    depth = w_in.shape[0]
    ffn_w = tuple(_to_bf16(w) for w in (w_ffn1_up, w_ffn1_down, w_ffn2_up, w_ffn2_down))
    w_in_bf = w_in.astype(BF16)
    weights = [_layer_weights(l, norm_gains, w_in_bf, conv_w, gdn_a_log, gdn_dt_bias, gdn_norm_gain,
                              w_branch_sb, w_branch_gdn, w_out) for l in range(depth)]
    bp = x_prompt.shape[0]
    chunk_prompt = 64
    zero_conv = jnp.zeros((depth, bp) + state_conv.shape[2:], state_conv.dtype)
    zero_state = jnp.zeros((depth, bp) + state_gdn.shape[2:], state_gdn.dtype)
    y_p, pk, pv, ps, pc = _run_group(x_prompt, weights, ffn_w, None, None, zero_conv, zero_state, chunk_prompt)
    y_s, sk, sv, ss, sc = _run_group(x_sample, weights, ffn_w, cache_sb_k, cache_sb_v, state_conv, state_gdn,
                                     x_sample.shape[1])
    return (y_p, y_s, pk, pv, ps, pc, sk, sv, ss, sc)
```

```python
import functools

import jax
import jax.numpy as jnp
from jax import lax
from jax.experimental import pallas as pl
from jax.experimental.pallas import tpu as pltpu

F32 = jnp.float32
BF16 = jnp.bfloat16

NORM_EPS = 1e-6
LOG2E = 1.4426950408889634
SB_HEADS = 8
GDN_HEADS = 4
LANES = 128
VMEM_LIMIT_BYTES = 56 * 1024 * 1024

ROW_TILE = 512
CAST_ROWS = 256
FF_CHUNK = 256
GDN_ROWS = 512
SB_KEY_TILE = 256
SB_QUERY_TILE = 256
SB_GROUP_LANES = 512
SB_ZERO_WEIGHT_LOG = 110.0
SB_MASKED_SCORE = -1e30
SB_NO_PAST_CARRY = 1e30


def _params(*sem):
    return pltpu.CompilerParams(dimension_semantics=sem, vmem_limit_bytes=VMEM_LIMIT_BYTES)


def _const_spec(shape):
    zeros = (0,) * len(shape)
    return pl.BlockSpec(shape, lambda *_: zeros)


def _rms(x, gain):
    ms = jnp.mean(x * x, axis=-1, keepdims=True)
    return x * lax.rsqrt(ms + NORM_EPS) * gain


def _dot(a, b):
    return jnp.dot(a, b, preferred_element_type=F32)


def _dot_nt(a, b):
    return lax.dot_general(a, b, (((1,), (1,)), ((), ())), preferred_element_type=F32)


def _dot_tn(a, b):
    return lax.dot_general(a, b, (((0,), (0,)), ((), ())), preferred_element_type=F32)


def _split3(x):
    hi = x.astype(BF16)
    r = x - hi.astype(F32)
    mid = r.astype(BF16)
    lo = (r - mid.astype(F32)).astype(BF16)
    return hi, mid, lo


def _dot_f32_exactrhs(a, b_bf16):
    hi, mid, lo = _split3(a)
    return _dot(hi, b_bf16) + _dot(mid, b_bf16) + _dot(lo, b_bf16)


def _dot_exactlhs_f32(a_bf16, b):
    hi, mid, lo = _split3(b)
    return _dot(a_bf16, hi) + _dot(a_bf16, mid) + _dot(a_bf16, lo)


def _ffn_kernel(x_ref, gin_ref, gout_ref, wup_ref, wd_ref, o_ref, act_ref):
    x = x_ref[...]
    h = _rms(x, gin_ref[...]).astype(BF16)
    d_ff = wd_ref.shape[1]
    for c in range(d_ff // FF_CHUNK):
        sl = slice(c * FF_CHUNK, (c + 1) * FF_CHUNK)
        g = _dot(h, wup_ref[0, :, sl])
        u = _dot(h, wup_ref[0, :, d_ff + c * FF_CHUNK:d_ff + (c + 1) * FF_CHUNK])
        act_ref[:, sl] = (g * jax.nn.sigmoid(g) * u).astype(BF16)
    y = _dot(act_ref[...], wd_ref[0])
    o_ref[...] = x + 0.5 * _rms(y, gout_ref[...])


def _ffn(x2, g_in, g_out, w_up, w_down, layer):
    n, d = x2.shape
    d_ff = w_down.shape[1]
    tm = min(ROW_TILE, n)
    row = pl.BlockSpec((tm, d), lambda i: (i, 0))
    return pl.pallas_call(
        _ffn_kernel,
        grid=(n // tm,),
        in_specs=[row, _const_spec((1, d)), _const_spec((1, d)),
                  pl.BlockSpec((1, d, 2 * d_ff), lambda i: (layer, 0, 0)),
                  pl.BlockSpec((1, d_ff, d), lambda i: (layer, 0, 0))],
        out_specs=row,
        out_shape=jax.ShapeDtypeStruct((n, d), F32),
        scratch_shapes=[pltpu.VMEM((tm, d_ff), BF16)],
        compiler_params=_params("parallel"),
        name="ffn_half_step",
    )(x2, g_in, g_out, w_up, w_down)


def _cast_kernel(w_ref, o_ref):
    o_ref[...] = w_ref[...].astype(o_ref.dtype)


def _to_bf16(w):
    depth, rows, cols = w.shape
    tr = min(rows, CAST_ROWS)
    assert rows % tr == 0
    spec = pl.BlockSpec((1, tr, cols), lambda li, ri: (li, ri, 0))
    return pl.pallas_call(
        _cast_kernel,
        grid=(depth, rows // tr),
        in_specs=[spec],
        out_specs=spec,
        out_shape=jax.ShapeDtypeStruct(w.shape, BF16),
        compiler_params=_params("parallel", "parallel"),
        name="weights_to_bf16",
    )(w)


def _inproj_kernel(x_ref, g_ref, wa_ref, wba_ref, *refs, sb_w, conv_dim, gdn_w, sb_tile, q_scale):
    q_ref, k_ref, v_ref, kt_ref, vb_ref, u_ref, z_ref, ba_ref, bat_ref = refs[-9:]
    h = _rms(x_ref[0], g_ref[...]).astype(BF16)
    tm = h.shape[0]

    def proj(lo, width):
        return _dot(h, wa_ref[:, lo:lo + width])

    q_ref[0] = (proj(0, sb_w) * q_scale).astype(BF16)
    k = proj(sb_w, sb_w)
    k_ref[0, 0] = k
    for later in range(1, k_ref.shape[0]):
        k_ref[later, 0] = jnp.zeros_like(k)
        v_ref[later, 0] = jnp.zeros_like(k)
    kt = k.T.astype(BF16)
    for hp in range(sb_w // SB_GROUP_LANES):
        for s in range(tm // sb_tile):
            kt_ref[0, hp, s] = kt[hp * SB_GROUP_LANES:(hp + 1) * SB_GROUP_LANES,
                                  s * sb_tile:(s + 1) * sb_tile]
    v = proj(2 * sb_w, sb_w)
    v_ref[0, 0] = v
    vb_ref[0] = v.astype(BF16)
    u_ref[0] = proj(3 * sb_w, conv_dim)
    z_ref[0] = proj(3 * sb_w + conv_dim, gdn_w)
    ba = _dot(h, wba_ref[...])
    ba_ref[0] = ba
    bat_ref[0] = ba.T[:8]


def _inproj(x, gain, wa, wba, kv_stacks, *, layer, depth, sb_w, conv_dim, gdn_w, sb_tile):
    b, l, d = x.shape
    tm = min(ROW_TILE, l)
    n_hp = sb_w // SB_GROUP_LANES
    kern = functools.partial(_inproj_kernel, sb_w=sb_w, conv_dim=conv_dim, gdn_w=gdn_w,
                             sb_tile=sb_tile,
                             q_scale=float((sb_w // SB_HEADS) ** -0.5))

    def rows(width):
        return pl.BlockSpec((1, tm, width), lambda bi, ti: (bi, ti, 0))

    out_shape = (
        jax.ShapeDtypeStruct((b, l, sb_w), BF16),
        jax.ShapeDtypeStruct((depth, b, l, sb_w), F32),
        jax.ShapeDtypeStruct((depth, b, l, sb_w), F32),
        jax.ShapeDtypeStruct((b, n_hp, l // sb_tile, SB_GROUP_LANES, sb_tile), BF16),
        jax.ShapeDtypeStruct((b, l, sb_w), BF16),
        jax.ShapeDtypeStruct((b, l, conv_dim), F32),
        jax.ShapeDtypeStruct((b, l, gdn_w), F32),
        jax.ShapeDtypeStruct((b, l, LANES), F32),
        jax.ShapeDtypeStruct((b, 8, l), F32),
    )
    if kv_stacks:
        layer_rows = pl.BlockSpec((1, 1, tm, sb_w), lambda bi, ti: (layer, bi, ti, 0))
    else:
        assert layer == 0
        layer_rows = pl.BlockSpec((depth, 1, tm, sb_w), lambda bi, ti: (0, bi, ti, 0))
    out_specs = (
        rows(sb_w), layer_rows, layer_rows,
        pl.BlockSpec((1, n_hp, tm // sb_tile, SB_GROUP_LANES, sb_tile), lambda bi, ti: (bi, 0, ti, 0, 0)),
        rows(sb_w), rows(conv_dim), rows(gdn_w), rows(LANES),
        pl.BlockSpec((1, 8, tm), lambda bi, ti: (bi, 0, ti)),
    )
    return pl.pallas_call(
        kern,
        grid=(b, l // tm),
        in_specs=[rows(d), _const_spec((1, d)), _const_spec(wa.shape), _const_spec(wba.shape)]
        + [pl.BlockSpec(memory_space=pl.ANY)] * len(kv_stacks),
        out_specs=out_specs,
        out_shape=out_shape,
        input_output_aliases={4 + i: 1 + i for i in range(len(kv_stacks))},
        compiler_params=_params("parallel", "parallel"),
        name="in_projection",
    )(x, gain, wa, wba, *kv_stacks)


def _sb_kernel(q_ref, ktd_ref, vd_ref, ktp_ref, vp_ref, o_ref, carry_ref, *, tq, td, nd, tk, heads, dh,
               causal_past, n_past_static, past_keys_on_lanes):
    qi = pl.program_id(2)

    def later_key_matrix(n):
        r = lax.broadcasted_iota(jnp.int32, (n, n), 0)
        c = lax.broadcasted_iota(jnp.int32, (n, n), 1)
        return (r > c).astype(BF16)

    def sweep(blocks, st):
        pairs = [(blk, hh) for blk in blocks for hh in range(heads)]
        scores = []
        for blk, hh in pairs:
            lanes = slice(hh * dh, (hh + 1) * dh)
            z = _dot(q_all[:, lanes], blk["kt"][lanes, :])
            if blk["visible"] is not None:
                z = jnp.where(blk["visible"], z, SB_MASKED_SCORE)
            nlk = jnp.maximum(z, 0.0) + jnp.log(1.0 + jnp.exp2(jnp.abs(z) * (-LOG2E)))
            scores.append((z - nlk, nlk.astype(BF16), jnp.sum(nlk, axis=1, keepdims=True)))
        csums = [_dot(nlk_b, blk["u"]) for (blk, _), (_, nlk_b, _) in zip(pairs, scores)]
        st = list(st)
        for (blk, hh), (log_beta, _, row_sum), csum in zip(pairs, scores, csums):
            lanes = slice(hh * dh, (hh + 1) * dh)
            carry, acc = st[2 * hh], st[2 * hh + 1]
            if carry is not None and blk.get("carry_bias") is not None:
                carry = carry + blk["carry_bias"]
            log_a = log_beta - csum if carry is None else log_beta - csum - carry
            a = jnp.exp(log_a).astype(BF16)
            pv = _dot_nt(a, blk["vt"][lanes, :]) if "vt" in blk else _dot(a, blk["v"][:, lanes])
            st[2 * hh + 1] = pv if acc is None else acc + pv
            st[2 * hh] = row_sum if carry is None else carry + row_sum
        return st

    q_all = q_ref[0]
    state = [None] * (2 * heads)

    u_diag = later_key_matrix(td)
    u_past = u_diag if tk == td else later_key_matrix(tk)
    rows_i = lax.broadcasted_iota(jnp.int32, (tq, td), 0)
    cols_i = lax.broadcasted_iota(jnp.int32, (tq, td), 1)
    n_past = qi * (tq // tk) if causal_past else n_past_static

    def past_block(j, carry_bias=None):
        row0 = pl.multiple_of(j * tk, tk)
        blk = dict(visible=None, u=u_past, carry_bias=carry_bias)
        if past_keys_on_lanes:
            blk["kt"] = ktp_ref[0, :, pl.ds(row0, tk)].astype(BF16)
            blk["vt"] = vp_ref[0, :, pl.ds(row0, tk)].astype(BF16)
        else:
            blk["kt"] = ktp_ref[0, 0, j]
            blk["v"] = vp_ref[0, pl.ds(row0, tk), :]
        return blk

    first = [dict(kt=ktd_ref[0, 0, sd], v=vd_ref[0, sd * td:(sd + 1) * td, :],
                  visible=cols_i + sd * td < rows_i, u=u_diag) for sd in reversed(range(nd))]
    no_past_bias = jnp.where(n_past > 0, 0.0, SB_NO_PAST_CARRY) if causal_past else None
    first.append(past_block(jnp.maximum(n_past - 1, 0), no_past_bias))
    state = sweep(first, state)

    def min_carry(st):
        m = st[0]
        for hh in range(1, heads):
            m = jnp.minimum(m, st[2 * hh])
        return jnp.min(m)

    def cond(loop):
        i, smallest, _ = loop
        return jnp.logical_and(i < n_past, smallest < SB_ZERO_WEIGHT_LOG)

    def body(loop):
        i, _, st = loop
        st = sweep([past_block(n_past - 1 - i)], st)
        return i + 1, min_carry(st), tuple(st)

    _, smallest, st = lax.while_loop(cond, body, (jnp.int32(1), min_carry(state), tuple(state)))
    o_ref[0] = jnp.concatenate([st[2 * hh + 1] for hh in range(heads)], axis=1).astype(BF16)
    carry_ref[...] = jnp.full(carry_ref.shape, smallest, F32)


def _sb_attention(q, kt_diag, v_diag, k_past, v_past, *, tq, td, tk, causal_past):
    b, l, w = q.shape
    n_g, hw = kt_diag.shape[1], kt_diag.shape[3]
    keys_on_lanes = k_past.ndim == 3
    p = v_past.shape[2] if keys_on_lanes else v_past.shape[1]
    n_past_blocks = p // tk
    assert tq % td == 0 and (not causal_past or tq % tk == 0)
    dh = w // SB_HEADS
    kern = functools.partial(_sb_kernel, tq=tq, td=td, nd=tq // td, tk=tk, heads=hw // dh, dh=dh,
                             causal_past=causal_past, n_past_static=n_past_blocks,
                             past_keys_on_lanes=keys_on_lanes)
    if keys_on_lanes:
        past_specs = [pl.BlockSpec((1, hw, p), lambda bi, hp, qi: (bi, hp, 0),
                                   pipeline_mode=pl.Buffered(1))] * 2
    else:
        past_specs = [pl.BlockSpec((1, 1, n_past_blocks, hw, tk), lambda bi, hp, qi: (bi, hp, 0, 0, 0),
                                   pipeline_mode=pl.Buffered(1)),
                      pl.BlockSpec((1, p, hw), lambda bi, hp, qi: (bi, 0, hp),
                                   pipeline_mode=pl.Buffered(1))]
    return pl.pallas_call(
        kern,
        grid=(b, n_g, l // tq),
        in_specs=[
            pl.BlockSpec((1, tq, hw), lambda bi, hp, qi: (bi, qi, hp)),
            pl.BlockSpec((1, 1, tq // td, hw, td), lambda bi, hp, qi: (bi, hp, qi, 0, 0)),
            pl.BlockSpec((1, tq, hw), lambda bi, hp, qi: (bi, qi, hp)),
        ] + past_specs,
        out_specs=(pl.BlockSpec((1, tq, hw), lambda bi, hp, qi: (bi, qi, hp)),
                   pl.BlockSpec((1, 1, 1, 8, LANES), lambda bi, hp, qi: (bi, hp, qi, 0, 0))),
        out_shape=(jax.ShapeDtypeStruct((b, l, w), BF16),
                   jax.ShapeDtypeStruct((b, n_g, l // tq, 8, LANES), F32)),
        compiler_params=_params("parallel", "parallel", "arbitrary"),
        name="stick_breaking_attention",
    )(q, kt_diag, v_diag, k_past, v_past)


def _gdn_kernel(u_ref, z_ref, ba_ref, bat_ref, hist_ref, s0_ref, cw_ref,
                alog_l_ref, dt_l_ref, alog_s_ref, dt_s_ref, hg_ref,
                o_ref, s_ref, cnew_ref, ext_ref, *, rows, chunk, width, n_taps):
    step = pl.program_id(1)
    n_steps = pl.num_programs(1)
    dk = width // GDN_HEADS
    n_sub = rows // chunk
    pad = 8
    n_hist = n_taps - 1

    @pl.when(step == 0)
    def _():
        ext_ref[0:pad, :] = jnp.zeros((pad, ext_ref.shape[1]), F32)
        ext_ref[pad - n_hist:pad, :] = hist_ref[0]
        s_ref[...] = s0_ref[...]

    ext_ref[pad:pad + rows, :] = u_ref[0]
    y = ext_ref[pad:pad + rows, :] * cw_ref[n_hist:n_taps, :]
    for i in reversed(range(n_hist)):
        y = y + ext_ref[pad - n_hist + i:pad - n_hist + i + rows, :] * cw_ref[i:i + 1, :]
    new_tail = ext_ref[rows:rows + pad, :]
    ext_ref[0:pad, :] = new_tail

    @pl.when(step == n_steps - 1)
    def _():
        cnew_ref[0] = new_tail[pad - n_hist:pad, :]

    qkv = y * jax.nn.sigmoid(y)

    heads = range(GDN_HEADS)
    qs, ks, vs = [], [], []
    for h in heads:
        q = qkv[:, h * dk:(h + 1) * dk]
        k = qkv[:, width + h * dk:width + (h + 1) * dk]
        qs.append(q * lax.rsqrt(jnp.sum(q * q, axis=-1, keepdims=True) + NORM_EPS) * float(dk ** -0.5))
        ks.append(k * lax.rsqrt(jnp.sum(k * k, axis=-1, keepdims=True) + NORM_EPS))
        vs.append(qkv[:, 2 * width + h * dk:2 * width + (h + 1) * dk])

    r_i = lax.broadcasted_iota(jnp.int32, (chunk, chunk), 0)
    c_i = lax.broadcasted_iota(jnp.int32, (chunk, chunk), 1)
    incl = r_i >= c_i
    strict = r_i > c_i
    eye = (r_i == c_i).astype(F32)
    rr = lax.broadcasted_iota(jnp.int32, (rows, rows), 0)
    cc = lax.broadcasted_iota(jnp.int32, (rows, rows), 1)
    same_chunk = (rr // chunk) == (cc // chunk)
    lower_incl = (same_chunk & (rr >= cc)).astype(BF16)
    upper_incl = (same_chunk & (rr <= cc)).astype(BF16)

    def softplus(t):
        return jnp.maximum(t, 0.0) + jnp.log(1.0 + jnp.exp(-jnp.abs(t)))

    ba = ba_ref[0]
    g_cols = -jnp.exp(alog_l_ref[...]) * softplus(ba + dt_l_ref[...])
    gcum_cols = _dot_exactlhs_f32(lower_incl, g_cols)
    beta_cols = jax.nn.sigmoid(ba)
    bat = bat_ref[0]
    g_rows = -jnp.exp(alog_s_ref[...]) * softplus(bat + dt_s_ref[...])
    gcum_rows = _dot_f32_exactrhs(g_rows, upper_incl)

    pairs = [(ci, h) for ci in range(n_sub) for h in heads]

    pre = {}
    for ci, h in pairs:
        rs = slice(ci * chunk, (ci + 1) * chunk)
        gc_col = gcum_cols[rs, GDN_HEADS + h:GDN_HEADS + h + 1]
        gc_row = gcum_rows[GDN_HEADS + h:GDN_HEADS + h + 1, rs]
        gc_last = gc_row[:, chunk - 1:chunk]
        beta = beta_cols[rs, h:h + 1]
        gamma = jnp.where(incl, jnp.exp(gc_col - gc_row), 0.0)
        decay_in = jnp.exp(gc_col)
        q, k, v = qs[h][rs], ks[h][rs], vs[h][rs]
        kb = k * beta
        kq = _dot_nt(jnp.concatenate([kb, q], axis=0).astype(BF16), k.astype(BF16))
        pre[ci, h] = dict(
            n=jnp.where(strict, -(kq[:chunk] * gamma), 0.0),
            qk=(kq[chunk:] * gamma).astype(BF16),
            rhs=jnp.concatenate([v * beta, kb * decay_in], axis=1).astype(BF16),
            q_dec=(q * decay_in).astype(BF16),
            k_end=(k * jnp.exp(gc_last - gc_col)).astype(BF16),
            chunk_decay=jnp.exp(gc_last))

    n_rounds = max(1, (chunk - 1).bit_length())
    m_pow = {p: pre[p]["n"] for p in pairs}
    t_inv = {p: eye + pre[p]["n"] for p in pairs}
    for i in range(n_rounds):
        last = i == n_rounds - 1
        for p in pairs:
            m_bf = m_pow[p].astype(BF16)
            if i == 0:
                m_pow[p] = _dot(m_bf, m_bf)
            elif last:
                t_inv[p] = t_inv[p] + _dot(t_inv[p].astype(BF16), m_bf)
            else:
                both = _dot(jnp.concatenate([m_bf, t_inv[p].astype(BF16)], axis=0), m_bf)
                m_pow[p] = both[:chunk]
                t_inv[p] = t_inv[p] + both[chunk:]

    sol = {p: _dot(t_inv[p].astype(BF16), pre[p]["rhs"]) for p in pairs}

    state = [s_ref[0, h] for h in heads]
    o_rows = [[None] * n_sub for _ in heads]
    for ci in range(n_sub):
        ws = [_dot(jnp.concatenate([sol[ci, h][:, dk:].astype(BF16), pre[ci, h]["q_dec"]], axis=0),
                   state[h].astype(BF16)) for h in heads]
        v_new = [(sol[ci, h][:, :dk] - ws[h][:chunk]).astype(BF16) for h in heads]
        for h in heads:
            o_rows[h][ci] = ws[h][chunk:] + _dot(pre[ci, h]["qk"], v_new[h])
        state = [state[h] * pre[ci, h]["chunk_decay"] + _dot_tn(pre[ci, h]["k_end"], v_new[h])
                 for h in heads]
    for h in heads:
        s_ref[0, h] = state[h]

    z = z_ref[0]
    outs = []
    for h in heads:
        o_h = o_rows[h][0] if n_sub == 1 else jnp.concatenate(o_rows[h], axis=0)
        z_h = z[:, h * dk:(h + 1) * dk]
        outs.append(_rms(o_h, hg_ref[...]) * (z_h * jax.nn.sigmoid(z_h)))
    o_ref[0] = jnp.concatenate(outs, axis=1).astype(BF16)


def _gdn(u, z, ba, bat, hist, s0, conv_w, alog_l, dt_l, alog_s, dt_s, head_gain, *, chunk):
    b, l, conv_dim = u.shape
    width = conv_dim // 3
    dk = width // GDN_HEADS
    n_taps = conv_w.shape[0]
    rows = min(GDN_ROWS, l)
    kern = functools.partial(_gdn_kernel, rows=rows, chunk=chunk, width=width, n_taps=n_taps)

    def row_block(wd):
        return pl.BlockSpec((1, rows, wd), lambda bi, si: (bi, si, 0))

    state_spec = pl.BlockSpec((1, GDN_HEADS, dk, dk), lambda bi, si: (bi, 0, 0, 0))
    hist_spec = pl.BlockSpec((1, n_taps - 1, conv_dim), lambda bi, si: (bi, 0, 0))
    return pl.pallas_call(
        kern,
        grid=(b, l // rows),
        in_specs=[row_block(conv_dim), row_block(width), row_block(LANES),
                  pl.BlockSpec((1, 8, rows), lambda bi, si: (bi, 0, si)),
                  hist_spec, state_spec, _const_spec(conv_w.shape),
                  _const_spec((1, LANES)), _const_spec((1, LANES)),
                  _const_spec((8, 1)), _const_spec((8, 1)), _const_spec((1, dk))],
        out_specs=(row_block(width), state_spec, hist_spec),
        out_shape=(jax.ShapeDtypeStruct((b, l, width), BF16),
                   jax.ShapeDtypeStruct(s0.shape, F32),
                   jax.ShapeDtypeStruct(hist.shape, F32)),
        scratch_shapes=[pltpu.VMEM((rows + 8, conv_dim), F32)],
        compiler_params=_params("parallel", "arbitrary"),
        name="gated_delta_rule",
    )(u, z, ba, bat, hist, s0, conv_w, alog_l, dt_l, alog_s, dt_s, head_gain)


def _merge_kernel(x_ref, osb_ref, ogdn_ref, gin_ref, gout_ref, wgate_ref, wsb_ref, wgdn_ref, wout_ref, o_ref):
    x = x_ref[...]
    d = x.shape[1]
    h = _rms(x, gin_ref[...]).astype(BF16)
    gate_sb = jax.nn.sigmoid(_dot(h, wgate_ref[:, :d]))
    merged = gate_sb * _dot(osb_ref[...], wsb_ref[...])
    gate_gdn = jax.nn.sigmoid(_dot(h, wgate_ref[:, d:]))
    merged = merged + gate_gdn * _dot(ogdn_ref[...], wgdn_ref[...])
    m = _dot(merged.astype(BF16), wout_ref[...])
    o_ref[...] = x + _rms(m, gout_ref[...])


def _merge(x2, osb2, ogdn2, g_in, g_out, wgate, wsb, wgdn, wout):
    n, d = x2.shape
    tm = min(ROW_TILE, n)

    def rows(wd):
        return pl.BlockSpec((tm, wd), lambda i: (i, 0))

    return pl.pallas_call(
        _merge_kernel,
        grid=(n // tm,),
        in_specs=[rows(d), rows(osb2.shape[1]), rows(ogdn2.shape[1]),
                  _const_spec((1, d)), _const_spec((1, d)),
                  _const_spec(wgate.shape), _const_spec(wsb.shape), _const_spec(wgdn.shape),
                  _const_spec(wout.shape)],
        out_specs=rows(d),
        out_shape=jax.ShapeDtypeStruct((n, d), F32),
        compiler_params=_params("parallel"),
        name="branch_merge",
    )(x2, osb2, ogdn2, g_in, g_out, wgate, wsb, wgdn, wout)


def _layer_weights(l, norm_gains, w_in_bf, conv_w, gdn_a_log, gdn_dt_bias, gdn_norm_gain,
                   w_branch_sb, w_branch_gdn, w_out):
    d = w_in_bf.shape[1]
    sb_w = w_branch_sb.shape[1]
    gdn_w = w_branch_gdn.shape[1]
    conv_dim = conv_w.shape[2]
    n_a = 3 * sb_w + conv_dim + gdn_w
    wi = w_in_bf[l]
    wba = jnp.zeros((d, LANES), BF16).at[:, :2 * GDN_HEADS].set(wi[:, n_a:n_a + 2 * GDN_HEADS])

    def lane_vec(p):
        return jnp.zeros((1, LANES), F32).at[0, GDN_HEADS:2 * GDN_HEADS].set(p)

    def sublane_vec(p):
        return jnp.zeros((8, 1), F32).at[GDN_HEADS:2 * GDN_HEADS, 0].set(p)

    return dict(
        gains=[norm_gains[l, i][None, :] for i in range(6)],
        wa=wi[:, :n_a],
        wba=wba,
        wgate=wi[:, n_a + 2 * GDN_HEADS:],
        wsb=w_branch_sb[l].astype(BF16), wgdn=w_branch_gdn[l].astype(BF16), wout=w_out[l].astype(BF16),
        conv_w=conv_w[l],
        alog_l=lane_vec(gdn_a_log[l]), dt_l=lane_vec(gdn_dt_bias[l]),
        alog_s=sublane_vec(gdn_a_log[l]), dt_s=sublane_vec(gdn_dt_bias[l]),
        head_gain=gdn_norm_gain[l][None, :],
        dims=(sb_w, conv_dim, gdn_w),
    )


def _run_group(x, weights, ffn_w, past_k, past_v, conv_hist, s0, chunk):
    b, l, d = x.shape
    n = b * l
    s_list, c_list = [], []
    kv_stacks = ()
    for li, w in enumerate(weights):
        sb_w, conv_dim, gdn_w = w["dims"]
        dh = sb_w // SB_HEADS
        tq = min(SB_QUERY_TILE, l)
        td = min(SB_KEY_TILE, l)
        g = w["gains"]
        x2 = _ffn(x.reshape(n, d), g[0], g[1], ffn_w[0], ffn_w[1], li)
        q, k_stack, v_stack, kt, vb, u, z, ba, bat = _inproj(
            x2.reshape(b, l, d), g[2], w["wa"], w["wba"], kv_stacks,
            layer=li, depth=len(weights), sb_w=sb_w, conv_dim=conv_dim, gdn_w=gdn_w, sb_tile=td)
        kv_stacks = (k_stack, v_stack)
        if past_k is None:
            o_sb, _ = _sb_attention(q, kt, vb, kt, vb, tq=tq, td=td, tk=td, causal_past=True)
        else:
            p = past_k.shape[2]
            tk = min(SB_KEY_TILE, p)

            def attend(kt_p, vt_p):
                return _sb_attention(q, kt, vb, kt_p, vt_p, tq=tq, td=td, tk=tk, causal_past=False)

            kt_all = jnp.transpose(past_k[li], (0, 2, 3, 1)).reshape(b, sb_w, p)
            vt_all = jnp.transpose(past_v[li], (0, 2, 3, 1)).reshape(b, sb_w, p)
            newest = lax.optimization_barrier((kt_all[:, :, p - tk:], vt_all[:, :, p - tk:]))
            o_sb, carry_min = attend(*(t.astype(BF16) for t in newest))
            if p > tk:
                o_sb = lax.cond(jnp.min(carry_min) >= SB_ZERO_WEIGHT_LOG, lambda: o_sb,
                                lambda: attend(kt_all, vt_all)[0])
        o_gdn, s_fin, conv_new = _gdn(u, z, ba, bat, conv_hist[li], s0[li], w["conv_w"],
                                      w["alog_l"], w["dt_l"], w["alog_s"], w["dt_s"], w["head_gain"],
                                      chunk=chunk)
        x2 = _merge(x2, o_sb.reshape(n, sb_w), o_gdn.reshape(n, gdn_w), g[2], g[3],
                    w["wgate"], w["wsb"], w["wgdn"], w["wout"])
        x2 = _ffn(x2, g[4], g[5], ffn_w[2], ffn_w[3], li)
        x = x2.reshape(b, l, d)
        s_list.append(s_fin)
        c_list.append(conv_new)
    k_all, v_all = (t.reshape(len(weights), b, l, SB_HEADS, dh) for t in kv_stacks)
    return x, k_all, v_all, jnp.stack(s_list), jnp.stack(c_list)


def kernel(x_prompt, x_sample, cache_sb_k, cache_sb_v, state_gdn, state_conv, norm_gains,
           w_ffn1_up, w_ffn1_down, w_in, conv_w, gdn_a_log, gdn_dt_bias, gdn_norm_gain,
           w_branch_sb, w_branch_gdn, w_out, w_ffn2_up, w_ffn2_down):
    depth = w_in.shape[0]
    ffn_w = tuple(_to_bf16(w) for w in (w_ffn1_up, w_ffn1_down, w_ffn2_up, w_ffn2_down))
    w_in_bf = w_in.astype(BF16)
    weights = [_layer_weights(l, norm_gains, w_in_bf, conv_w, gdn_a_log, gdn_dt_bias, gdn_norm_gain,
                              w_branch_sb, w_branch_gdn, w_out) for l in range(depth)]
    bp = x_prompt.shape[0]
    chunk_prompt = 64
    zero_conv = jnp.zeros((depth, bp) + state_conv.shape[2:], state_conv.dtype)
    zero_state = jnp.zeros((depth, bp) + state_gdn.shape[2:], state_gdn.dtype)
    y_p, pk, pv, ps, pc = _run_group(x_prompt, weights, ffn_w, None, None, zero_conv, zero_state, chunk_prompt)
    y_s, sk, sv, ss, sc = _run_group(x_sample, weights, ffn_w, cache_sb_k, cache_sb_v, state_conv, state_gdn,
                                     x_sample.shape[1])
    return (y_p, y_s, pk, pv, ps, pc, sk, sv, ss, sc)
```

```python
import functools

import jax
import jax.numpy as jnp
from jax import lax
from jax.experimental import pallas as pl
from jax.experimental.pallas import tpu as pltpu

F32 = jnp.float32
BF16 = jnp.bfloat16

NORM_EPS = 1e-6
LOG2E = 1.4426950408889634
SB_HEADS = 8
GDN_HEADS = 4
LANES = 128
VMEM_LIMIT_BYTES = 56 * 1024 * 1024

ROW_TILE = 512
CAST_ROWS = 256
FF_CHUNK = 256
GDN_ROWS = 512
SB_KEY_TILE = 256
SB_QUERY_TILE = 256
SB_GROUP_LANES = 512
SB_ZERO_WEIGHT_LOG = 110.0
SB_MASKED_SCORE = -1e30
SB_NO_PAST_CARRY = 1e30


def _params(*sem):
    return pltpu.CompilerParams(dimension_semantics=sem, vmem_limit_bytes=VMEM_LIMIT_BYTES)


def _const_spec(shape):
    zeros = (0,) * len(shape)
    return pl.BlockSpec(shape, lambda *_: zeros)


def _rms(x, gain):
    ms = jnp.mean(x * x, axis=-1, keepdims=True)
    return x * lax.rsqrt(ms + NORM_EPS) * gain


def _dot(a, b):
    return jnp.dot(a, b, preferred_element_type=F32)


def _dot_nt(a, b):
    return lax.dot_general(a, b, (((1,), (1,)), ((), ())), preferred_element_type=F32)


def _dot_tn(a, b):
    return lax.dot_general(a, b, (((0,), (0,)), ((), ())), preferred_element_type=F32)


def _split3(x):
    hi = x.astype(BF16)
    r = x - hi.astype(F32)
    mid = r.astype(BF16)
    lo = (r - mid.astype(F32)).astype(BF16)
    return hi, mid, lo


def _dot_f32_exactrhs(a, b_bf16):
    hi, mid, lo = _split3(a)
    return _dot(hi, b_bf16) + _dot(mid, b_bf16) + _dot(lo, b_bf16)


def _dot_exactlhs_f32(a_bf16, b):
    hi, mid, lo = _split3(b)
    return _dot(a_bf16, hi) + _dot(a_bf16, mid) + _dot(a_bf16, lo)


def _ffn_kernel(x_ref, gin_ref, gout_ref, wup_ref, wd_ref, o_ref, act_ref):
    x = x_ref[...]
    h = _rms(x, gin_ref[...]).astype(BF16)
    d_ff = wd_ref.shape[1]
    for c in range(d_ff // FF_CHUNK):
        sl = slice(c * FF_CHUNK, (c + 1) * FF_CHUNK)
        g = _dot(h, wup_ref[0, :, sl])
        u = _dot(h, wup_ref[0, :, d_ff + c * FF_CHUNK:d_ff + (c + 1) * FF_CHUNK])
        act_ref[:, sl] = (g * jax.nn.sigmoid(g) * u).astype(BF16)
    y = _dot(act_ref[...], wd_ref[0])
    o_ref[...] = x + 0.5 * _rms(y, gout_ref[...])


def _ffn(x2, g_in, g_out, w_up, w_down, layer):
    n, d = x2.shape
    d_ff = w_down.shape[1]
    tm = min(ROW_TILE, n)
    row = pl.BlockSpec((tm, d), lambda i: (i, 0))
    return pl.pallas_call(
        _ffn_kernel,
        grid=(n // tm,),
        in_specs=[row, _const_spec((1, d)), _const_spec((1, d)),
                  pl.BlockSpec((1, d, 2 * d_ff), lambda i: (layer, 0, 0)),
                  pl.BlockSpec((1, d_ff, d), lambda i: (layer, 0, 0))],
        out_specs=row,
        out_shape=jax.ShapeDtypeStruct((n, d), F32),
        scratch_shapes=[pltpu.VMEM((tm, d_ff), BF16)],
        compiler_params=_params("parallel"),
        name="ffn_half_step",
    )(x2, g_in, g_out, w_up, w_down)


def _cast_kernel(w_ref, o_ref):
    o_ref[...] = w_ref[...].astype(o_ref.dtype)


def _to_bf16(w):
    depth, rows, cols = w.shape
    tr = min(rows, CAST_ROWS)
    assert rows % tr == 0
    spec = pl.BlockSpec((1, tr, cols), lambda li, ri: (li, ri, 0))
    return pl.pallas_call(
        _cast_kernel,
        grid=(depth, rows // tr),
        in_specs=[spec],
        out_specs=spec,
        out_shape=jax.ShapeDtypeStruct(w.shape, BF16),
        compiler_params=_params("parallel", "parallel"),
        name="weights_to_bf16",
    )(w)


def _inproj_kernel(x_ref, g_ref, wa_ref, wba_ref, *refs, sb_w, conv_dim, gdn_w, sb_tile, q_scale):
    q_ref, k_ref, v_ref, kt_ref, vb_ref, u_ref, z_ref, ba_ref, bat_ref = refs[-9:]
    h = _rms(x_ref[0], g_ref[...]).astype(BF16)
    tm = h.shape[0]

    def proj(lo, width):
        return _dot(h, wa_ref[:, lo:lo + width])

    q_ref[0] = (proj(0, sb_w) * q_scale).astype(BF16)
    k = proj(sb_w, sb_w)
    k_ref[0, 0] = k
    for later in range(1, k_ref.shape[0]):
        k_ref[later, 0] = jnp.zeros_like(k)
        v_ref[later, 0] = jnp.zeros_like(k)
    kt = k.T.astype(BF16)
    for hp in range(sb_w // SB_GROUP_LANES):
        for s in range(tm // sb_tile):
            kt_ref[0, hp, s] = kt[hp * SB_GROUP_LANES:(hp + 1) * SB_GROUP_LANES,
                                  s * sb_tile:(s + 1) * sb_tile]
    v = proj(2 * sb_w, sb_w)
    v_ref[0, 0] = v
    vb_ref[0] = v.astype(BF16)
    u_ref[0] = proj(3 * sb_w, conv_dim)
    z_ref[0] = proj(3 * sb_w + conv_dim, gdn_w)
    ba = _dot(h, wba_ref[...])
    ba_ref[0] = ba
    bat_ref[0] = ba.T[:8]


def _inproj(x, gain, wa, wba, kv_stacks, *, layer, depth, sb_w, conv_dim, gdn_w, sb_tile):
    b, l, d = x.shape
    tm = min(ROW_TILE, l)
    n_hp = sb_w // SB_GROUP_LANES
    kern = functools.partial(_inproj_kernel, sb_w=sb_w, conv_dim=conv_dim, gdn_w=gdn_w,
                             sb_tile=sb_tile,
                             q_scale=float((sb_w // SB_HEADS) ** -0.5))

    def rows(width):
        return pl.BlockSpec((1, tm, width), lambda bi, ti: (bi, ti, 0))

    out_shape = (
        jax.ShapeDtypeStruct((b, l, sb_w), BF16),
        jax.ShapeDtypeStruct((depth, b, l, sb_w), F32),
        jax.ShapeDtypeStruct((depth, b, l, sb_w), F32),
        jax.ShapeDtypeStruct((b, n_hp, l // sb_tile, SB_GROUP_LANES, sb_tile), BF16),
        jax.ShapeDtypeStruct((b, l, sb_w), BF16),
        jax.ShapeDtypeStruct((b, l, conv_dim), F32),
        jax.ShapeDtypeStruct((b, l, gdn_w), F32),
        jax.ShapeDtypeStruct((b, l, LANES), F32),
        jax.ShapeDtypeStruct((b, 8, l), F32),
    )
    if kv_stacks:
        layer_rows = pl.BlockSpec((1, 1, tm, sb_w), lambda bi, ti: (layer, bi, ti, 0))
    else:
        assert layer == 0
        layer_rows = pl.BlockSpec((depth, 1, tm, sb_w), lambda bi, ti: (0, bi, ti, 0))
    out_specs = (
        rows(sb_w), layer_rows, layer_rows,
        pl.BlockSpec((1, n_hp, tm // sb_tile, SB_GROUP_LANES, sb_tile), lambda bi, ti: (bi, 0, ti, 0, 0)),
        rows(sb_w), rows(conv_dim), rows(gdn_w), rows(LANES),
        pl.BlockSpec((1, 8, tm), lambda bi, ti: (bi, 0, ti)),
    )
    return pl.pallas_call(
        kern,
        grid=(b, l // tm),
        in_specs=[rows(d), _const_spec((1, d)), _const_spec(wa.shape), _const_spec(wba.shape)]
        + [pl.BlockSpec(memory_space=pl.ANY)] * len(kv_stacks),
        out_specs=out_specs,
        out_shape=out_shape,
        input_output_aliases={4 + i: 1 + i for i in range(len(kv_stacks))},
        compiler_params=_params("parallel", "parallel"),
        name="in_projection",
    )(x, gain, wa, wba, *kv_stacks)


def _sb_kernel(q_ref, ktd_ref, vd_ref, ktp_ref, vp_ref, o_ref, carry_ref, *, tq, td, nd, tk, heads, dh,
               causal_past, n_past_static, past_keys_on_lanes):
    qi = pl.program_id(2)

    def later_key_matrix(n):
        r = lax.broadcasted_iota(jnp.int32, (n, n), 0)
        c = lax.broadcasted_iota(jnp.int32, (n, n), 1)
        return (r > c).astype(BF16)

    def sweep(blocks, st):
        pairs = [(blk, hh) for blk in blocks for hh in range(heads)]
        scores = []
        for blk, hh in pairs:
            lanes = slice(hh * dh, (hh + 1) * dh)
            z = _dot(q_all[:, lanes], blk["kt"][lanes, :])
            if blk["visible"] is not None:
                z = jnp.where(blk["visible"], z, SB_MASKED_SCORE)
            nlk = jnp.maximum(z, 0.0) + jnp.log(1.0 + jnp.exp2(jnp.abs(z) * (-LOG2E)))
            scores.append((z - nlk, nlk.astype(BF16), jnp.sum(nlk, axis=1, keepdims=True)))
        csums = [_dot(nlk_b, blk["u"]) for (blk, _), (_, nlk_b, _) in zip(pairs, scores)]
        st = list(st)
        for (blk, hh), (log_beta, _, row_sum), csum in zip(pairs, scores, csums):
            lanes = slice(hh * dh, (hh + 1) * dh)
            carry, acc = st[2 * hh], st[2 * hh + 1]
            if carry is not None and blk.get("carry_bias") is not None:
                carry = carry + blk["carry_bias"]
            log_a = log_beta - csum if carry is None else log_beta - csum - carry
            a = jnp.exp(log_a).astype(BF16)
            pv = _dot_nt(a, blk["vt"][lanes, :]) if "vt" in blk else _dot(a, blk["v"][:, lanes])
            st[2 * hh + 1] = pv if acc is None else acc + pv
            st[2 * hh] = row_sum if carry is None else carry + row_sum
        return st

    q_all = q_ref[0]
    state = [None] * (2 * heads)

    u_diag = later_key_matrix(td)
    u_past = u_diag if tk == td else later_key_matrix(tk)
    rows_i = lax.broadcasted_iota(jnp.int32, (tq, td), 0)
    cols_i = lax.broadcasted_iota(jnp.int32, (tq, td), 1)
    n_past = qi * (tq // tk) if causal_past else n_past_static

    def past_block(j, carry_bias=None):
        row0 = pl.multiple_of(j * tk, tk)
        blk = dict(visible=None, u=u_past, carry_bias=carry_bias)
        if past_keys_on_lanes:
            blk["kt"] = ktp_ref[0, :, pl.ds(row0, tk)].astype(BF16)
            blk["vt"] = vp_ref[0, :, pl.ds(row0, tk)].astype(BF16)
        else:
            blk["kt"] = ktp_ref[0, 0, j]
            blk["v"] = vp_ref[0, pl.ds(row0, tk), :]
        return blk

    first = [dict(kt=ktd_ref[0, 0, sd], v=vd_ref[0, sd * td:(sd + 1) * td, :],
                  visible=cols_i + sd * td < rows_i, u=u_diag) for sd in reversed(range(nd))]
    no_past_bias = jnp.where(n_past > 0, 0.0, SB_NO_PAST_CARRY) if causal_past else None
    first.append(past_block(jnp.maximum(n_past - 1, 0), no_past_bias))
    state = sweep(first, state)

    def min_carry(st):
        m = st[0]
        for hh in range(1, heads):
            m = jnp.minimum(m, st[2 * hh])
        return jnp.min(m)

    def cond(loop):
        i, smallest, _ = loop
        return jnp.logical_and(i < n_past, smallest < SB_ZERO_WEIGHT_LOG)

    def body(loop):
        i, _, st = loop
        st = sweep([past_block(n_past - 1 - i)], st)
        return i + 1, min_carry(st), tuple(st)

    _, smallest, st = lax.while_loop(cond, body, (jnp.int32(1), min_carry(state), tuple(state)))
    o_ref[0] = jnp.concatenate([st[2 * hh + 1] for hh in range(heads)], axis=1).astype(BF16)
    carry_ref[...] = jnp.full(carry_ref.shape, smallest, F32)


def _sb_attention(q, kt_diag, v_diag, k_past, v_past, *, tq, td, tk, causal_past):
    b, l, w = q.shape
    n_g, hw = kt_diag.shape[1], kt_diag.shape[3]
    keys_on_lanes = k_past.ndim == 3
    p = v_past.shape[2] if keys_on_lanes else v_past.shape[1]
    n_past_blocks = p // tk
    assert tq % td == 0 and (not causal_past or tq % tk == 0)
    dh = w // SB_HEADS
    kern = functools.partial(_sb_kernel, tq=tq, td=td, nd=tq // td, tk=tk, heads=hw // dh, dh=dh,
                             causal_past=causal_past, n_past_static=n_past_blocks,
                             past_keys_on_lanes=keys_on_lanes)
    if keys_on_lanes:
        past_specs = [pl.BlockSpec((1, hw, p), lambda bi, hp, qi: (bi, hp, 0),
                                   pipeline_mode=pl.Buffered(1))] * 2
    else:
        past_specs = [pl.BlockSpec((1, 1, n_past_blocks, hw, tk), lambda bi, hp, qi: (bi, hp, 0, 0, 0),
                                   pipeline_mode=pl.Buffered(1)),
                      pl.BlockSpec((1, p, hw), lambda bi, hp, qi: (bi, 0, hp),
                                   pipeline_mode=pl.Buffered(1))]
    return pl.pallas_call(
        kern,
        grid=(b, n_g, l // tq),
        in_specs=[
            pl.BlockSpec((1, tq, hw), lambda bi, hp, qi: (bi, qi, hp)),
            pl.BlockSpec((1, 1, tq // td, hw, td), lambda bi, hp, qi: (bi, hp, qi, 0, 0)),
            pl.BlockSpec((1, tq, hw), lambda bi, hp, qi: (bi, qi, hp)),
        ] + past_specs,
        out_specs=(pl.BlockSpec((1, tq, hw), lambda bi, hp, qi: (bi, qi, hp)),
                   pl.BlockSpec((1, 1, 1, 8, LANES), lambda bi, hp, qi: (bi, hp, qi, 0, 0))),
        out_shape=(jax.ShapeDtypeStruct((b, l, w), BF16),
                   jax.ShapeDtypeStruct((b, n_g, l // tq, 8, LANES), F32)),
        compiler_params=_params("parallel", "parallel", "arbitrary"),
        name="stick_breaking_attention",
    )(q, kt_diag, v_diag, k_past, v_past)


def _gdn_kernel(u_ref, z_ref, ba_ref, bat_ref, hist_ref, s0_ref, cw_ref,
                alog_l_ref, dt_l_ref, alog_s_ref, dt_s_ref, hg_ref,
                o_ref, s_ref, cnew_ref, ext_ref, *, rows, chunk, width, n_taps):
    step = pl.program_id(1)
    n_steps = pl.num_programs(1)
    dk = width // GDN_HEADS
    n_sub = rows // chunk
    pad = 8
    n_hist = n_taps - 1

    @pl.when(step == 0)
    def _():
        ext_ref[0:pad, :] = jnp.zeros((pad, ext_ref.shape[1]), F32)
        ext_ref[pad - n_hist:pad, :] = hist_ref[0]
        s_ref[...] = s0_ref[...]

    ext_ref[pad:pad + rows, :] = u_ref[0]
    y = ext_ref[pad:pad + rows, :] * cw_ref[n_hist:n_taps, :]
    for i in reversed(range(n_hist)):
        y = y + ext_ref[pad - n_hist + i:pad - n_hist + i + rows, :] * cw_ref[i:i + 1, :]
    new_tail = ext_ref[rows:rows + pad, :]
    ext_ref[0:pad, :] = new_tail

    @pl.when(step == n_steps - 1)
    def _():
        cnew_ref[0] = new_tail[pad - n_hist:pad, :]

    qkv = y * jax.nn.sigmoid(y)

    heads = range(GDN_HEADS)
    qs, ks, vs = [], [], []
    for h in heads:
        q = qkv[:, h * dk:(h + 1) * dk]
        k = qkv[:, width + h * dk:width + (h + 1) * dk]
        qs.append(q * lax.rsqrt(jnp.sum(q * q, axis=-1, keepdims=True) + NORM_EPS) * float(dk ** -0.5))
        ks.append(k * lax.rsqrt(jnp.sum(k * k, axis=-1, keepdims=True) + NORM_EPS))
        vs.append(qkv[:, 2 * width + h * dk:2 * width + (h + 1) * dk])

    r_i = lax.broadcasted_iota(jnp.int32, (chunk, chunk), 0)
    c_i = lax.broadcasted_iota(jnp.int32, (chunk, chunk), 1)
    incl = r_i >= c_i
    strict = r_i > c_i
    eye = (r_i == c_i).astype(F32)
    rr = lax.broadcasted_iota(jnp.int32, (rows, rows), 0)
    cc = lax.broadcasted_iota(jnp.int32, (rows, rows), 1)
    same_chunk = (rr // chunk) == (cc // chunk)
    lower_incl = (same_chunk & (rr >= cc)).astype(BF16)
    upper_incl = (same_chunk & (rr <= cc)).astype(BF16)

    def softplus(t):
        return jnp.maximum(t, 0.0) + jnp.log(1.0 + jnp.exp(-jnp.abs(t)))

    ba = ba_ref[0]
    g_cols = -jnp.exp(alog_l_ref[...]) * softplus(ba + dt_l_ref[...])
    gcum_cols = _dot_exactlhs_f32(lower_incl, g_cols)
    beta_cols = jax.nn.sigmoid(ba)
    bat = bat_ref[0]
    g_rows = -jnp.exp(alog_s_ref[...]) * softplus(bat + dt_s_ref[...])
    gcum_rows = _dot_f32_exactrhs(g_rows, upper_incl)

    pairs = [(ci, h) for ci in range(n_sub) for h in heads]

    pre = {}
    for ci, h in pairs:
        rs = slice(ci * chunk, (ci + 1) * chunk)
        gc_col = gcum_cols[rs, GDN_HEADS + h:GDN_HEADS + h + 1]
        gc_row = gcum_rows[GDN_HEADS + h:GDN_HEADS + h + 1, rs]
        gc_last = gc_row[:, chunk - 1:chunk]
        beta = beta_cols[rs, h:h + 1]
        gamma = jnp.where(incl, jnp.exp(gc_col - gc_row), 0.0)
        decay_in = jnp.exp(gc_col)
        q, k, v = qs[h][rs], ks[h][rs], vs[h][rs]
        kb = k * beta
        kq = _dot_nt(jnp.concatenate([kb, q], axis=0).astype(BF16), k.astype(BF16))
        pre[ci, h] = dict(
            n=jnp.where(strict, -(kq[:chunk] * gamma), 0.0),
            qk=(kq[chunk:] * gamma).astype(BF16),
            rhs=jnp.concatenate([v * beta, kb * decay_in], axis=1).astype(BF16),
            q_dec=(q * decay_in).astype(BF16),
            k_end=(k * jnp.exp(gc_last - gc_col)).astype(BF16),
            chunk_decay=jnp.exp(gc_last))

    n_rounds = max(1, (chunk - 1).bit_length())
    m_pow = {p: pre[p]["n"] for p in pairs}
    t_inv = {p: eye + pre[p]["n"] for p in pairs}
    for i in range(n_rounds):
        last = i == n_rounds - 1
        for p in pairs:
            m_bf = m_pow[p].astype(BF16)
            if i == 0:
                m_pow[p] = _dot(m_bf, m_bf)
            elif last:
                t_inv[p] = t_inv[p] + _dot(t_inv[p].astype(BF16), m_bf)
            else:
                both = _dot(jnp.concatenate([m_bf, t_inv[p].astype(BF16)], axis=0), m_bf)
                m_pow[p] = both[:chunk]
                t_inv[p] = t_inv[p] + both[chunk:]

    sol = {p: _dot(t_inv[p].astype(BF16), pre[p]["rhs"]) for p in pairs}

    state = [s_ref[0, h] for h in heads]
    o_rows = [[None] * n_sub for _ in heads]
    for ci in range(n_sub):
        ws = [_dot(jnp.concatenate([sol[ci, h][:, dk:].astype(BF16), pre[ci, h]["q_dec"]], axis=0),
                   state[h].astype(BF16)) for h in heads]
        v_new = [(sol[ci, h][:, :dk] - ws[h][:chunk]).astype(BF16) for h in heads]
        for h in heads:
            o_rows[h][ci] = ws[h][chunk:] + _dot(pre[ci, h]["qk"], v_new[h])
        state = [state[h] * pre[ci, h]["chunk_decay"] + _dot_tn(pre[ci, h]["k_end"], v_new[h])
                 for h in heads]
    for h in heads:
        s_ref[0, h] = state[h]

    z = z_ref[0]
    outs = []
    for h in heads:
        o_h = o_rows[h][0] if n_sub == 1 else jnp.concatenate(o_rows[h], axis=0)
        z_h = z[:, h * dk:(h + 1) * dk]
        outs.append(_rms(o_h, hg_ref[...]) * (z_h * jax.nn.sigmoid(z_h)))
    o_ref[0] = jnp.concatenate(outs, axis=1).astype(BF16)


def _gdn(u, z, ba, bat, hist, s0, conv_w, alog_l, dt_l, alog_s, dt_s, head_gain, *, chunk):
    b, l, conv_dim = u.shape
    width = conv_dim // 3
    dk = width // GDN_HEADS
    n_taps = conv_w.shape[0]
    rows = min(GDN_ROWS, l)
    kern = functools.partial(_gdn_kernel, rows=rows, chunk=chunk, width=width, n_taps=n_taps)

    def row_block(wd):
        return pl.BlockSpec((1, rows, wd), lambda bi, si: (bi, si, 0))

    state_spec = pl.BlockSpec((1, GDN_HEADS, dk, dk), lambda bi, si: (bi, 0, 0, 0))
    hist_spec = pl.BlockSpec((1, n_taps - 1, conv_dim), lambda bi, si: (bi, 0, 0))
    return pl.pallas_call(
        kern,
        grid=(b, l // rows),
        in_specs=[row_block(conv_dim), row_block(width), row_block(LANES),
                  pl.BlockSpec((1, 8, rows), lambda bi, si: (bi, 0, si)),
                  hist_spec, state_spec, _const_spec(conv_w.shape),
                  _const_spec((1, LANES)), _const_spec((1, LANES)),
                  _const_spec((8, 1)), _const_spec((8, 1)), _const_spec((1, dk))],
        out_specs=(row_block(width), state_spec, hist_spec),
        out_shape=(jax.ShapeDtypeStruct((b, l, width), BF16),
                   jax.ShapeDtypeStruct(s0.shape, F32),
                   jax.ShapeDtypeStruct(hist.shape, F32)),
        scratch_shapes=[pltpu.VMEM((rows + 8, conv_dim), F32)],
        compiler_params=_params("parallel", "arbitrary"),
        name="gated_delta_rule",
    )(u, z, ba, bat, hist, s0, conv_w, alog_l, dt_l, alog_s, dt_s, head_gain)


def _merge_kernel(x_ref, osb_ref, ogdn_ref, gin_ref, gout_ref, wgate_ref, wsb_ref, wgdn_ref, wout_ref, o_ref):
    x = x_ref[...]
    d = x.shape[1]
    h = _rms(x, gin_ref[...]).astype(BF16)
    gate_sb = jax.nn.sigmoid(_dot(h, wgate_ref[:, :d]))
    merged = gate_sb * _dot(osb_ref[...], wsb_ref[...])
    gate_gdn = jax.nn.sigmoid(_dot(h, wgate_ref[:, d:]))
    merged = merged + gate_gdn * _dot(ogdn_ref[...], wgdn_ref[...])
    m = _dot(merged.astype(BF16), wout_ref[...])
    o_ref[...] = x + _rms(m, gout_ref[...])


def _merge(x2, osb2, ogdn2, g_in, g_out, wgate, wsb, wgdn, wout):
    n, d = x2.shape
    tm = min(ROW_TILE, n)

    def rows(wd):
        return pl.BlockSpec((tm, wd), lambda i: (i, 0))

    return pl.pallas_call(
        _merge_kernel,
        grid=(n // tm,),
        in_specs=[rows(d), rows(osb2.shape[1]), rows(ogdn2.shape[1]),
                  _const_spec((1, d)), _const_spec((1, d)),
                  _const_spec(wgate.shape), _const_spec(wsb.shape), _const_spec(wgdn.shape),
                  _const_spec(wout.shape)],
        out_specs=rows(d),
        out_shape=jax.ShapeDtypeStruct((n, d), F32),
        compiler_params=_params("parallel"),
        name="branch_merge",
    )(x2, osb2, ogdn2, g_in, g_out, wgate, wsb, wgdn, wout)


def _layer_weights(l, norm_gains, w_in_bf, conv_w, gdn_a_log, gdn_dt_bias, gdn_norm_gain,
                   w_branch_sb, w_branch_gdn, w_out):
    d = w_in_bf.shape[1]
    sb_w = w_branch_sb.shape[1]
    gdn_w = w_branch_gdn.shape[1]
    conv_dim = conv_w.shape[2]
    n_a = 3 * sb_w + conv_dim + gdn_w
    wi = w_in_bf[l]
    wba = jnp.zeros((d, LANES), BF16).at[:, :2 * GDN_HEADS].set(wi[:, n_a:n_a + 2 * GDN_HEADS])

    def lane_vec(p):
        return jnp.zeros((1, LANES), F32).at[0, GDN_HEADS:2 * GDN_HEADS].set(p)

    def sublane_vec(p):
        return jnp.zeros((8, 1), F32).at[GDN_HEADS:2 * GDN_HEADS, 0].set(p)

    return dict(
        gains=[norm_gains[l, i][None, :] for i in range(6)],
        wa=wi[:, :n_a],
        wba=wba,
        wgate=wi[:, n_a + 2 * GDN_HEADS:],
        wsb=w_branch_sb[l].astype(BF16), wgdn=w_branch_gdn[l].astype(BF16), wout=w_out[l].astype(BF16),
        conv_w=conv_w[l],
        alog_l=lane_vec(gdn_a_log[l]), dt_l=lane_vec(gdn_dt_bias[l]),
        alog_s=sublane_vec(gdn_a_log[l]), dt_s=sublane_vec(gdn_dt_bias[l]),
        head_gain=gdn_norm_gain[l][None, :],
        dims=(sb_w, conv_dim, gdn_w),
    )


def _run_group(x, weights, ffn_w, past_k, past_v, conv_hist, s0, chunk):
    b, l, d = x.shape
    n = b * l
    s_list, c_list = [], []
    kv_stacks = ()
    for li, w in enumerate(weights):
        sb_w, conv_dim, gdn_w = w["dims"]
        dh = sb_w // SB_HEADS
        tq = min(SB_QUERY_TILE, l)
        td = min(SB_KEY_TILE, l)
        g = w["gains"]
        x2 = _ffn(x.reshape(n, d), g[0], g[1], ffn_w[0], ffn_w[1], li)
        q, k_stack, v_stack, kt, vb, u, z, ba, bat = _inproj(
            x2.reshape(b, l, d), g[2], w["wa"], w["wba"], kv_stacks,
            layer=li, depth=len(weights), sb_w=sb_w, conv_dim=conv_dim, gdn_w=gdn_w, sb_tile=td)
        kv_stacks = (k_stack, v_stack)
        if past_k is None:
            o_sb, _ = _sb_attention(q, kt, vb, kt, vb, tq=tq, td=td, tk=td, causal_past=True)
        else:
            p = past_k.shape[2]
            tk = min(SB_KEY_TILE, p)

            def attend(kt_p, vt_p):
                return _sb_attention(q, kt, vb, kt_p, vt_p, tq=tq, td=td, tk=tk, causal_past=False)

            def keys_on_lanes(rows):
                return jnp.transpose(rows, (0, 2, 3, 1)).reshape(b, sb_w, rows.shape[1])

            newest = lax.optimization_barrier((past_k[li][:, p - tk:], past_v[li][:, p - tk:]))
            o_sb, carry_min = attend(*(keys_on_lanes(t).astype(BF16) for t in newest))
            if p > tk:
                o_sb = lax.cond(jnp.min(carry_min) >= SB_ZERO_WEIGHT_LOG, lambda: o_sb,
                                lambda: attend(keys_on_lanes(past_k[li]), keys_on_lanes(past_v[li]))[0])
        o_gdn, s_fin, conv_new = _gdn(u, z, ba, bat, conv_hist[li], s0[li], w["conv_w"],
                                      w["alog_l"], w["dt_l"], w["alog_s"], w["dt_s"], w["head_gain"],
                                      chunk=chunk)
        x2 = _merge(x2, o_sb.reshape(n, sb_w), o_gdn.reshape(n, gdn_w), g[2], g[3],
                    w["wgate"], w["wsb"], w["wgdn"], w["wout"])
        x2 = _ffn(x2, g[4], g[5], ffn_w[2], ffn_w[3], li)
        x = x2.reshape(b, l, d)
        s_list.append(s_fin)
        c_list.append(conv_new)
    k_all, v_all = (t.reshape(len(weights), b, l, SB_HEADS, dh) for t in kv_stacks)
    return x, k_all, v_all, jnp.stack(s_list), jnp.stack(c_list)


def kernel(x_prompt, x_sample, cache_sb_k, cache_sb_v, state_gdn, state_conv, norm_gains,
           w_ffn1_up, w_ffn1_down, w_in, conv_w, gdn_a_log, gdn_dt_bias, gdn_norm_gain,
           w_branch_sb, w_branch_gdn, w_out, w_ffn2_up, w_ffn2_down):
    depth = w_in.shape[0]
    ffn_w = tuple(_to_bf16(w) for w in (w_ffn1_up, w_ffn1_down, w_ffn2_up, w_ffn2_down))
    w_in_bf = w_in.astype(BF16)
    weights = [_layer_weights(l, norm_gains, w_in_bf, conv_w, gdn_a_log, gdn_dt_bias, gdn_norm_gain,
                              w_branch_sb, w_branch_gdn, w_out) for l in range(depth)]
    bp = x_prompt.shape[0]
    chunk_prompt = 64
    zero_conv = jnp.zeros((depth, bp) + state_conv.shape[2:], state_conv.dtype)
    zero_state = jnp.zeros((depth, bp) + state_gdn.shape[2:], state_gdn.dtype)
    y_p, pk, pv, ps, pc = _run_group(x_prompt, weights, ffn_w, None, None, zero_conv, zero_state, chunk_prompt)
    y_s, sk, sv, ss, sc = _run_group(x_sample, weights, ffn_w, cache_sb_k, cache_sb_v, state_conv, state_gdn,
                                     x_sample.shape[1])
    return (y_p, y_s, pk, pv, ps, pc, sk, sv, ss, sc)
```

```python
import functools

import jax
import jax.numpy as jnp
from jax import lax
from jax.experimental import pallas as pl
from jax.experimental.pallas import tpu as pltpu

F32 = jnp.float32
BF16 = jnp.bfloat16

NORM_EPS = 1e-6
LOG2E = 1.4426950408889634
SB_HEADS = 8
GDN_HEADS = 4
LANES = 128
VMEM_LIMIT_BYTES = 56 * 1024 * 1024

ROW_TILE = 512
CAST_ROWS = 256
FF_CHUNK = 256
GDN_ROWS = 512
SB_KEY_TILE = 256
SB_QUERY_TILE = 256
SB_GROUP_LANES = 512
SB_ZERO_WEIGHT_LOG = 110.0
SB_MASKED_SCORE = -1e30
SB_NO_PAST_CARRY = 1e30


def _params(*sem):
    return pltpu.CompilerParams(dimension_semantics=sem, vmem_limit_bytes=VMEM_LIMIT_BYTES)


def _const_spec(shape):
    zeros = (0,) * len(shape)
    return pl.BlockSpec(shape, lambda *_: zeros)


def _rms(x, gain):
    ms = jnp.mean(x * x, axis=-1, keepdims=True)
    return x * lax.rsqrt(ms + NORM_EPS) * gain


def _dot(a, b):
    return jnp.dot(a, b, preferred_element_type=F32)


def _dot_nt(a, b):
    return lax.dot_general(a, b, (((1,), (1,)), ((), ())), preferred_element_type=F32)


def _dot_tn(a, b):
    return lax.dot_general(a, b, (((0,), (0,)), ((), ())), preferred_element_type=F32)


def _split3(x):
    hi = x.astype(BF16)
    r = x - hi.astype(F32)
    mid = r.astype(BF16)
    lo = (r - mid.astype(F32)).astype(BF16)
    return hi, mid, lo


def _dot_f32_exactrhs(a, b_bf16):
    hi, mid, lo = _split3(a)
    return _dot(hi, b_bf16) + _dot(mid, b_bf16) + _dot(lo, b_bf16)


def _dot_exactlhs_f32(a_bf16, b):
    hi, mid, lo = _split3(b)
    return _dot(a_bf16, hi) + _dot(a_bf16, mid) + _dot(a_bf16, lo)


def _ffn_kernel(x_ref, gin_ref, gout_ref, wup_ref, wd_ref, o_ref, act_ref):
    x = x_ref[...]
    h = _rms(x, gin_ref[...]).astype(BF16)
    d_ff = wd_ref.shape[1]
    for c in range(d_ff // FF_CHUNK):
        sl = slice(c * FF_CHUNK, (c + 1) * FF_CHUNK)
        g = _dot(h, wup_ref[0, :, sl])
        u = _dot(h, wup_ref[0, :, d_ff + c * FF_CHUNK:d_ff + (c + 1) * FF_CHUNK])
        act_ref[:, sl] = (g * jax.nn.sigmoid(g) * u).astype(BF16)
    y = _dot(act_ref[...], wd_ref[0])
    o_ref[...] = x + 0.5 * _rms(y, gout_ref[...])


def _ffn(x2, g_in, g_out, w_up, w_down, layer):
    n, d = x2.shape
    d_ff = w_down.shape[1]
    tm = min(ROW_TILE, n)
    row = pl.BlockSpec((tm, d), lambda i: (i, 0))
    return pl.pallas_call(
        _ffn_kernel,
        grid=(n // tm,),
        in_specs=[row, _const_spec((1, d)), _const_spec((1, d)),
                  pl.BlockSpec((1, d, 2 * d_ff), lambda i: (layer, 0, 0)),
                  pl.BlockSpec((1, d_ff, d), lambda i: (layer, 0, 0))],
        out_specs=row,
        out_shape=jax.ShapeDtypeStruct((n, d), F32),
        scratch_shapes=[pltpu.VMEM((tm, d_ff), BF16)],
        compiler_params=_params("parallel"),
        name="ffn_half_step",
    )(x2, g_in, g_out, w_up, w_down)


def _cast_kernel(w_ref, o_ref):
    o_ref[...] = w_ref[...].astype(o_ref.dtype)


def _to_bf16(w):
    depth, rows, cols = w.shape
    tr = min(rows, CAST_ROWS)
    assert rows % tr == 0
    spec = pl.BlockSpec((1, tr, cols), lambda li, ri: (li, ri, 0))
    return pl.pallas_call(
        _cast_kernel,
        grid=(depth, rows // tr),
        in_specs=[spec],
        out_specs=spec,
        out_shape=jax.ShapeDtypeStruct(w.shape, BF16),
        compiler_params=_params("parallel", "parallel"),
        name="weights_to_bf16",
    )(w)


def _inproj_kernel(x_ref, g_ref, wa_ref, wba_ref, *refs, sb_w, conv_dim, gdn_w, sb_tile, q_scale):
    q_ref, k_ref, v_ref, kt_ref, vb_ref, u_ref, z_ref, ba_ref, bat_ref = refs[-9:]
    h = _rms(x_ref[0], g_ref[...]).astype(BF16)
    tm = h.shape[0]

    def proj(lo, width):
        return _dot(h, wa_ref[:, lo:lo + width])

    q_ref[0] = (proj(0, sb_w) * q_scale).astype(BF16)
    k = proj(sb_w, sb_w)
    k_ref[0, 0] = k
    for later in range(1, k_ref.shape[0]):
        k_ref[later, 0] = jnp.zeros_like(k)
        v_ref[later, 0] = jnp.zeros_like(k)
    kt = k.T.astype(BF16)
    for hp in range(sb_w // SB_GROUP_LANES):
        for s in range(tm // sb_tile):
            kt_ref[0, hp, s] = kt[hp * SB_GROUP_LANES:(hp + 1) * SB_GROUP_LANES,
                                  s * sb_tile:(s + 1) * sb_tile]
    v = proj(2 * sb_w, sb_w)
    v_ref[0, 0] = v
    vb_ref[0] = v.astype(BF16)
    u_ref[0] = proj(3 * sb_w, conv_dim)
    z_ref[0] = proj(3 * sb_w + conv_dim, gdn_w)
    ba = _dot(h, wba_ref[...])
    ba_ref[0] = ba
    bat_ref[0] = ba.T[:8]


def _inproj(x, gain, wa, wba, kv_stacks, *, layer, depth, sb_w, conv_dim, gdn_w, sb_tile):
    b, l, d = x.shape
    tm = min(ROW_TILE, l)
    n_hp = sb_w // SB_GROUP_LANES
    kern = functools.partial(_inproj_kernel, sb_w=sb_w, conv_dim=conv_dim, gdn_w=gdn_w,
                             sb_tile=sb_tile,
                             q_scale=float((sb_w // SB_HEADS) ** -0.5))

    def rows(width):
        return pl.BlockSpec((1, tm, width), lambda bi, ti: (bi, ti, 0))

    out_shape = (
        jax.ShapeDtypeStruct((b, l, sb_w), BF16),
        jax.ShapeDtypeStruct((depth, b, l, sb_w), F32),
        jax.ShapeDtypeStruct((depth, b, l, sb_w), F32),
        jax.ShapeDtypeStruct((b, n_hp, l // sb_tile, SB_GROUP_LANES, sb_tile), BF16),
        jax.ShapeDtypeStruct((b, l, sb_w), BF16),
        jax.ShapeDtypeStruct((b, l, conv_dim), F32),
        jax.ShapeDtypeStruct((b, l, gdn_w), F32),
        jax.ShapeDtypeStruct((b, l, LANES), F32),
        jax.ShapeDtypeStruct((b, 8, l), F32),
    )
    if kv_stacks:
        layer_rows = pl.BlockSpec((1, 1, tm, sb_w), lambda bi, ti: (layer, bi, ti, 0))
    else:
        assert layer == 0
        layer_rows = pl.BlockSpec((depth, 1, tm, sb_w), lambda bi, ti: (0, bi, ti, 0))
    out_specs = (
        rows(sb_w), layer_rows, layer_rows,
        pl.BlockSpec((1, n_hp, tm // sb_tile, SB_GROUP_LANES, sb_tile), lambda bi, ti: (bi, 0, ti, 0, 0)),
        rows(sb_w), rows(conv_dim), rows(gdn_w), rows(LANES),
        pl.BlockSpec((1, 8, tm), lambda bi, ti: (bi, 0, ti)),
    )
    return pl.pallas_call(
        kern,
        grid=(b, l // tm),
        in_specs=[rows(d), _const_spec((1, d)), _const_spec(wa.shape), _const_spec(wba.shape)]
        + [pl.BlockSpec(memory_space=pl.ANY)] * len(kv_stacks),
        out_specs=out_specs,
        out_shape=out_shape,
        input_output_aliases={4 + i: 1 + i for i in range(len(kv_stacks))},
        compiler_params=_params("parallel", "parallel"),
        name="in_projection",
    )(x, gain, wa, wba, *kv_stacks)


def _sb_kernel(q_ref, ktd_ref, vd_ref, ktp_ref, vp_ref, o_ref, carry_ref, *, tq, td, nd, tk, heads, dh,
               causal_past, n_past_static, past_keys_on_lanes):
    qi = pl.program_id(2)

    def later_key_matrix(n):
        r = lax.broadcasted_iota(jnp.int32, (n, n), 0)
        c = lax.broadcasted_iota(jnp.int32, (n, n), 1)
        return (r > c).astype(BF16)

    def sweep(blocks, st):
        pairs = [(blk, hh) for blk in blocks for hh in range(heads)]
        scores = []
        for blk, hh in pairs:
            lanes = slice(hh * dh, (hh + 1) * dh)
            z = _dot(q_all[:, lanes], blk["kt"][lanes, :])
            if blk["visible"] is not None:
                z = jnp.where(blk["visible"], z, SB_MASKED_SCORE)
            nlk = jnp.maximum(z, 0.0) + jnp.log(1.0 + jnp.exp2(jnp.abs(z) * (-LOG2E)))
            scores.append((z - nlk, nlk.astype(BF16), jnp.sum(nlk, axis=1, keepdims=True)))
        csums = [_dot(nlk_b, blk["u"]) for (blk, _), (_, nlk_b, _) in zip(pairs, scores)]
        st = list(st)
        for (blk, hh), (log_beta, _, row_sum), csum in zip(pairs, scores, csums):
            lanes = slice(hh * dh, (hh + 1) * dh)
            carry, acc = st[2 * hh], st[2 * hh + 1]
            if carry is not None and blk.get("carry_bias") is not None:
                carry = carry + blk["carry_bias"]
            log_a = log_beta - csum if carry is None else log_beta - csum - carry
            a = jnp.exp(log_a).astype(BF16)
            pv = _dot_nt(a, blk["vt"][lanes, :]) if "vt" in blk else _dot(a, blk["v"][:, lanes])
            st[2 * hh + 1] = pv if acc is None else acc + pv
            st[2 * hh] = row_sum if carry is None else carry + row_sum
        return st

    q_all = q_ref[0]
    state = [None] * (2 * heads)

    u_diag = later_key_matrix(td)
    u_past = u_diag if tk == td else later_key_matrix(tk)
    rows_i = lax.broadcasted_iota(jnp.int32, (tq, td), 0)
    cols_i = lax.broadcasted_iota(jnp.int32, (tq, td), 1)
    n_past = qi * (tq // tk) if causal_past else n_past_static

    def past_block(j, carry_bias=None):
        row0 = pl.multiple_of(j * tk, tk)
        blk = dict(visible=None, u=u_past, carry_bias=carry_bias)
        if past_keys_on_lanes:
            blk["kt"] = ktp_ref[0, :, pl.ds(row0, tk)].astype(BF16)
            blk["vt"] = vp_ref[0, :, pl.ds(row0, tk)].astype(BF16)
        else:
            blk["kt"] = ktp_ref[0, 0, j]
            blk["v"] = vp_ref[0, pl.ds(row0, tk), :]
        return blk

    first = [dict(kt=ktd_ref[0, 0, sd], v=vd_ref[0, sd * td:(sd + 1) * td, :],
                  visible=cols_i + sd * td < rows_i, u=u_diag) for sd in reversed(range(nd))]
    no_past_bias = jnp.where(n_past > 0, 0.0, SB_NO_PAST_CARRY) if causal_past else None
    first.append(past_block(jnp.maximum(n_past - 1, 0), no_past_bias))
    state = sweep(first, state)

    def min_carry(st):
        m = st[0]
        for hh in range(1, heads):
            m = jnp.minimum(m, st[2 * hh])
        return jnp.min(m)

    def cond(loop):
        i, smallest, _ = loop
        return jnp.logical_and(i < n_past, smallest < SB_ZERO_WEIGHT_LOG)

    def body(loop):
        i, _, st = loop
        st = sweep([past_block(n_past - 1 - i)], st)
        return i + 1, min_carry(st), tuple(st)

    _, smallest, st = lax.while_loop(cond, body, (jnp.int32(1), min_carry(state), tuple(state)))
    o_ref[0] = jnp.concatenate([st[2 * hh + 1] for hh in range(heads)], axis=1).astype(BF16)
    carry_ref[...] = jnp.full(carry_ref.shape, smallest, F32)


def _sb_attention(q, kt_diag, v_diag, k_past, v_past, *, tq, td, tk, causal_past):
    b, l, w = q.shape
    n_g, hw = kt_diag.shape[1], kt_diag.shape[3]
    keys_on_lanes = k_past.ndim == 3
    p = v_past.shape[2] if keys_on_lanes else v_past.shape[1]
    n_past_blocks = p // tk
    assert tq % td == 0 and (not causal_past or tq % tk == 0)
    dh = w // SB_HEADS
    kern = functools.partial(_sb_kernel, tq=tq, td=td, nd=tq // td, tk=tk, heads=hw // dh, dh=dh,
                             causal_past=causal_past, n_past_static=n_past_blocks,
                             past_keys_on_lanes=keys_on_lanes)
    if keys_on_lanes:
        past_specs = [pl.BlockSpec((1, hw, p), lambda bi, hp, qi: (bi, hp, 0),
                                   pipeline_mode=pl.Buffered(1))] * 2
    else:
        past_specs = [pl.BlockSpec((1, 1, n_past_blocks, hw, tk), lambda bi, hp, qi: (bi, hp, 0, 0, 0),
                                   pipeline_mode=pl.Buffered(1)),
                      pl.BlockSpec((1, p, hw), lambda bi, hp, qi: (bi, 0, hp),
                                   pipeline_mode=pl.Buffered(1))]
    return pl.pallas_call(
        kern,
        grid=(b, n_g, l // tq),
        in_specs=[
            pl.BlockSpec((1, tq, hw), lambda bi, hp, qi: (bi, qi, hp)),
            pl.BlockSpec((1, 1, tq // td, hw, td), lambda bi, hp, qi: (bi, hp, qi, 0, 0)),
            pl.BlockSpec((1, tq, hw), lambda bi, hp, qi: (bi, qi, hp)),
        ] + past_specs,
        out_specs=(pl.BlockSpec((1, tq, hw), lambda bi, hp, qi: (bi, qi, hp)),
                   pl.BlockSpec((1, 1, 1, 8, LANES), lambda bi, hp, qi: (bi, hp, qi, 0, 0))),
        out_shape=(jax.ShapeDtypeStruct((b, l, w), BF16),
                   jax.ShapeDtypeStruct((b, n_g, l // tq, 8, LANES), F32)),
        compiler_params=_params("parallel", "parallel", "arbitrary"),
        name="stick_breaking_attention",
    )(q, kt_diag, v_diag, k_past, v_past)


def _gdn_kernel(u_ref, z_ref, ba_ref, bat_ref, hist_ref, s0_ref, cw_ref,
                alog_l_ref, dt_l_ref, alog_s_ref, dt_s_ref, hg_ref,
                o_ref, s_ref, cnew_ref, ext_ref, *, rows, chunk, width, n_taps):
    step = pl.program_id(1)
    n_steps = pl.num_programs(1)
    dk = width // GDN_HEADS
    n_sub = rows // chunk
    pad = 8
    n_hist = n_taps - 1

    @pl.when(step == 0)
    def _():
        ext_ref[0:pad, :] = jnp.zeros((pad, ext_ref.shape[1]), F32)
        ext_ref[pad - n_hist:pad, :] = hist_ref[0]
        s_ref[...] = s0_ref[...]

    ext_ref[pad:pad + rows, :] = u_ref[0]
    y = ext_ref[pad:pad + rows, :] * cw_ref[n_hist:n_taps, :]
    for i in reversed(range(n_hist)):
        y = y + ext_ref[pad - n_hist + i:pad - n_hist + i + rows, :] * cw_ref[i:i + 1, :]
    new_tail = ext_ref[rows:rows + pad, :]
    ext_ref[0:pad, :] = new_tail

    @pl.when(step == n_steps - 1)
    def _():
        cnew_ref[0] = new_tail[pad - n_hist:pad, :]

    qkv = y * jax.nn.sigmoid(y)

    heads = range(GDN_HEADS)
    qs, ks, vs = [], [], []
    for h in heads:
        q = qkv[:, h * dk:(h + 1) * dk]
        k = qkv[:, width + h * dk:width + (h + 1) * dk]
        qs.append(q * lax.rsqrt(jnp.sum(q * q, axis=-1, keepdims=True) + NORM_EPS) * float(dk ** -0.5))
        ks.append(k * lax.rsqrt(jnp.sum(k * k, axis=-1, keepdims=True) + NORM_EPS))
        vs.append(qkv[:, 2 * width + h * dk:2 * width + (h + 1) * dk])

    r_i = lax.broadcasted_iota(jnp.int32, (chunk, chunk), 0)
    c_i = lax.broadcasted_iota(jnp.int32, (chunk, chunk), 1)
    incl = r_i >= c_i
    strict = r_i > c_i
    eye = (r_i == c_i).astype(F32)
    rr = lax.broadcasted_iota(jnp.int32, (rows, rows), 0)
    cc = lax.broadcasted_iota(jnp.int32, (rows, rows), 1)
    same_chunk = (rr // chunk) == (cc // chunk)
    lower_incl = (same_chunk & (rr >= cc)).astype(BF16)
    upper_incl = (same_chunk & (rr <= cc)).astype(BF16)

    def softplus(t):
        return jnp.maximum(t, 0.0) + jnp.log(1.0 + jnp.exp(-jnp.abs(t)))

    ba = ba_ref[0]
    g_cols = -jnp.exp(alog_l_ref[...]) * softplus(ba + dt_l_ref[...])
    gcum_cols = _dot_exactlhs_f32(lower_incl, g_cols)
    beta_cols = jax.nn.sigmoid(ba)
    bat = bat_ref[0]
    g_rows = -jnp.exp(alog_s_ref[...]) * softplus(bat + dt_s_ref[...])
    gcum_rows = _dot_f32_exactrhs(g_rows, upper_incl)

    pairs = [(ci, h) for ci in range(n_sub) for h in heads]

    pre = {}
    for ci, h in pairs:
        rs = slice(ci * chunk, (ci + 1) * chunk)
        gc_col = gcum_cols[rs, GDN_HEADS + h:GDN_HEADS + h + 1]
        gc_row = gcum_rows[GDN_HEADS + h:GDN_HEADS + h + 1, rs]
        gc_last = gc_row[:, chunk - 1:chunk]
        beta = beta_cols[rs, h:h + 1]
        gamma = jnp.where(incl, jnp.exp(gc_col - gc_row), 0.0)
        decay_in = jnp.exp(gc_col)
        q, k, v = qs[h][rs], ks[h][rs], vs[h][rs]
        kb = k * beta
        kq = _dot_nt(jnp.concatenate([kb, q], axis=0).astype(BF16), k.astype(BF16))
        pre[ci, h] = dict(
            n=jnp.where(strict, -(kq[:chunk] * gamma), 0.0),
            qk=(kq[chunk:] * gamma).astype(BF16),
            rhs=jnp.concatenate([v * beta, kb * decay_in], axis=1).astype(BF16),
            q_dec=(q * decay_in).astype(BF16),
            k_end=(k * jnp.exp(gc_last - gc_col)).astype(BF16),
            chunk_decay=jnp.exp(gc_last))

    n_rounds = max(1, (chunk - 1).bit_length())
    m_pow = {p: pre[p]["n"] for p in pairs}
    t_inv = {p: eye + pre[p]["n"] for p in pairs}
    for i in range(n_rounds):
        last = i == n_rounds - 1
        for p in pairs:
            m_bf = m_pow[p].astype(BF16)
            if i == 0:
                m_pow[p] = _dot(m_bf, m_bf)
            elif last:
                t_inv[p] = t_inv[p] + _dot(t_inv[p].astype(BF16), m_bf)
            else:
                both = _dot(jnp.concatenate([m_bf, t_inv[p].astype(BF16)], axis=0), m_bf)
                m_pow[p] = both[:chunk]
                t_inv[p] = t_inv[p] + both[chunk:]

    sol = {p: _dot(t_inv[p].astype(BF16), pre[p]["rhs"]) for p in pairs}

    state = [s_ref[0, h] for h in heads]
    o_rows = [[None] * n_sub for _ in heads]
    for ci in range(n_sub):
        ws = [_dot(jnp.concatenate([sol[ci, h][:, dk:].astype(BF16), pre[ci, h]["q_dec"]], axis=0),
                   state[h].astype(BF16)) for h in heads]
        v_new = [(sol[ci, h][:, :dk] - ws[h][:chunk]).astype(BF16) for h in heads]
        for h in heads:
            o_rows[h][ci] = ws[h][chunk:] + _dot(pre[ci, h]["qk"], v_new[h])
        state = [state[h] * pre[ci, h]["chunk_decay"] + _dot_tn(pre[ci, h]["k_end"], v_new[h])
                 for h in heads]
    for h in heads:
        s_ref[0, h] = state[h]

    z = z_ref[0]
    outs = []
    for h in heads:
        o_h = o_rows[h][0] if n_sub == 1 else jnp.concatenate(o_rows[h], axis=0)
        z_h = z[:, h * dk:(h + 1) * dk]
        outs.append(_rms(o_h, hg_ref[...]) * (z_h * jax.nn.sigmoid(z_h)))
    o_ref[0] = jnp.concatenate(outs, axis=1).astype(BF16)


def _gdn(u, z, ba, bat, hist, s0, conv_w, alog_l, dt_l, alog_s, dt_s, head_gain, *, chunk):
    b, l, conv_dim = u.shape
    width = conv_dim // 3
    dk = width // GDN_HEADS
    n_taps = conv_w.shape[0]
    rows = min(GDN_ROWS, l)
    kern = functools.partial(_gdn_kernel, rows=rows, chunk=chunk, width=width, n_taps=n_taps)

    def row_block(wd):
        return pl.BlockSpec((1, rows, wd), lambda bi, si: (bi, si, 0))

    state_spec = pl.BlockSpec((1, GDN_HEADS, dk, dk), lambda bi, si: (bi, 0, 0, 0))
    hist_spec = pl.BlockSpec((1, n_taps - 1, conv_dim), lambda bi, si: (bi, 0, 0))
    return pl.pallas_call(
        kern,
        grid=(b, l // rows),
        in_specs=[row_block(conv_dim), row_block(width), row_block(LANES),
                  pl.BlockSpec((1, 8, rows), lambda bi, si: (bi, 0, si)),
                  hist_spec, state_spec, _const_spec(conv_w.shape),
                  _const_spec((1, LANES)), _const_spec((1, LANES)),
                  _const_spec((8, 1)), _const_spec((8, 1)), _const_spec((1, dk))],
        out_specs=(row_block(width), state_spec, hist_spec),
        out_shape=(jax.ShapeDtypeStruct((b, l, width), BF16),
                   jax.ShapeDtypeStruct(s0.shape, F32),
                   jax.ShapeDtypeStruct(hist.shape, F32)),
        scratch_shapes=[pltpu.VMEM((rows + 8, conv_dim), F32)],
        compiler_params=_params("parallel", "arbitrary"),
        name="gated_delta_rule",
    )(u, z, ba, bat, hist, s0, conv_w, alog_l, dt_l, alog_s, dt_s, head_gain)


def _merge_kernel(x_ref, osb_ref, ogdn_ref, gin_ref, gout_ref, wgate_ref, wsb_ref, wgdn_ref, wout_ref, o_ref):
    x = x_ref[...]
    d = x.shape[1]
    h = _rms(x, gin_ref[...]).astype(BF16)
    gate_sb = jax.nn.sigmoid(_dot(h, wgate_ref[:, :d]))
    merged = gate_sb * _dot(osb_ref[...], wsb_ref[...])
    gate_gdn = jax.nn.sigmoid(_dot(h, wgate_ref[:, d:]))
    merged = merged + gate_gdn * _dot(ogdn_ref[...], wgdn_ref[...])
    m = _dot(merged.astype(BF16), wout_ref[...])
    o_ref[...] = x + _rms(m, gout_ref[...])


def _merge(x2, osb2, ogdn2, g_in, g_out, wgate, wsb, wgdn, wout):
    n, d = x2.shape
    tm = min(ROW_TILE, n)

    def rows(wd):
        return pl.BlockSpec((tm, wd), lambda i: (i, 0))

    return pl.pallas_call(
        _merge_kernel,
        grid=(n // tm,),
        in_specs=[rows(d), rows(osb2.shape[1]), rows(ogdn2.shape[1]),
                  _const_spec((1, d)), _const_spec((1, d)),
                  _const_spec(wgate.shape), _const_spec(wsb.shape), _const_spec(wgdn.shape),
                  _const_spec(wout.shape)],
        out_specs=rows(d),
        out_shape=jax.ShapeDtypeStruct((n, d), F32),
        compiler_params=_params("parallel"),
        name="branch_merge",
    )(x2, osb2, ogdn2, g_in, g_out, wgate, wsb, wgdn, wout)


def _layer_weights(l, norm_gains, w_in_bf, conv_w, gdn_a_log, gdn_dt_bias, gdn_norm_gain,
                   w_branch_sb, w_branch_gdn, w_out):
    d = w_in_bf.shape[1]
    sb_w = w_branch_sb.shape[1]
    gdn_w = w_branch_gdn.shape[1]
    conv_dim = conv_w.shape[2]
    n_a = 3 * sb_w + conv_dim + gdn_w
    wi = w_in_bf[l]
    wba = jnp.zeros((d, LANES), BF16).at[:, :2 * GDN_HEADS].set(wi[:, n_a:n_a + 2 * GDN_HEADS])

    def lane_vec(p):
        return jnp.zeros((1, LANES), F32).at[0, GDN_HEADS:2 * GDN_HEADS].set(p)

    def sublane_vec(p):
        return jnp.zeros((8, 1), F32).at[GDN_HEADS:2 * GDN_HEADS, 0].set(p)

    return dict(
        gains=[norm_gains[l, i][None, :] for i in range(6)],
        wa=wi[:, :n_a],
        wba=wba,
        wgate=wi[:, n_a + 2 * GDN_HEADS:],
        wsb=w_branch_sb[l].astype(BF16), wgdn=w_branch_gdn[l].astype(BF16), wout=w_out[l].astype(BF16),
        conv_w=conv_w[l],
        alog_l=lane_vec(gdn_a_log[l]), dt_l=lane_vec(gdn_dt_bias[l]),
        alog_s=sublane_vec(gdn_a_log[l]), dt_s=sublane_vec(gdn_dt_bias[l]),
        head_gain=gdn_norm_gain[l][None, :],
        dims=(sb_w, conv_dim, gdn_w),
    )


def _run_group(x, weights, ffn_w, past_k, past_v, conv_hist, s0, chunk):
    b, l, d = x.shape
    n = b * l
    s_list, c_list = [], []
    kv_stacks = ()
    for li, w in enumerate(weights):
        sb_w, conv_dim, gdn_w = w["dims"]
        dh = sb_w // SB_HEADS
        tq = min(SB_QUERY_TILE, l)
        td = min(SB_KEY_TILE, l)
        g = w["gains"]
        x2 = _ffn(x.reshape(n, d), g[0], g[1], ffn_w[0], ffn_w[1], li)
        q, k_stack, v_stack, kt, vb, u, z, ba, bat = _inproj(
            x2.reshape(b, l, d), g[2], w["wa"], w["wba"], kv_stacks,
            layer=li, depth=len(weights), sb_w=sb_w, conv_dim=conv_dim, gdn_w=gdn_w, sb_tile=td)
        kv_stacks = (k_stack, v_stack)
        if past_k is None:
            o_sb, _ = _sb_attention(q, kt, vb, kt, vb, tq=tq, td=td, tk=td, causal_past=True)
        else:
            p = past_k.shape[2]
            tk = min(SB_KEY_TILE, p)

            def attend(kt_p, vt_p):
                return _sb_attention(q, kt, vb, kt_p, vt_p, tq=tq, td=td, tk=tk, causal_past=False)

            def keys_on_lanes(rows):
                return jnp.transpose(rows, (0, 2, 3, 1)).reshape(b, sb_w, rows.shape[1])

            o_sb, carry_min = attend(keys_on_lanes(past_k[li][:, p - tk:]),
                                     keys_on_lanes(past_v[li][:, p - tk:]))
            if p > tk:
                o_sb = lax.cond(jnp.min(carry_min) >= SB_ZERO_WEIGHT_LOG, lambda: o_sb,
                                lambda: attend(keys_on_lanes(past_k[li]), keys_on_lanes(past_v[li]))[0])
        o_gdn, s_fin, conv_new = _gdn(u, z, ba, bat, conv_hist[li], s0[li], w["conv_w"],
                                      w["alog_l"], w["dt_l"], w["alog_s"], w["dt_s"], w["head_gain"],
                                      chunk=chunk)
        x2 = _merge(x2, o_sb.reshape(n, sb_w), o_gdn.reshape(n, gdn_w), g[2], g[3],
                    w["wgate"], w["wsb"], w["wgdn"], w["wout"])
        x2 = _ffn(x2, g[4], g[5], ffn_w[2], ffn_w[3], li)
        x = x2.reshape(b, l, d)
        s_list.append(s_fin)
        c_list.append(conv_new)
    k_all, v_all = (t.reshape(len(weights), b, l, SB_HEADS, dh) for t in kv_stacks)
    return x, k_all, v_all, jnp.stack(s_list), jnp.stack(c_list)


def kernel(x_prompt, x_sample, cache_sb_k, cache_sb_v, state_gdn, state_conv, norm_gains,
           w_ffn1_up, w_ffn1_down, w_in, conv_w, gdn_a_log, gdn_dt_bias, gdn_norm_gain,
           w_branch_sb, w_branch_gdn, w_out, w_ffn2_up, w_ffn2_down):
    depth = w_in.shape[0]
    ffn_w = tuple(_to_bf16(w) for w in (w_ffn1_up, w_ffn1_down, w_ffn2_up, w_ffn2_down))
    w_in_bf = w_in.astype(BF16)
    weights = [_layer_weights(l, norm_gains, w_in_bf, conv_w, gdn_a_log, gdn_dt_bias, gdn_norm_gain,
                              w_branch_sb, w_branch_gdn, w_out) for l in range(depth)]
    bp = x_prompt.shape[0]
    chunk_prompt = 64
    zero_conv = jnp.zeros((depth, bp) + state_conv.shape[2:], state_conv.dtype)
    zero_state = jnp.zeros((depth, bp) + state_gdn.shape[2:], state_gdn.dtype)
    y_p, pk, pv, ps, pc = _run_group(x_prompt, weights, ffn_w, None, None, zero_conv, zero_state, chunk_prompt)
    y_s, sk, sv, ss, sc = _run_group(x_sample, weights, ffn_w, cache_sb_k, cache_sb_v, state_conv, state_gdn,
                                     x_sample.shape[1])
    return (y_p, y_s, pk, pv, ps, pc, sk, sv, ss, sc)
```

```python
import functools

import jax
import jax.numpy as jnp
from jax import lax
from jax.experimental import pallas as pl
from jax.experimental.pallas import tpu as pltpu

F32 = jnp.float32
BF16 = jnp.bfloat16

NORM_EPS = 1e-6
LOG2E = 1.4426950408889634
SB_HEADS = 8
GDN_HEADS = 4
LANES = 128
VMEM_LIMIT_BYTES = 56 * 1024 * 1024

ROW_TILE = 512
CAST_ROWS = 256
FF_CHUNK = 256
GDN_ROWS = 512
SB_KEY_TILE = 256
SB_QUERY_TILE = 256
SB_GROUP_LANES = 512
SB_ZERO_WEIGHT_LOG = 110.0
SB_MASKED_SCORE = -1e30
SB_NO_PAST_CARRY = 1e30


def _params(*sem):
    return pltpu.CompilerParams(dimension_semantics=sem, vmem_limit_bytes=VMEM_LIMIT_BYTES)


def _const_spec(shape):
    zeros = (0,) * len(shape)
    return pl.BlockSpec(shape, lambda *_: zeros)


def _rms(x, gain):
    ms = jnp.mean(x * x, axis=-1, keepdims=True)
    return x * lax.rsqrt(ms + NORM_EPS) * gain


def _dot(a, b):
    return jnp.dot(a, b, preferred_element_type=F32)


def _dot_nt(a, b):
    return lax.dot_general(a, b, (((1,), (1,)), ((), ())), preferred_element_type=F32)


def _dot_tn(a, b):
    return lax.dot_general(a, b, (((0,), (0,)), ((), ())), preferred_element_type=F32)


def _split3(x):
    hi = x.astype(BF16)
    r = x - hi.astype(F32)
    mid = r.astype(BF16)
    lo = (r - mid.astype(F32)).astype(BF16)
    return hi, mid, lo


def _dot_f32_exactrhs(a, b_bf16):
    hi, mid, lo = _split3(a)
    return _dot(hi, b_bf16) + _dot(mid, b_bf16) + _dot(lo, b_bf16)


def _dot_exactlhs_f32(a_bf16, b):
    hi, mid, lo = _split3(b)
    return _dot(a_bf16, hi) + _dot(a_bf16, mid) + _dot(a_bf16, lo)


def _ffn_kernel(x_ref, gin_ref, gout_ref, wup_ref, wd_ref, o_ref, act_ref):
    x = x_ref[...]
    h = _rms(x, gin_ref[...]).astype(BF16)
    d_ff = wd_ref.shape[1]
    for c in range(d_ff // FF_CHUNK):
        sl = slice(c * FF_CHUNK, (c + 1) * FF_CHUNK)
        g = _dot(h, wup_ref[0, :, sl])
        u = _dot(h, wup_ref[0, :, d_ff + c * FF_CHUNK:d_ff + (c + 1) * FF_CHUNK])
        act_ref[:, sl] = (g * jax.nn.sigmoid(g) * u).astype(BF16)
    y = _dot(act_ref[...], wd_ref[0])
    o_ref[...] = x + 0.5 * _rms(y, gout_ref[...])


def _ffn(x2, g_in, g_out, w_up, w_down, layer):
    n, d = x2.shape
    d_ff = w_down.shape[1]
    tm = min(ROW_TILE, n)
    row = pl.BlockSpec((tm, d), lambda i: (i, 0))
    return pl.pallas_call(
        _ffn_kernel,
        grid=(n // tm,),
        in_specs=[row, _const_spec((1, d)), _const_spec((1, d)),
                  pl.BlockSpec((1, d, 2 * d_ff), lambda i: (layer, 0, 0)),
                  pl.BlockSpec((1, d_ff, d), lambda i: (layer, 0, 0))],
        out_specs=row,
        out_shape=jax.ShapeDtypeStruct((n, d), F32),
        scratch_shapes=[pltpu.VMEM((tm, d_ff), BF16)],
        compiler_params=_params("parallel"),
        name="ffn_half_step",
    )(x2, g_in, g_out, w_up, w_down)


def _cast_kernel(w_ref, o_ref):
    o_ref[...] = w_ref[...].astype(o_ref.dtype)


def _to_bf16(w):
    depth, rows, cols = w.shape
    tr = min(rows, CAST_ROWS)
    assert rows % tr == 0
    spec = pl.BlockSpec((1, tr, cols), lambda li, ri: (li, ri, 0))
    return pl.pallas_call(
        _cast_kernel,
        grid=(depth, rows // tr),
        in_specs=[spec],
        out_specs=spec,
        out_shape=jax.ShapeDtypeStruct(w.shape, BF16),
        compiler_params=_params("parallel", "parallel"),
        name="weights_to_bf16",
    )(w)


def _inproj_kernel(x_ref, g_ref, wa_ref, wba_ref, *refs, sb_w, conv_dim, gdn_w, sb_tile, q_scale):
    q_ref, k_ref, v_ref, kt_ref, vb_ref, u_ref, z_ref, ba_ref, bat_ref = refs[-9:]
    h = _rms(x_ref[0], g_ref[...]).astype(BF16)
    tm = h.shape[0]

    def proj(lo, width):
        return _dot(h, wa_ref[:, lo:lo + width])

    q_ref[0] = (proj(0, sb_w) * q_scale).astype(BF16)
    k = proj(sb_w, sb_w)
    k_ref[0, 0] = k
    for later in range(1, k_ref.shape[0]):
        k_ref[later, 0] = jnp.zeros_like(k)
        v_ref[later, 0] = jnp.zeros_like(k)
    kt = k.T.astype(BF16)
    for hp in range(sb_w // SB_GROUP_LANES):
        for s in range(tm // sb_tile):
            kt_ref[0, hp, s] = kt[hp * SB_GROUP_LANES:(hp + 1) * SB_GROUP_LANES,
                                  s * sb_tile:(s + 1) * sb_tile]
    v = proj(2 * sb_w, sb_w)
    v_ref[0, 0] = v
    vb_ref[0] = v.astype(BF16)
    u_ref[0] = proj(3 * sb_w, conv_dim)
    z_ref[0] = proj(3 * sb_w + conv_dim, gdn_w)
    ba = _dot(h, wba_ref[...])
    ba_ref[0] = ba
    bat_ref[0] = ba.T[:8]


def _inproj(x, gain, wa, wba, kv_stacks, *, layer, depth, sb_w, conv_dim, gdn_w, sb_tile):
    b, l, d = x.shape
    tm = min(ROW_TILE, l)
    n_hp = sb_w // SB_GROUP_LANES
    kern = functools.partial(_inproj_kernel, sb_w=sb_w, conv_dim=conv_dim, gdn_w=gdn_w,
                             sb_tile=sb_tile,
                             q_scale=float((sb_w // SB_HEADS) ** -0.5))

    def rows(width):
        return pl.BlockSpec((1, tm, width), lambda bi, ti: (bi, ti, 0))

    out_shape = (
        jax.ShapeDtypeStruct((b, l, sb_w), BF16),
        jax.ShapeDtypeStruct((depth, b, l, sb_w), F32),
        jax.ShapeDtypeStruct((depth, b, l, sb_w), F32),
        jax.ShapeDtypeStruct((b, n_hp, l // sb_tile, SB_GROUP_LANES, sb_tile), BF16),
        jax.ShapeDtypeStruct((b, l, sb_w), BF16),
        jax.ShapeDtypeStruct((b, l, conv_dim), F32),
        jax.ShapeDtypeStruct((b, l, gdn_w), F32),
        jax.ShapeDtypeStruct((b, l, LANES), F32),
        jax.ShapeDtypeStruct((b, 8, l), F32),
    )
    if kv_stacks:
        layer_rows = pl.BlockSpec((1, 1, tm, sb_w), lambda bi, ti: (layer, bi, ti, 0))
    else:
        assert layer == 0
        layer_rows = pl.BlockSpec((depth, 1, tm, sb_w), lambda bi, ti: (0, bi, ti, 0))
    out_specs = (
        rows(sb_w), layer_rows, layer_rows,
        pl.BlockSpec((1, n_hp, tm // sb_tile, SB_GROUP_LANES, sb_tile), lambda bi, ti: (bi, 0, ti, 0, 0)),
        rows(sb_w), rows(conv_dim), rows(gdn_w), rows(LANES),
        pl.BlockSpec((1, 8, tm), lambda bi, ti: (bi, 0, ti)),
    )
    return pl.pallas_call(
        kern,
        grid=(b, l // tm),
        in_specs=[rows(d), _const_spec((1, d)), _const_spec(wa.shape), _const_spec(wba.shape)]
        + [pl.BlockSpec(memory_space=pl.ANY)] * len(kv_stacks),
        out_specs=out_specs,
        out_shape=out_shape,
        input_output_aliases={4 + i: 1 + i for i in range(len(kv_stacks))},
        compiler_params=_params("parallel", "parallel"),
        name="in_projection",
    )(x, gain, wa, wba, *kv_stacks)


def _sb_kernel(q_ref, ktd_ref, vd_ref, ktp_ref, vp_ref, o_ref, carry_ref, *, tq, td, nd, tk, heads, dh,
               causal_past, n_past_static, past_keys_on_lanes):
    qi = pl.program_id(2)

    def later_key_matrix(n):
        r = lax.broadcasted_iota(jnp.int32, (n, n), 0)
        c = lax.broadcasted_iota(jnp.int32, (n, n), 1)
        return (r > c).astype(BF16)

    def sweep(blocks, st):
        pairs = [(blk, hh) for blk in blocks for hh in range(heads)]
        scores = []
        for blk, hh in pairs:
            lanes = slice(hh * dh, (hh + 1) * dh)
            z = _dot(q_all[:, lanes], blk["kt"][lanes, :])
            if blk["visible"] is not None:
                z = jnp.where(blk["visible"], z, SB_MASKED_SCORE)
            nlk = jnp.maximum(z, 0.0) + jnp.log(1.0 + jnp.exp2(jnp.abs(z) * (-LOG2E)))
            scores.append((z - nlk, nlk.astype(BF16), jnp.sum(nlk, axis=1, keepdims=True)))
        csums = [_dot(nlk_b, blk["u"]) for (blk, _), (_, nlk_b, _) in zip(pairs, scores)]
        st = list(st)
        for (blk, hh), (log_beta, _, row_sum), csum in zip(pairs, scores, csums):
            lanes = slice(hh * dh, (hh + 1) * dh)
            carry, acc = st[2 * hh], st[2 * hh + 1]
            if carry is not None and blk.get("carry_bias") is not None:
                carry = carry + blk["carry_bias"]
            log_a = log_beta - csum if carry is None else log_beta - csum - carry
            a = jnp.exp(log_a).astype(BF16)
            pv = _dot_nt(a, blk["vt"][lanes, :]) if "vt" in blk else _dot(a, blk["v"][:, lanes])
            st[2 * hh + 1] = pv if acc is None else acc + pv
            st[2 * hh] = row_sum if carry is None else carry + row_sum
        return st

    q_all = q_ref[0]
    state = [None] * (2 * heads)

    u_diag = later_key_matrix(td)
    u_past = u_diag if tk == td else later_key_matrix(tk)
    rows_i = lax.broadcasted_iota(jnp.int32, (tq, td), 0)
    cols_i = lax.broadcasted_iota(jnp.int32, (tq, td), 1)
    n_past = qi * (tq // tk) if causal_past else n_past_static

    def past_block(j, carry_bias=None):
        row0 = pl.multiple_of(j * tk, tk)
        blk = dict(visible=None, u=u_past, carry_bias=carry_bias)
        if past_keys_on_lanes:
            blk["kt"] = ktp_ref[0, 0, :, :, pl.ds(row0, tk)].reshape(heads * dh, tk).astype(BF16)
            blk["vt"] = vp_ref[0, 0, :, :, pl.ds(row0, tk)].reshape(heads * dh, tk).astype(BF16)
        else:
            blk["kt"] = ktp_ref[0, 0, j]
            blk["v"] = vp_ref[0, pl.ds(row0, tk), :]
        return blk

    first = [dict(kt=ktd_ref[0, 0, sd], v=vd_ref[0, sd * td:(sd + 1) * td, :],
                  visible=cols_i + sd * td < rows_i, u=u_diag) for sd in reversed(range(nd))]
    no_past_bias = jnp.where(n_past > 0, 0.0, SB_NO_PAST_CARRY) if causal_past else None
    first.append(past_block(jnp.maximum(n_past - 1, 0), no_past_bias))
    state = sweep(first, state)

    def min_carry(st):
        m = st[0]
        for hh in range(1, heads):
            m = jnp.minimum(m, st[2 * hh])
        return jnp.min(m)

    def cond(loop):
        i, smallest, _ = loop
        return jnp.logical_and(i < n_past, smallest < SB_ZERO_WEIGHT_LOG)

    def body(loop):
        i, _, st = loop
        st = sweep([past_block(n_past - 1 - i)], st)
        return i + 1, min_carry(st), tuple(st)

    _, smallest, st = lax.while_loop(cond, body, (jnp.int32(1), min_carry(state), tuple(state)))
    o_ref[0] = jnp.concatenate([st[2 * hh + 1] for hh in range(heads)], axis=1).astype(BF16)
    carry_ref[...] = jnp.full(carry_ref.shape, smallest, F32)


def _sb_attention(q, kt_diag, v_diag, kt_past, v_past, *, tq, td, tk, causal_past, cache_window=None):
    b, l, w = q.shape
    n_g, hw = kt_diag.shape[1], kt_diag.shape[3]
    assert tq % td == 0 and (not causal_past or tq % tk == 0)
    dh = w // SB_HEADS
    if cache_window is None:
        p = v_past.shape[1]
        n_past_blocks = kt_past.shape[2]
        past_specs = [pl.BlockSpec((1, 1, n_past_blocks, hw, tk), lambda bi, hp, qi: (bi, hp, 0, 0, 0),
                                   pipeline_mode=pl.Buffered(1)),
                      pl.BlockSpec((1, p, hw), lambda bi, hp, qi: (bi, 0, hp),
                                   pipeline_mode=pl.Buffered(1))]
    else:
        layer, key0, p = cache_window
        assert n_g == 1 and key0 % p == 0 and p % tk == 0
        n_past_blocks = p // tk
        past_specs = [pl.BlockSpec((1, 1, SB_HEADS, dh, p), lambda bi, hp, qi: (layer, bi, 0, 0, key0 // p),
                                   pipeline_mode=pl.Buffered(1))] * 2
    kern = functools.partial(_sb_kernel, tq=tq, td=td, nd=tq // td, tk=tk, heads=hw // dh, dh=dh,
                             causal_past=causal_past, n_past_static=n_past_blocks,
                             past_keys_on_lanes=cache_window is not None)
    return pl.pallas_call(
        kern,
        grid=(b, n_g, l // tq),
        in_specs=[
            pl.BlockSpec((1, tq, hw), lambda bi, hp, qi: (bi, qi, hp)),
            pl.BlockSpec((1, 1, tq // td, hw, td), lambda bi, hp, qi: (bi, hp, qi, 0, 0)),
            pl.BlockSpec((1, tq, hw), lambda bi, hp, qi: (bi, qi, hp)),
        ] + past_specs,
        out_specs=(pl.BlockSpec((1, tq, hw), lambda bi, hp, qi: (bi, qi, hp)),
                   pl.BlockSpec((1, 1, 1, 8, LANES), lambda bi, hp, qi: (bi, hp, qi, 0, 0))),
        out_shape=(jax.ShapeDtypeStruct((b, l, w), BF16),
                   jax.ShapeDtypeStruct((b, n_g, l // tq, 8, LANES), F32)),
        compiler_params=_params("parallel", "parallel", "arbitrary"),
        name="stick_breaking_attention",
    )(q, kt_diag, v_diag, kt_past, v_past)


def _gdn_kernel(u_ref, z_ref, ba_ref, bat_ref, hist_ref, s0_ref, cw_ref,
                alog_l_ref, dt_l_ref, alog_s_ref, dt_s_ref, hg_ref,
                o_ref, s_ref, cnew_ref, ext_ref, *, rows, chunk, width, n_taps):
    step = pl.program_id(1)
    n_steps = pl.num_programs(1)
    dk = width // GDN_HEADS
    n_sub = rows // chunk
    pad = 8
    n_hist = n_taps - 1

    @pl.when(step == 0)
    def _():
        ext_ref[0:pad, :] = jnp.zeros((pad, ext_ref.shape[1]), F32)
        ext_ref[pad - n_hist:pad, :] = hist_ref[0]
        s_ref[...] = s0_ref[...]

    ext_ref[pad:pad + rows, :] = u_ref[0]
    y = ext_ref[pad:pad + rows, :] * cw_ref[n_hist:n_taps, :]
    for i in reversed(range(n_hist)):
        y = y + ext_ref[pad - n_hist + i:pad - n_hist + i + rows, :] * cw_ref[i:i + 1, :]
    new_tail = ext_ref[rows:rows + pad, :]
    ext_ref[0:pad, :] = new_tail

    @pl.when(step == n_steps - 1)
    def _():
        cnew_ref[0] = new_tail[pad - n_hist:pad, :]

    qkv = y * jax.nn.sigmoid(y)

    heads = range(GDN_HEADS)
    qs, ks, vs = [], [], []
    for h in heads:
        q = qkv[:, h * dk:(h + 1) * dk]
        k = qkv[:, width + h * dk:width + (h + 1) * dk]
        qs.append(q * lax.rsqrt(jnp.sum(q * q, axis=-1, keepdims=True) + NORM_EPS) * float(dk ** -0.5))
        ks.append(k * lax.rsqrt(jnp.sum(k * k, axis=-1, keepdims=True) + NORM_EPS))
        vs.append(qkv[:, 2 * width + h * dk:2 * width + (h + 1) * dk])

    r_i = lax.broadcasted_iota(jnp.int32, (chunk, chunk), 0)
    c_i = lax.broadcasted_iota(jnp.int32, (chunk, chunk), 1)
    incl = r_i >= c_i
    strict = r_i > c_i
    eye = (r_i == c_i).astype(F32)
    rr = lax.broadcasted_iota(jnp.int32, (rows, rows), 0)
    cc = lax.broadcasted_iota(jnp.int32, (rows, rows), 1)
    same_chunk = (rr // chunk) == (cc // chunk)
    lower_incl = (same_chunk & (rr >= cc)).astype(BF16)
    upper_incl = (same_chunk & (rr <= cc)).astype(BF16)

    def softplus(t):
        return jnp.maximum(t, 0.0) + jnp.log(1.0 + jnp.exp(-jnp.abs(t)))

    ba = ba_ref[0]
    g_cols = -jnp.exp(alog_l_ref[...]) * softplus(ba + dt_l_ref[...])
    gcum_cols = _dot_exactlhs_f32(lower_incl, g_cols)
    beta_cols = jax.nn.sigmoid(ba)
    bat = bat_ref[0]
    g_rows = -jnp.exp(alog_s_ref[...]) * softplus(bat + dt_s_ref[...])
    gcum_rows = _dot_f32_exactrhs(g_rows, upper_incl)

    pairs = [(ci, h) for ci in range(n_sub) for h in heads]

    pre = {}
    for ci, h in pairs:
        rs = slice(ci * chunk, (ci + 1) * chunk)
        gc_col = gcum_cols[rs, GDN_HEADS + h:GDN_HEADS + h + 1]
        gc_row = gcum_rows[GDN_HEADS + h:GDN_HEADS + h + 1, rs]
        gc_last = gc_row[:, chunk - 1:chunk]
        beta = beta_cols[rs, h:h + 1]
        gamma = jnp.where(incl, jnp.exp(gc_col - gc_row), 0.0)
        decay_in = jnp.exp(gc_col)
        q, k, v = qs[h][rs], ks[h][rs], vs[h][rs]
        kb = k * beta
        kq = _dot_nt(jnp.concatenate([kb, q], axis=0).astype(BF16), k.astype(BF16))
        pre[ci, h] = dict(
            n=jnp.where(strict, -(kq[:chunk] * gamma), 0.0),
            qk=(kq[chunk:] * gamma).astype(BF16),
            rhs=jnp.concatenate([v * beta, kb * decay_in], axis=1).astype(BF16),
            q_dec=(q * decay_in).astype(BF16),
            k_end=(k * jnp.exp(gc_last - gc_col)).astype(BF16),
            chunk_decay=jnp.exp(gc_last))

    n_rounds = max(1, (chunk - 1).bit_length())
    m_pow = {p: pre[p]["n"] for p in pairs}
    t_inv = {p: eye + pre[p]["n"] for p in pairs}
    for i in range(n_rounds):
        last = i == n_rounds - 1
        for p in pairs:
            m_bf = m_pow[p].astype(BF16)
            if i == 0:
                m_pow[p] = _dot(m_bf, m_bf)
            elif last:
                t_inv[p] = t_inv[p] + _dot(t_inv[p].astype(BF16), m_bf)
            else:
                both = _dot(jnp.concatenate([m_bf, t_inv[p].astype(BF16)], axis=0), m_bf)
                m_pow[p] = both[:chunk]
                t_inv[p] = t_inv[p] + both[chunk:]

    sol = {p: _dot(t_inv[p].astype(BF16), pre[p]["rhs"]) for p in pairs}

    state = [s_ref[0, h] for h in heads]
    o_rows = [[None] * n_sub for _ in heads]
    for ci in range(n_sub):
        ws = [_dot(jnp.concatenate([sol[ci, h][:, dk:].astype(BF16), pre[ci, h]["q_dec"]], axis=0),
                   state[h].astype(BF16)) for h in heads]
        v_new = [(sol[ci, h][:, :dk] - ws[h][:chunk]).astype(BF16) for h in heads]
        for h in heads:
            o_rows[h][ci] = ws[h][chunk:] + _dot(pre[ci, h]["qk"], v_new[h])
        state = [state[h] * pre[ci, h]["chunk_decay"] + _dot_tn(pre[ci, h]["k_end"], v_new[h])
                 for h in heads]
    for h in heads:
        s_ref[0, h] = state[h]

    z = z_ref[0]
    outs = []
    for h in heads:
        o_h = o_rows[h][0] if n_sub == 1 else jnp.concatenate(o_rows[h], axis=0)
        z_h = z[:, h * dk:(h + 1) * dk]
        outs.append(_rms(o_h, hg_ref[...]) * (z_h * jax.nn.sigmoid(z_h)))
    o_ref[0] = jnp.concatenate(outs, axis=1).astype(BF16)


def _gdn(u, z, ba, bat, hist, s0, conv_w, alog_l, dt_l, alog_s, dt_s, head_gain, *, chunk):
    b, l, conv_dim = u.shape
    width = conv_dim // 3
    dk = width // GDN_HEADS
    n_taps = conv_w.shape[0]
    rows = min(GDN_ROWS, l)
    kern = functools.partial(_gdn_kernel, rows=rows, chunk=chunk, width=width, n_taps=n_taps)

    def row_block(wd):
        return pl.BlockSpec((1, rows, wd), lambda bi, si: (bi, si, 0))

    state_spec = pl.BlockSpec((1, GDN_HEADS, dk, dk), lambda bi, si: (bi, 0, 0, 0))
    hist_spec = pl.BlockSpec((1, n_taps - 1, conv_dim), lambda bi, si: (bi, 0, 0))
    return pl.pallas_call(
        kern,
        grid=(b, l // rows),
        in_specs=[row_block(conv_dim), row_block(width), row_block(LANES),
                  pl.BlockSpec((1, 8, rows), lambda bi, si: (bi, 0, si)),
                  hist_spec, state_spec, _const_spec(conv_w.shape),
                  _const_spec((1, LANES)), _const_spec((1, LANES)),
                  _const_spec((8, 1)), _const_spec((8, 1)), _const_spec((1, dk))],
        out_specs=(row_block(width), state_spec, hist_spec),
        out_shape=(jax.ShapeDtypeStruct((b, l, width), BF16),
                   jax.ShapeDtypeStruct(s0.shape, F32),
                   jax.ShapeDtypeStruct(hist.shape, F32)),
        scratch_shapes=[pltpu.VMEM((rows + 8, conv_dim), F32)],
        compiler_params=_params("parallel", "arbitrary"),
        name="gated_delta_rule",
    )(u, z, ba, bat, hist, s0, conv_w, alog_l, dt_l, alog_s, dt_s, head_gain)


def _merge_kernel(x_ref, osb_ref, ogdn_ref, gin_ref, gout_ref, wgate_ref, wsb_ref, wgdn_ref, wout_ref, o_ref):
    x = x_ref[...]
    d = x.shape[1]
    h = _rms(x, gin_ref[...]).astype(BF16)
    gate_sb = jax.nn.sigmoid(_dot(h, wgate_ref[:, :d]))
    merged = gate_sb * _dot(osb_ref[...], wsb_ref[...])
    gate_gdn = jax.nn.sigmoid(_dot(h, wgate_ref[:, d:]))
    merged = merged + gate_gdn * _dot(ogdn_ref[...], wgdn_ref[...])
    m = _dot(merged.astype(BF16), wout_ref[...])
    o_ref[...] = x + _rms(m, gout_ref[...])


def _merge(x2, osb2, ogdn2, g_in, g_out, wgate, wsb, wgdn, wout):
    n, d = x2.shape
    tm = min(ROW_TILE, n)

    def rows(wd):
        return pl.BlockSpec((tm, wd), lambda i: (i, 0))

    return pl.pallas_call(
        _merge_kernel,
        grid=(n // tm,),
        in_specs=[rows(d), rows(osb2.shape[1]), rows(ogdn2.shape[1]),
                  _const_spec((1, d)), _const_spec((1, d)),
                  _const_spec(wgate.shape), _const_spec(wsb.shape), _const_spec(wgdn.shape),
                  _const_spec(wout.shape)],
        out_specs=rows(d),
        out_shape=jax.ShapeDtypeStruct((n, d), F32),
        compiler_params=_params("parallel"),
        name="branch_merge",
    )(x2, osb2, ogdn2, g_in, g_out, wgate, wsb, wgdn, wout)


def _layer_weights(l, norm_gains, w_in_bf, conv_w, gdn_a_log, gdn_dt_bias, gdn_norm_gain,
                   w_branch_sb, w_branch_gdn, w_out):
    d = w_in_bf.shape[1]
    sb_w = w_branch_sb.shape[1]
    gdn_w = w_branch_gdn.shape[1]
    conv_dim = conv_w.shape[2]
    n_a = 3 * sb_w + conv_dim + gdn_w
    wi = w_in_bf[l]
    wba = jnp.zeros((d, LANES), BF16).at[:, :2 * GDN_HEADS].set(wi[:, n_a:n_a + 2 * GDN_HEADS])

    def lane_vec(p):
        return jnp.zeros((1, LANES), F32).at[0, GDN_HEADS:2 * GDN_HEADS].set(p)

    def sublane_vec(p):
        return jnp.zeros((8, 1), F32).at[GDN_HEADS:2 * GDN_HEADS, 0].set(p)

    return dict(
        gains=[norm_gains[l, i][None, :] for i in range(6)],
        wa=wi[:, :n_a],
        wba=wba,
        wgate=wi[:, n_a + 2 * GDN_HEADS:],
        wsb=w_branch_sb[l].astype(BF16), wgdn=w_branch_gdn[l].astype(BF16), wout=w_out[l].astype(BF16),
        conv_w=conv_w[l],
        alog_l=lane_vec(gdn_a_log[l]), dt_l=lane_vec(gdn_dt_bias[l]),
        alog_s=sublane_vec(gdn_a_log[l]), dt_s=sublane_vec(gdn_dt_bias[l]),
        head_gain=gdn_norm_gain[l][None, :],
        dims=(sb_w, conv_dim, gdn_w),
    )


def _run_group(x, weights, ffn_w, past_k, past_v, conv_hist, s0, chunk):
    b, l, d = x.shape
    n = b * l
    s_list, c_list = [], []
    kv_stacks = ()
    for li, w in enumerate(weights):
        sb_w, conv_dim, gdn_w = w["dims"]
        dh = sb_w // SB_HEADS
        tq = min(SB_QUERY_TILE, l)
        td = min(SB_KEY_TILE, l)
        g = w["gains"]
        x2 = _ffn(x.reshape(n, d), g[0], g[1], ffn_w[0], ffn_w[1], li)
        q, k_stack, v_stack, kt, vb, u, z, ba, bat = _inproj(
            x2.reshape(b, l, d), g[2], w["wa"], w["wba"], kv_stacks,
            layer=li, depth=len(weights), sb_w=sb_w, conv_dim=conv_dim, gdn_w=gdn_w, sb_tile=td)
        kv_stacks = (k_stack, v_stack)
        if past_k is None:
            o_sb, _ = _sb_attention(q, kt, vb, kt, vb, tq=tq, td=td, tk=td, causal_past=True)
        else:
            p = past_k.shape[2]
            tk = min(SB_KEY_TILE, p)

            kt_cache = jnp.transpose(past_k, (0, 1, 3, 4, 2))
            vt_cache = jnp.transpose(past_v, (0, 1, 3, 4, 2))

            def attend(key0, n_keys):
                return _sb_attention(q, kt, vb, kt_cache, vt_cache, tq=tq, td=td, tk=tk,
                                     causal_past=False, cache_window=(li, key0, n_keys))

            o_sb, carry_min = attend(p - tk, tk)
            if p > tk:
                o_sb = lax.cond(jnp.min(carry_min) >= SB_ZERO_WEIGHT_LOG, lambda: o_sb,
                                lambda: attend(0, p)[0])
        o_gdn, s_fin, conv_new = _gdn(u, z, ba, bat, conv_hist[li], s0[li], w["conv_w"],
                                      w["alog_l"], w["dt_l"], w["alog_s"], w["dt_s"], w["head_gain"],
                                      chunk=chunk)
        x2 = _merge(x2, o_sb.reshape(n, sb_w), o_gdn.reshape(n, gdn_w), g[2], g[3],
                    w["wgate"], w["wsb"], w["wgdn"], w["wout"])
        x2 = _ffn(x2, g[4], g[5], ffn_w[2], ffn_w[3], li)
        x = x2.reshape(b, l, d)
        s_list.append(s_fin)
        c_list.append(conv_new)
    k_all, v_all = (t.reshape(len(weights), b, l, SB_HEADS, dh) for t in kv_stacks)
    return x, k_all, v_all, jnp.stack(s_list), jnp.stack(c_list)


def kernel(x_prompt, x_sample, cache_sb_k, cache_sb_v, state_gdn, state_conv, norm_gains,
           w_ffn1_up, w_ffn1_down, w_in, conv_w, gdn_a_log, gdn_dt_bias, gdn_norm_gain,
           w_branch_sb, w_branch_gdn, w_out, w_ffn2_up, w_ffn2_down):
    depth = w_in.shape[0]
    ffn_w = tuple(_to_bf16(w) for w in (w_ffn1_up, w_ffn1_down, w_ffn2_up, w_ffn2_down))
    w_in_bf = w_in.astype(BF16)
    weights = [_layer_weights(l, norm_gains, w_in_bf, conv_w, gdn_a_log, gdn_dt_bias, gdn_norm_gain,
                              w_branch_sb, w_branch_gdn, w_out) for l in range(depth)]
    bp = x_prompt.shape[0]
    chunk_prompt = 64
    zero_conv = jnp.zeros((depth, bp) + state_conv.shape[2:], state_conv.dtype)
    zero_state = jnp.zeros((depth, bp) + state_gdn.shape[2:], state_gdn.dtype)
    y_p, pk, pv, ps, pc = _run_group(x_prompt, weights, ffn_w, None, None, zero_conv, zero_state, chunk_prompt)
    y_s, sk, sv, ss, sc = _run_group(x_sample, weights, ffn_w, cache_sb_k, cache_sb_v, state_conv, state_gdn,
                                     x_sample.shape[1])
    return (y_p, y_s, pk, pv, ps, pc, sk, sv, ss, sc)
```

```python
import functools

import jax
import jax.numpy as jnp
from jax import lax
from jax.experimental import pallas as pl
from jax.experimental.pallas import tpu as pltpu

F32 = jnp.float32
BF16 = jnp.bfloat16

NORM_EPS = 1e-6
LOG2E = 1.4426950408889634
SB_HEADS = 8
GDN_HEADS = 4
LANES = 128
VMEM_LIMIT_BYTES = 56 * 1024 * 1024

ROW_TILE = 512
CAST_ROWS = 256
FF_CHUNK = 256
GDN_ROWS = 512
SB_KEY_TILE = 256
SB_QUERY_TILE = 256
SB_GROUP_LANES = 512
SB_ZERO_WEIGHT_LOG = 110.0
SB_MASKED_SCORE = -1e30
SB_NO_PAST_CARRY = 1e30


def _params(*sem):
    return pltpu.CompilerParams(dimension_semantics=sem, vmem_limit_bytes=VMEM_LIMIT_BYTES)


def _const_spec(shape):
    zeros = (0,) * len(shape)
    return pl.BlockSpec(shape, lambda *_: zeros)


def _rms(x, gain):
    ms = jnp.mean(x * x, axis=-1, keepdims=True)
    return x * lax.rsqrt(ms + NORM_EPS) * gain


def _dot(a, b):
    return jnp.dot(a, b, preferred_element_type=F32)


def _dot_nt(a, b):
    return lax.dot_general(a, b, (((1,), (1,)), ((), ())), preferred_element_type=F32)


def _dot_tn(a, b):
    return lax.dot_general(a, b, (((0,), (0,)), ((), ())), preferred_element_type=F32)


def _split3(x):
    hi = x.astype(BF16)
    r = x - hi.astype(F32)
    mid = r.astype(BF16)
    lo = (r - mid.astype(F32)).astype(BF16)
    return hi, mid, lo


def _dot_f32_exactrhs(a, b_bf16):
    hi, mid, lo = _split3(a)
    return _dot(hi, b_bf16) + _dot(mid, b_bf16) + _dot(lo, b_bf16)


def _dot_exactlhs_f32(a_bf16, b):
    hi, mid, lo = _split3(b)
    return _dot(a_bf16, hi) + _dot(a_bf16, mid) + _dot(a_bf16, lo)


def _ffn_kernel(x_ref, gin_ref, gout_ref, wup_ref, wd_ref, o_ref, act_ref):
    x = x_ref[...]
    h = _rms(x, gin_ref[...]).astype(BF16)
    d_ff = wd_ref.shape[1]
    for c in range(d_ff // FF_CHUNK):
        sl = slice(c * FF_CHUNK, (c + 1) * FF_CHUNK)
        g = _dot(h, wup_ref[0, :, sl])
        u = _dot(h, wup_ref[0, :, d_ff + c * FF_CHUNK:d_ff + (c + 1) * FF_CHUNK])
        act_ref[:, sl] = (g * jax.nn.sigmoid(g) * u).astype(BF16)
    y = _dot(act_ref[...], wd_ref[0])
    o_ref[...] = x + 0.5 * _rms(y, gout_ref[...])


def _ffn(x2, g_in, g_out, w_up, w_down, layer):
    n, d = x2.shape
    d_ff = w_down.shape[1]
    tm = min(ROW_TILE, n)
    row = pl.BlockSpec((tm, d), lambda i: (i, 0))
    return pl.pallas_call(
        _ffn_kernel,
        grid=(n // tm,),
        in_specs=[row, _const_spec((1, d)), _const_spec((1, d)),
                  pl.BlockSpec((1, d, 2 * d_ff), lambda i: (layer, 0, 0)),
                  pl.BlockSpec((1, d_ff, d), lambda i: (layer, 0, 0))],
        out_specs=row,
        out_shape=jax.ShapeDtypeStruct((n, d), F32),
        scratch_shapes=[pltpu.VMEM((tm, d_ff), BF16)],
        compiler_params=_params("parallel"),
        name="ffn_half_step",
    )(x2, g_in, g_out, w_up, w_down)


def _cast_kernel(w_ref, o_ref):
    o_ref[...] = w_ref[...].astype(o_ref.dtype)


def _to_bf16(w):
    depth, rows, cols = w.shape
    tr = min(rows, CAST_ROWS)
    assert rows % tr == 0
    spec = pl.BlockSpec((1, tr, cols), lambda li, ri: (li, ri, 0))
    return pl.pallas_call(
        _cast_kernel,
        grid=(depth, rows // tr),
        in_specs=[spec],
        out_specs=spec,
        out_shape=jax.ShapeDtypeStruct(w.shape, BF16),
        compiler_params=_params("parallel", "parallel"),
        name="weights_to_bf16",
    )(w)


def _inproj_kernel(x_ref, g_ref, wa_ref, wba_ref, *refs, sb_w, conv_dim, gdn_w, sb_tile, q_scale):
    q_ref, k_ref, v_ref, kt_ref, vb_ref, u_ref, z_ref, ba_ref, bat_ref = refs[-9:]
    h = _rms(x_ref[0], g_ref[...]).astype(BF16)
    tm = h.shape[0]

    def proj(lo, width):
        return _dot(h, wa_ref[:, lo:lo + width])

    q_ref[0] = (proj(0, sb_w) * q_scale).astype(BF16)
    k = proj(sb_w, sb_w)
    k_ref[0, 0] = k
    for later in range(1, k_ref.shape[0]):
        k_ref[later, 0] = jnp.zeros_like(k)
        v_ref[later, 0] = jnp.zeros_like(k)
    kt = k.T.astype(BF16)
    for hp in range(sb_w // SB_GROUP_LANES):
        for s in range(tm // sb_tile):
            kt_ref[0, hp, s] = kt[hp * SB_GROUP_LANES:(hp + 1) * SB_GROUP_LANES,
                                  s * sb_tile:(s + 1) * sb_tile]
    v = proj(2 * sb_w, sb_w)
    v_ref[0, 0] = v
    vb_ref[0] = v.astype(BF16)
    u_ref[0] = proj(3 * sb_w, conv_dim)
    z_ref[0] = proj(3 * sb_w + conv_dim, gdn_w)
    ba = _dot(h, wba_ref[...])
    ba_ref[0] = ba
    bat_ref[0] = ba.T[:8]


def _inproj(x, gain, wa, wba, kv_stacks, *, layer, depth, sb_w, conv_dim, gdn_w, sb_tile):
    b, l, d = x.shape
    tm = min(ROW_TILE, l)
    n_hp = sb_w // SB_GROUP_LANES
    kern = functools.partial(_inproj_kernel, sb_w=sb_w, conv_dim=conv_dim, gdn_w=gdn_w,
                             sb_tile=sb_tile,
                             q_scale=float((sb_w // SB_HEADS) ** -0.5))

    def rows(width):
        return pl.BlockSpec((1, tm, width), lambda bi, ti: (bi, ti, 0))

    out_shape = (
        jax.ShapeDtypeStruct((b, l, sb_w), BF16),
        jax.ShapeDtypeStruct((depth, b, l, sb_w), F32),
        jax.ShapeDtypeStruct((depth, b, l, sb_w), F32),
        jax.ShapeDtypeStruct((b, n_hp, l // sb_tile, SB_GROUP_LANES, sb_tile), BF16),
        jax.ShapeDtypeStruct((b, l, sb_w), BF16),
        jax.ShapeDtypeStruct((b, l, conv_dim), F32),
        jax.ShapeDtypeStruct((b, l, gdn_w), F32),
        jax.ShapeDtypeStruct((b, l, LANES), F32),
        jax.ShapeDtypeStruct((b, 8, l), F32),
    )
    if kv_stacks:
        layer_rows = pl.BlockSpec((1, 1, tm, sb_w), lambda bi, ti: (layer, bi, ti, 0))
    else:
        assert layer == 0
        layer_rows = pl.BlockSpec((depth, 1, tm, sb_w), lambda bi, ti: (0, bi, ti, 0))
    out_specs = (
        rows(sb_w), layer_rows, layer_rows,
        pl.BlockSpec((1, n_hp, tm // sb_tile, SB_GROUP_LANES, sb_tile), lambda bi, ti: (bi, 0, ti, 0, 0)),
        rows(sb_w), rows(conv_dim), rows(gdn_w), rows(LANES),
        pl.BlockSpec((1, 8, tm), lambda bi, ti: (bi, 0, ti)),
    )
    return pl.pallas_call(
        kern,
        grid=(b, l // tm),
        in_specs=[rows(d), _const_spec((1, d)), _const_spec(wa.shape), _const_spec(wba.shape)]
        + [pl.BlockSpec(memory_space=pl.ANY)] * len(kv_stacks),
        out_specs=out_specs,
        out_shape=out_shape,
        input_output_aliases={4 + i: 1 + i for i in range(len(kv_stacks))},
        compiler_params=_params("parallel", "parallel"),
        name="in_projection",
    )(x, gain, wa, wba, *kv_stacks)


def _sb_kernel(q_ref, ktd_ref, vd_ref, ktp_ref, vp_ref, o_ref, carry_ref, *, tq, td, nd, tk, heads, dh,
               causal_past, n_past_static, past_keys_on_lanes):
    qi = pl.program_id(2)

    def later_key_matrix(n):
        r = lax.broadcasted_iota(jnp.int32, (n, n), 0)
        c = lax.broadcasted_iota(jnp.int32, (n, n), 1)
        return (r > c).astype(BF16)

    def sweep(blocks, st):
        pairs = [(blk, hh) for blk in blocks for hh in range(heads)]
        scores = []
        for blk, hh in pairs:
            lanes = slice(hh * dh, (hh + 1) * dh)
            z = _dot(q_all[:, lanes], blk["kt"][lanes, :])
            if blk["visible"] is not None:
                z = jnp.where(blk["visible"], z, SB_MASKED_SCORE)
            nlk = jnp.maximum(z, 0.0) + jnp.log(1.0 + jnp.exp2(jnp.abs(z) * (-LOG2E)))
            scores.append((z - nlk, nlk.astype(BF16), jnp.sum(nlk, axis=1, keepdims=True)))
        csums = [_dot(nlk_b, blk["u"]) for (blk, _), (_, nlk_b, _) in zip(pairs, scores)]
        st = list(st)
        for (blk, hh), (log_beta, _, row_sum), csum in zip(pairs, scores, csums):
            lanes = slice(hh * dh, (hh + 1) * dh)
            carry, acc = st[2 * hh], st[2 * hh + 1]
            if carry is not None and blk.get("carry_bias") is not None:
                carry = carry + blk["carry_bias"]
            log_a = log_beta - csum if carry is None else log_beta - csum - carry
            a = jnp.exp(log_a).astype(BF16)
            pv = _dot_nt(a, blk["vt"][lanes, :]) if "vt" in blk else _dot(a, blk["v"][:, lanes])
            st[2 * hh + 1] = pv if acc is None else acc + pv
            st[2 * hh] = row_sum if carry is None else carry + row_sum
        return st

    q_all = q_ref[0]
    state = [None] * (2 * heads)

    u_diag = later_key_matrix(td)
    u_past = u_diag if tk == td else later_key_matrix(tk)
    rows_i = lax.broadcasted_iota(jnp.int32, (tq, td), 0)
    cols_i = lax.broadcasted_iota(jnp.int32, (tq, td), 1)
    n_past = qi * (tq // tk) if causal_past else n_past_static

    def past_block(j, carry_bias=None):
        row0 = pl.multiple_of(j * tk, tk)
        blk = dict(visible=None, u=u_past, carry_bias=carry_bias)
        if past_keys_on_lanes:
            blk["kt"] = ktp_ref[0, 0, :, :, pl.ds(row0, tk)].reshape(heads * dh, tk).astype(BF16)
            blk["vt"] = vp_ref[0, 0, :, :, pl.ds(row0, tk)].reshape(heads * dh, tk).astype(BF16)
        else:
            blk["kt"] = ktp_ref[0, 0, j]
            blk["v"] = vp_ref[0, pl.ds(row0, tk), :]
        return blk

    first = [dict(kt=ktd_ref[0, 0, sd], v=vd_ref[0, sd * td:(sd + 1) * td, :],
                  visible=cols_i + sd * td < rows_i, u=u_diag) for sd in reversed(range(nd))]
    no_past_bias = jnp.where(n_past > 0, 0.0, SB_NO_PAST_CARRY) if causal_past else None
    first.append(past_block(jnp.maximum(n_past - 1, 0), no_past_bias))
    state = sweep(first, state)

    def min_carry(st):
        m = st[0]
        for hh in range(1, heads):
            m = jnp.minimum(m, st[2 * hh])
        return jnp.min(m)

    def cond(loop):
        i, smallest, _ = loop
        return jnp.logical_and(i < n_past, smallest < SB_ZERO_WEIGHT_LOG)

    def body(loop):
        i, _, st = loop
        st = sweep([past_block(n_past - 1 - i)], st)
        return i + 1, min_carry(st), tuple(st)

    _, smallest, st = lax.while_loop(cond, body, (jnp.int32(1), min_carry(state), tuple(state)))
    o_ref[0] = jnp.concatenate([st[2 * hh + 1] for hh in range(heads)], axis=1).astype(BF16)
    carry_ref[...] = jnp.full(carry_ref.shape, smallest, F32)


def _sb_attention(q, kt_diag, v_diag, kt_past, v_past, *, tq, td, tk, causal_past, cache_window=None):
    b, l, w = q.shape
    n_g, hw = kt_diag.shape[1], kt_diag.shape[3]
    assert tq % td == 0 and (not causal_past or tq % tk == 0)
    dh = w // SB_HEADS
    if cache_window is None:
        p = v_past.shape[1]
        n_past_blocks = kt_past.shape[2]
        past_specs = [pl.BlockSpec((1, 1, n_past_blocks, hw, tk), lambda bi, hp, qi: (bi, hp, 0, 0, 0),
                                   pipeline_mode=pl.Buffered(1)),
                      pl.BlockSpec((1, p, hw), lambda bi, hp, qi: (bi, 0, hp),
                                   pipeline_mode=pl.Buffered(1))]
    else:
        layer, key0, p = cache_window
        assert n_g == 1 and key0 % p == 0 and p % tk == 0
        n_past_blocks = p // tk
        past_specs = [pl.BlockSpec((1, 1, SB_HEADS, dh, p), lambda bi, hp, qi: (layer, bi, 0, 0, key0 // p),
                                   pipeline_mode=pl.Buffered(1))] * 2
    kern = functools.partial(_sb_kernel, tq=tq, td=td, nd=tq // td, tk=tk, heads=hw // dh, dh=dh,
                             causal_past=causal_past, n_past_static=n_past_blocks,
                             past_keys_on_lanes=cache_window is not None)
    return pl.pallas_call(
        kern,
        grid=(b, n_g, l // tq),
        in_specs=[
            pl.BlockSpec((1, tq, hw), lambda bi, hp, qi: (bi, qi, hp)),
            pl.BlockSpec((1, 1, tq // td, hw, td), lambda bi, hp, qi: (bi, hp, qi, 0, 0)),
            pl.BlockSpec((1, tq, hw), lambda bi, hp, qi: (bi, qi, hp)),
        ] + past_specs,
        out_specs=(pl.BlockSpec((1, tq, hw), lambda bi, hp, qi: (bi, qi, hp)),
                   pl.BlockSpec((1, 1, 1, 8, LANES), lambda bi, hp, qi: (bi, hp, qi, 0, 0))),
        out_shape=(jax.ShapeDtypeStruct((b, l, w), BF16),
                   jax.ShapeDtypeStruct((b, n_g, l // tq, 8, LANES), F32)),
        compiler_params=_params("parallel", "parallel", "arbitrary"),
        name="stick_breaking_attention",
    )(q, kt_diag, v_diag, kt_past, v_past)


def _gdn_kernel(u_ref, z_ref, ba_ref, bat_ref, hist_ref, s0_ref, cw_ref,
                alog_l_ref, dt_l_ref, alog_s_ref, dt_s_ref, hg_ref,
                o_ref, s_ref, cnew_ref, ext_ref, *, rows, chunk, width, n_taps):
    step = pl.program_id(1)
    n_steps = pl.num_programs(1)
    dk = width // GDN_HEADS
    n_sub = rows // chunk
    pad = 8
    n_hist = n_taps - 1

    @pl.when(step == 0)
    def _():
        ext_ref[0:pad, :] = jnp.zeros((pad, ext_ref.shape[1]), F32)
        ext_ref[pad - n_hist:pad, :] = hist_ref[0]
        s_ref[...] = s0_ref[...]

    ext_ref[pad:pad + rows, :] = u_ref[0]
    y = ext_ref[pad:pad + rows, :] * cw_ref[n_hist:n_taps, :]
    for i in reversed(range(n_hist)):
        y = y + ext_ref[pad - n_hist + i:pad - n_hist + i + rows, :] * cw_ref[i:i + 1, :]
    new_tail = ext_ref[rows:rows + pad, :]
    ext_ref[0:pad, :] = new_tail

    @pl.when(step == n_steps - 1)
    def _():
        cnew_ref[0] = new_tail[pad - n_hist:pad, :]

    qkv = y * jax.nn.sigmoid(y)

    heads = range(GDN_HEADS)
    qs, ks, vs = [], [], []
    for h in heads:
        q = qkv[:, h * dk:(h + 1) * dk]
        k = qkv[:, width + h * dk:width + (h + 1) * dk]
        qs.append(q * lax.rsqrt(jnp.sum(q * q, axis=-1, keepdims=True) + NORM_EPS) * float(dk ** -0.5))
        ks.append(k * lax.rsqrt(jnp.sum(k * k, axis=-1, keepdims=True) + NORM_EPS))
        vs.append(qkv[:, 2 * width + h * dk:2 * width + (h + 1) * dk])

    r_i = lax.broadcasted_iota(jnp.int32, (chunk, chunk), 0)
    c_i = lax.broadcasted_iota(jnp.int32, (chunk, chunk), 1)
    incl = r_i >= c_i
    strict = r_i > c_i
    eye = (r_i == c_i).astype(F32)
    rr = lax.broadcasted_iota(jnp.int32, (rows, rows), 0)
    cc = lax.broadcasted_iota(jnp.int32, (rows, rows), 1)
    same_chunk = (rr // chunk) == (cc // chunk)
    lower_incl = (same_chunk & (rr >= cc)).astype(BF16)
    upper_incl = (same_chunk & (rr <= cc)).astype(BF16)

    def softplus(t):
        return jnp.maximum(t, 0.0) + jnp.log(1.0 + jnp.exp(-jnp.abs(t)))

    ba = ba_ref[0]
    g_cols = -jnp.exp(alog_l_ref[...]) * softplus(ba + dt_l_ref[...])
    gcum_cols = _dot_exactlhs_f32(lower_incl, g_cols)
    beta_cols = jax.nn.sigmoid(ba)
    bat = bat_ref[0]
    g_rows = -jnp.exp(alog_s_ref[...]) * softplus(bat + dt_s_ref[...])
    gcum_rows = _dot_f32_exactrhs(g_rows, upper_incl)

    pairs = [(ci, h) for ci in range(n_sub) for h in heads]

    pre = {}
    for ci, h in pairs:
        rs = slice(ci * chunk, (ci + 1) * chunk)
        gc_col = gcum_cols[rs, GDN_HEADS + h:GDN_HEADS + h + 1]
        gc_row = gcum_rows[GDN_HEADS + h:GDN_HEADS + h + 1, rs]
        gc_last = gc_row[:, chunk - 1:chunk]
        beta = beta_cols[rs, h:h + 1]
        gamma = jnp.where(incl, jnp.exp(gc_col - gc_row), 0.0)
        decay_in = jnp.exp(gc_col)
        q, k, v = qs[h][rs], ks[h][rs], vs[h][rs]
        kb = k * beta
        kq = _dot_nt(jnp.concatenate([kb, q], axis=0).astype(BF16), k.astype(BF16))
        pre[ci, h] = dict(
            n=jnp.where(strict, -(kq[:chunk] * gamma), 0.0),
            qk=(kq[chunk:] * gamma).astype(BF16),
            rhs=jnp.concatenate([v * beta, kb * decay_in], axis=1).astype(BF16),
            q_dec=(q * decay_in).astype(BF16),
            k_end=(k * jnp.exp(gc_last - gc_col)).astype(BF16),
            chunk_decay=jnp.exp(gc_last))

    n_rounds = max(1, (chunk - 1).bit_length())
    m_pow = {p: pre[p]["n"] for p in pairs}
    t_inv = {p: eye + pre[p]["n"] for p in pairs}
    for i in range(n_rounds):
        last = i == n_rounds - 1
        for p in pairs:
            m_bf = m_pow[p].astype(BF16)
            if i == 0:
                m_pow[p] = _dot(m_bf, m_bf)
            elif last:
                t_inv[p] = t_inv[p] + _dot(t_inv[p].astype(BF16), m_bf)
            else:
                both = _dot(jnp.concatenate([m_bf, t_inv[p].astype(BF16)], axis=0), m_bf)
                m_pow[p] = both[:chunk]
                t_inv[p] = t_inv[p] + both[chunk:]

    sol = {p: _dot(t_inv[p].astype(BF16), pre[p]["rhs"]) for p in pairs}

    state = [s_ref[0, h] for h in heads]
    o_rows = [[None] * n_sub for _ in heads]
    for ci in range(n_sub):
        ws = [_dot(jnp.concatenate([sol[ci, h][:, dk:].astype(BF16), pre[ci, h]["q_dec"]], axis=0),
                   state[h].astype(BF16)) for h in heads]
        v_new = [(sol[ci, h][:, :dk] - ws[h][:chunk]).astype(BF16) for h in heads]
        for h in heads:
            o_rows[h][ci] = ws[h][chunk:] + _dot(pre[ci, h]["qk"], v_new[h])
        state = [state[h] * pre[ci, h]["chunk_decay"] + _dot_tn(pre[ci, h]["k_end"], v_new[h])
                 for h in heads]
    for h in heads:
        s_ref[0, h] = state[h]

    z = z_ref[0]
    outs = []
    for h in heads:
        o_h = o_rows[h][0] if n_sub == 1 else jnp.concatenate(o_rows[h], axis=0)
        z_h = z[:, h * dk:(h + 1) * dk]
        outs.append(_rms(o_h, hg_ref[...]) * (z_h * jax.nn.sigmoid(z_h)))
    o_ref[0] = jnp.concatenate(outs, axis=1).astype(BF16)


def _gdn(u, z, ba, bat, hist, s0, conv_w, alog_l, dt_l, alog_s, dt_s, head_gain, *, chunk):
    b, l, conv_dim = u.shape
    width = conv_dim // 3
    dk = width // GDN_HEADS
    n_taps = conv_w.shape[0]
    rows = min(GDN_ROWS, l)
    kern = functools.partial(_gdn_kernel, rows=rows, chunk=chunk, width=width, n_taps=n_taps)

    def row_block(wd):
        return pl.BlockSpec((1, rows, wd), lambda bi, si: (bi, si, 0))

    state_spec = pl.BlockSpec((1, GDN_HEADS, dk, dk), lambda bi, si: (bi, 0, 0, 0))
    hist_spec = pl.BlockSpec((1, n_taps - 1, conv_dim), lambda bi, si: (bi, 0, 0))
    return pl.pallas_call(
        kern,
        grid=(b, l // rows),
        in_specs=[row_block(conv_dim), row_block(width), row_block(LANES),
                  pl.BlockSpec((1, 8, rows), lambda bi, si: (bi, 0, si)),
                  hist_spec, state_spec, _const_spec(conv_w.shape),
                  _const_spec((1, LANES)), _const_spec((1, LANES)),
                  _const_spec((8, 1)), _const_spec((8, 1)), _const_spec((1, dk))],
        out_specs=(row_block(width), state_spec, hist_spec),
        out_shape=(jax.ShapeDtypeStruct((b, l, width), BF16),
                   jax.ShapeDtypeStruct(s0.shape, F32),
                   jax.ShapeDtypeStruct(hist.shape, F32)),
        scratch_shapes=[pltpu.VMEM((rows + 8, conv_dim), F32)],
        compiler_params=_params("parallel", "arbitrary"),
        name="gated_delta_rule",
    )(u, z, ba, bat, hist, s0, conv_w, alog_l, dt_l, alog_s, dt_s, head_gain)


def _merge_kernel(x_ref, osb_ref, ogdn_ref, gin_ref, gout_ref, wgate_ref, wsb_ref, wgdn_ref, wout_ref, o_ref):
    x = x_ref[...]
    d = x.shape[1]
    h = _rms(x, gin_ref[...]).astype(BF16)
    gate_sb = jax.nn.sigmoid(_dot(h, wgate_ref[:, :d]))
    merged = gate_sb * _dot(osb_ref[...], wsb_ref[...])
    gate_gdn = jax.nn.sigmoid(_dot(h, wgate_ref[:, d:]))
    merged = merged + gate_gdn * _dot(ogdn_ref[...], wgdn_ref[...])
    m = _dot(merged.astype(BF16), wout_ref[...])
    o_ref[...] = x + _rms(m, gout_ref[...])


def _merge(x2, osb2, ogdn2, g_in, g_out, wgate, wsb, wgdn, wout):
    n, d = x2.shape
    tm = min(ROW_TILE, n)

    def rows(wd):
        return pl.BlockSpec((tm, wd), lambda i: (i, 0))

    return pl.pallas_call(
        _merge_kernel,
        grid=(n // tm,),
        in_specs=[rows(d), rows(osb2.shape[1]), rows(ogdn2.shape[1]),
                  _const_spec((1, d)), _const_spec((1, d)),
                  _const_spec(wgate.shape), _const_spec(wsb.shape), _const_spec(wgdn.shape),
                  _const_spec(wout.shape)],
        out_specs=rows(d),
        out_shape=jax.ShapeDtypeStruct((n, d), F32),
        compiler_params=_params("parallel"),
        name="branch_merge",
    )(x2, osb2, ogdn2, g_in, g_out, wgate, wsb, wgdn, wout)


def _layer_weights(l, norm_gains, w_in_bf, conv_w, gdn_a_log, gdn_dt_bias, gdn_norm_gain,
                   w_branch_sb, w_branch_gdn, w_out):
    d = w_in_bf.shape[1]
    sb_w = w_branch_sb.shape[1]
    gdn_w = w_branch_gdn.shape[1]
    conv_dim = conv_w.shape[2]
    n_a = 3 * sb_w + conv_dim + gdn_w
    wi = w_in_bf[l]
    wba = jnp.zeros((d, LANES), BF16).at[:, :2 * GDN_HEADS].set(wi[:, n_a:n_a + 2 * GDN_HEADS])

    def lane_vec(p):
        return jnp.zeros((1, LANES), F32).at[0, GDN_HEADS:2 * GDN_HEADS].set(p)

    def sublane_vec(p):
        return jnp.zeros((8, 1), F32).at[GDN_HEADS:2 * GDN_HEADS, 0].set(p)

    return dict(
        gains=[norm_gains[l, i][None, :] for i in range(6)],
        wa=wi[:, :n_a],
        wba=wba,
        wgate=wi[:, n_a + 2 * GDN_HEADS:],
        wsb=w_branch_sb[l].astype(BF16), wgdn=w_branch_gdn[l].astype(BF16), wout=w_out[l].astype(BF16),
        conv_w=conv_w[l],
        alog_l=lane_vec(gdn_a_log[l]), dt_l=lane_vec(gdn_dt_bias[l]),
        alog_s=sublane_vec(gdn_a_log[l]), dt_s=sublane_vec(gdn_dt_bias[l]),
        head_gain=gdn_norm_gain[l][None, :],
        dims=(sb_w, conv_dim, gdn_w),
    )


def _run_group(x, weights, ffn_w, past_k, past_v, conv_hist, s0, chunk):
    b, l, d = x.shape
    n = b * l
    s_list, c_list = [], []
    kv_stacks = ()
    for li, w in enumerate(weights):
        sb_w, conv_dim, gdn_w = w["dims"]
        dh = sb_w // SB_HEADS
        tq = min(SB_QUERY_TILE, l)
        td = min(SB_KEY_TILE, l)
        g = w["gains"]
        x2 = _ffn(x.reshape(n, d), g[0], g[1], ffn_w[0], ffn_w[1], li)
        q, k_stack, v_stack, kt, vb, u, z, ba, bat = _inproj(
            x2.reshape(b, l, d), g[2], w["wa"], w["wba"], kv_stacks,
            layer=li, depth=len(weights), sb_w=sb_w, conv_dim=conv_dim, gdn_w=gdn_w, sb_tile=td)
        kv_stacks = (k_stack, v_stack)
        if past_k is None:
            o_sb, _ = _sb_attention(q, kt, vb, kt, vb, tq=tq, td=td, tk=td, causal_past=True)
        else:
            p = past_k.shape[2]
            tk = min(SB_KEY_TILE, p)

            kt_cache = jnp.transpose(past_k, (0, 1, 3, 4, 2))
            vt_cache = jnp.transpose(past_v, (0, 1, 3, 4, 2))

            def attend(key0, n_keys):
                return _sb_attention(q, kt, vb, kt_cache, vt_cache, tq=tq, td=td, tk=tk,
                                     causal_past=False, cache_window=(li, key0, n_keys))

            o_sb, carry_min = attend(p - tk, tk)
            if p > tk:
                o_sb = lax.cond(jnp.min(carry_min) >= SB_ZERO_WEIGHT_LOG, lambda: o_sb,
                                lambda: attend(0, p)[0])
        o_gdn, s_fin, conv_new = _gdn(u, z, ba, bat, conv_hist[li], s0[li], w["conv_w"],
                                      w["alog_l"], w["dt_l"], w["alog_s"], w["dt_s"], w["head_gain"],
                                      chunk=chunk)
        x2 = _merge(x2, o_sb.reshape(n, sb_w), o_gdn.reshape(n, gdn_w), g[2], g[3],
                    w["wgate"], w["wsb"], w["wgdn"], w["wout"])
        x2 = _ffn(x2, g[4], g[5], ffn_w[2], ffn_w[3], li)
        x = x2.reshape(b, l, d)
        s_list.append(s_fin)
        c_list.append(conv_new)
    k_all, v_all = (t.reshape(len(weights), b, l, SB_HEADS, dh) for t in kv_stacks)
    return x, k_all, v_all, jnp.stack(s_list), jnp.stack(c_list)


def kernel(x_prompt, x_sample, cache_sb_k, cache_sb_v, state_gdn, state_conv, norm_gains,
           w_ffn1_up, w_ffn1_down, w_in, conv_w, gdn_a_log, gdn_dt_bias, gdn_norm_gain,
           w_branch_sb, w_branch_gdn, w_out, w_ffn2_up, w_ffn2_down):
    depth = w_in.shape[0]
    ffn_w = tuple(_to_bf16(w) for w in (w_ffn1_up, w_ffn1_down, w_ffn2_up, w_ffn2_down))
    w_in_bf = w_in.astype(BF16)
    weights = [_layer_weights(l, norm_gains, w_in_bf, conv_w, gdn_a_log, gdn_dt_bias, gdn_norm_gain,
                              w_branch_sb, w_branch_gdn, w_out) for l in range(depth)]
    bp = x_prompt.shape[0]
    chunk_prompt = 64
    zero_conv = jnp.zeros((depth, bp) + state_conv.shape[2:], state_conv.dtype)
    zero_state = jnp.zeros((depth, bp) + state_gdn.shape[2:], state_gdn.dtype)
    y_s, sk, sv, ss, sc = _run_group(x_sample, weights, ffn_w, cache_sb_k, cache_sb_v, state_conv, state_gdn,
                                     x_sample.shape[1])
    y_p, pk, pv, ps, pc = _run_group(x_prompt, weights, ffn_w, None, None, zero_conv, zero_state, chunk_prompt)
    return (y_p, y_s, pk, pv, ps, pc, sk, sv, ss, sc)
```

```python
import functools

import jax
import jax.numpy as jnp
from jax import lax
from jax.experimental import pallas as pl
from jax.experimental.pallas import tpu as pltpu

F32 = jnp.float32
BF16 = jnp.bfloat16

NORM_EPS = 1e-6
LOG2E = 1.4426950408889634
SB_HEADS = 8
GDN_HEADS = 4
LANES = 128
VMEM_LIMIT_BYTES = 56 * 1024 * 1024

ROW_TILE = 512
CAST_ROWS = 256
FF_CHUNK = 256
GDN_ROWS = 512
SB_KEY_TILE = 256
SB_QUERY_TILE = 256
SB_GROUP_LANES = 512
SB_ZERO_WEIGHT_LOG = 110.0
SB_MASKED_SCORE = -1e30
SB_NO_PAST_CARRY = 1e30


def _params(*sem):
    return pltpu.CompilerParams(dimension_semantics=sem, vmem_limit_bytes=VMEM_LIMIT_BYTES)


def _const_spec(shape):
    zeros = (0,) * len(shape)
    return pl.BlockSpec(shape, lambda *_: zeros)


def _rms(x, gain):
    ms = jnp.mean(x * x, axis=-1, keepdims=True)
    return x * lax.rsqrt(ms + NORM_EPS) * gain


def _dot(a, b):
    return jnp.dot(a, b, preferred_element_type=F32)


def _dot_nt(a, b):
    return lax.dot_general(a, b, (((1,), (1,)), ((), ())), preferred_element_type=F32)


def _dot_tn(a, b):
    return lax.dot_general(a, b, (((0,), (0,)), ((), ())), preferred_element_type=F32)


def _split3(x):
    hi = x.astype(BF16)
    r = x - hi.astype(F32)
    mid = r.astype(BF16)
    lo = (r - mid.astype(F32)).astype(BF16)
    return hi, mid, lo


def _dot_f32_exactrhs(a, b_bf16):
    hi, mid, lo = _split3(a)
    return _dot(hi, b_bf16) + _dot(mid, b_bf16) + _dot(lo, b_bf16)


def _dot_exactlhs_f32(a_bf16, b):
    hi, mid, lo = _split3(b)
    return _dot(a_bf16, hi) + _dot(a_bf16, mid) + _dot(a_bf16, lo)


def _ffn_kernel(x_ref, gin_ref, gout_ref, wup_ref, wd_ref, o_ref, act_ref):
    x = x_ref[...]
    h = _rms(x, gin_ref[...]).astype(BF16)
    d_ff = wd_ref.shape[1]
    for c in range(d_ff // FF_CHUNK):
        sl = slice(c * FF_CHUNK, (c + 1) * FF_CHUNK)
        g = _dot(h, wup_ref[0, :, sl])
        u = _dot(h, wup_ref[0, :, d_ff + c * FF_CHUNK:d_ff + (c + 1) * FF_CHUNK])
        act_ref[:, sl] = (g * jax.nn.sigmoid(g) * u).astype(BF16)
    y = _dot(act_ref[...], wd_ref[0])
    o_ref[...] = x + 0.5 * _rms(y, gout_ref[...])


def _ffn(x2, g_in, g_out, w_up, w_down, layer):
    n, d = x2.shape
    d_ff = w_down.shape[1]
    if n <= ROW_TILE:
        return _ffn_few_rows(x2, g_in, g_out, w_up, w_down, layer)
    tm = ROW_TILE
    row = pl.BlockSpec((tm, d), lambda i: (i, 0))
    return pl.pallas_call(
        _ffn_kernel,
        grid=(n // tm,),
        in_specs=[row, _const_spec((1, d)), _const_spec((1, d)),
                  pl.BlockSpec((1, d, 2 * d_ff), lambda i: (layer, 0, 0)),
                  pl.BlockSpec((1, d_ff, d), lambda i: (layer, 0, 0))],
        out_specs=row,
        out_shape=jax.ShapeDtypeStruct((n, d), F32),
        scratch_shapes=[pltpu.VMEM((tm, d_ff), BF16)],
        compiler_params=_params("parallel"),
        name="ffn_half_step",
    )(x2, g_in, g_out, w_up, w_down)


def _ffn_few_rows_kernel(x_ref, gin_ref, gout_ref, wg_ref, wu_ref, wd_ref, o_ref, h_ref, acc_ref):
    c = pl.program_id(0)

    @pl.when(c == 0)
    def _():
        h_ref[...] = _rms(x_ref[...], gin_ref[...]).astype(BF16)
        acc_ref[...] = jnp.zeros(acc_ref.shape, F32)

    h = h_ref[...]
    g = _dot(h, wg_ref[0])
    u = _dot(h, wu_ref[0])
    acc_ref[...] += _dot((g * jax.nn.sigmoid(g) * u).astype(BF16), wd_ref[0])

    @pl.when(c == pl.num_programs(0) - 1)
    def _():
        o_ref[...] = x_ref[...] + 0.5 * _rms(acc_ref[...], gout_ref[...])


def _ffn_few_rows(x2, g_in, g_out, w_up, w_down, layer):
    n, d = x2.shape
    d_ff = w_down.shape[1]
    n_chunks = d_ff // FF_CHUNK
    rows = _const_spec((n, d))
    return pl.pallas_call(
        _ffn_few_rows_kernel,
        grid=(n_chunks,),
        in_specs=[rows, _const_spec((1, d)), _const_spec((1, d)),
                  pl.BlockSpec((1, d, FF_CHUNK), lambda c: (layer, 0, c)),
                  pl.BlockSpec((1, d, FF_CHUNK), lambda c: (layer, 0, n_chunks + c)),
                  pl.BlockSpec((1, FF_CHUNK, d), lambda c: (layer, c, 0))],
        out_specs=rows,
        out_shape=jax.ShapeDtypeStruct((n, d), F32),
        scratch_shapes=[pltpu.VMEM((n, d), BF16), pltpu.VMEM((n, d), F32)],
        compiler_params=_params("arbitrary"),
        name="ffn_half_step_few_rows",
    )(x2, g_in, g_out, w_up, w_up, w_down)


def _cast_kernel(w_ref, o_ref):
    o_ref[...] = w_ref[...].astype(o_ref.dtype)


def _to_bf16(w):
    depth, rows, cols = w.shape
    tr = min(rows, CAST_ROWS)
    assert rows % tr == 0
    spec = pl.BlockSpec((1, tr, cols), lambda li, ri: (li, ri, 0))
    return pl.pallas_call(
        _cast_kernel,
        grid=(depth, rows // tr),
        in_specs=[spec],
        out_specs=spec,
        out_shape=jax.ShapeDtypeStruct(w.shape, BF16),
        compiler_params=_params("parallel", "parallel"),
        name="weights_to_bf16",
    )(w)


def _inproj_kernel(x_ref, g_ref, wa_ref, wba_ref, *refs, sb_w, conv_dim, gdn_w, sb_tile, q_scale):
    q_ref, k_ref, v_ref, kt_ref, vb_ref, u_ref, z_ref, ba_ref, bat_ref = refs[-9:]
    h = _rms(x_ref[0], g_ref[...]).astype(BF16)
    tm = h.shape[0]

    def proj(lo, width):
        return _dot(h, wa_ref[:, lo:lo + width])

    q_ref[0] = (proj(0, sb_w) * q_scale).astype(BF16)
    k = proj(sb_w, sb_w)
    k_ref[0, 0] = k
    for later in range(1, k_ref.shape[0]):
        k_ref[later, 0] = jnp.zeros_like(k)
        v_ref[later, 0] = jnp.zeros_like(k)
    kt = k.T.astype(BF16)
    for hp in range(sb_w // SB_GROUP_LANES):
        for s in range(tm // sb_tile):
            kt_ref[0, hp, s] = kt[hp * SB_GROUP_LANES:(hp + 1) * SB_GROUP_LANES,
                                  s * sb_tile:(s + 1) * sb_tile]
    v = proj(2 * sb_w, sb_w)
    v_ref[0, 0] = v
    vb_ref[0] = v.astype(BF16)
    u_ref[0] = proj(3 * sb_w, conv_dim)
    z_ref[0] = proj(3 * sb_w + conv_dim, gdn_w)
    ba = _dot(h, wba_ref[...])
    ba_ref[0] = ba
    bat_ref[0] = ba.T[:8]


def _inproj(x, gain, wa, wba, kv_stacks, *, layer, depth, sb_w, conv_dim, gdn_w, sb_tile):
    b, l, d = x.shape
    tm = min(ROW_TILE, l)
    n_hp = sb_w // SB_GROUP_LANES
    kern = functools.partial(_inproj_kernel, sb_w=sb_w, conv_dim=conv_dim, gdn_w=gdn_w,
                             sb_tile=sb_tile,
                             q_scale=float((sb_w // SB_HEADS) ** -0.5))

    def rows(width):
        return pl.BlockSpec((1, tm, width), lambda bi, ti: (bi, ti, 0))

    out_shape = (
        jax.ShapeDtypeStruct((b, l, sb_w), BF16),
        jax.ShapeDtypeStruct((depth, b, l, sb_w), F32),
        jax.ShapeDtypeStruct((depth, b, l, sb_w), F32),
        jax.ShapeDtypeStruct((b, n_hp, l // sb_tile, SB_GROUP_LANES, sb_tile), BF16),
        jax.ShapeDtypeStruct((b, l, sb_w), BF16),
        jax.ShapeDtypeStruct((b, l, conv_dim), F32),
        jax.ShapeDtypeStruct((b, l, gdn_w), F32),
        jax.ShapeDtypeStruct((b, l, LANES), F32),
        jax.ShapeDtypeStruct((b, 8, l), F32),
    )
    if kv_stacks:
        layer_rows = pl.BlockSpec((1, 1, tm, sb_w), lambda bi, ti: (layer, bi, ti, 0))
    else:
        assert layer == 0
        layer_rows = pl.BlockSpec((depth, 1, tm, sb_w), lambda bi, ti: (0, bi, ti, 0))
    out_specs = (
        rows(sb_w), layer_rows, layer_rows,
        pl.BlockSpec((1, n_hp, tm // sb_tile, SB_GROUP_LANES, sb_tile), lambda bi, ti: (bi, 0, ti, 0, 0)),
        rows(sb_w), rows(conv_dim), rows(gdn_w), rows(LANES),
        pl.BlockSpec((1, 8, tm), lambda bi, ti: (bi, 0, ti)),
    )
    return pl.pallas_call(
        kern,
        grid=(b, l // tm),
        in_specs=[rows(d), _const_spec((1, d)), _const_spec(wa.shape), _const_spec(wba.shape)]
        + [pl.BlockSpec(memory_space=pl.ANY)] * len(kv_stacks),
        out_specs=out_specs,
        out_shape=out_shape,
        input_output_aliases={4 + i: 1 + i for i in range(len(kv_stacks))},
        compiler_params=_params("parallel", "parallel"),
        name="in_projection",
    )(x, gain, wa, wba, *kv_stacks)


def _sb_kernel(q_ref, ktd_ref, vd_ref, ktp_ref, vp_ref, o_ref, carry_ref, *, tq, td, nd, tk, heads, dh,
               causal_past, n_past_static, past_keys_on_lanes):
    qi = pl.program_id(2)

    def later_key_matrix(n):
        r = lax.broadcasted_iota(jnp.int32, (n, n), 0)
        c = lax.broadcasted_iota(jnp.int32, (n, n), 1)
        return (r > c).astype(BF16)

    def sweep(blocks, st):
        pairs = [(blk, hh) for blk in blocks for hh in range(heads)]
        scores = []
        for blk, hh in pairs:
            lanes = slice(hh * dh, (hh + 1) * dh)
            z = _dot(q_all[:, lanes], blk["kt"][lanes, :])
            if blk["visible"] is not None:
                z = jnp.where(blk["visible"], z, SB_MASKED_SCORE)
            nlk = jnp.maximum(z, 0.0) + jnp.log(1.0 + jnp.exp2(jnp.abs(z) * (-LOG2E)))
            scores.append((z - nlk, nlk.astype(BF16), jnp.sum(nlk, axis=1, keepdims=True)))
        csums = [_dot(nlk_b, blk["u"]) for (blk, _), (_, nlk_b, _) in zip(pairs, scores)]
        st = list(st)
        for (blk, hh), (log_beta, _, row_sum), csum in zip(pairs, scores, csums):
            lanes = slice(hh * dh, (hh + 1) * dh)
            carry, acc = st[2 * hh], st[2 * hh + 1]
            if carry is not None and blk.get("carry_bias") is not None:
                carry = carry + blk["carry_bias"]
            log_a = log_beta - csum if carry is None else log_beta - csum - carry
            a = jnp.exp(log_a).astype(BF16)
            pv = _dot_nt(a, blk["vt"][lanes, :]) if "vt" in blk else _dot(a, blk["v"][:, lanes])
            st[2 * hh + 1] = pv if acc is None else acc + pv
            st[2 * hh] = row_sum if carry is None else carry + row_sum
        return st

    q_all = q_ref[0]
    state = [None] * (2 * heads)

    u_diag = later_key_matrix(td)
    u_past = u_diag if tk == td else later_key_matrix(tk)
    rows_i = lax.broadcasted_iota(jnp.int32, (tq, td), 0)
    cols_i = lax.broadcasted_iota(jnp.int32, (tq, td), 1)
    n_past = qi * (tq // tk) if causal_past else n_past_static

    def past_block(j, carry_bias=None):
        row0 = pl.multiple_of(j * tk, tk)
        blk = dict(visible=None, u=u_past, carry_bias=carry_bias)
        if past_keys_on_lanes:
            blk["kt"] = ktp_ref[0, 0, :, :, pl.ds(row0, tk)].reshape(heads * dh, tk).astype(BF16)
            blk["vt"] = vp_ref[0, 0, :, :, pl.ds(row0, tk)].reshape(heads * dh, tk).astype(BF16)
        else:
            blk["kt"] = ktp_ref[0, 0, j]
            blk["v"] = vp_ref[0, pl.ds(row0, tk), :]
        return blk

    first = [dict(kt=ktd_ref[0, 0, sd], v=vd_ref[0, sd * td:(sd + 1) * td, :],
                  visible=cols_i + sd * td < rows_i, u=u_diag) for sd in reversed(range(nd))]
    no_past_bias = jnp.where(n_past > 0, 0.0, SB_NO_PAST_CARRY) if causal_past else None
    first.append(past_block(jnp.maximum(n_past - 1, 0), no_past_bias))
    state = sweep(first, state)

    def min_carry(st):
        m = st[0]
        for hh in range(1, heads):
            m = jnp.minimum(m, st[2 * hh])
        return jnp.min(m)

    def cond(loop):
        i, smallest, _ = loop
        return jnp.logical_and(i < n_past, smallest < SB_ZERO_WEIGHT_LOG)

    def body(loop):
        i, _, st = loop
        st = sweep([past_block(n_past - 1 - i)], st)
        return i + 1, min_carry(st), tuple(st)

    _, smallest, st = lax.while_loop(cond, body, (jnp.int32(1), min_carry(state), tuple(state)))
    o_ref[0] = jnp.concatenate([st[2 * hh + 1] for hh in range(heads)], axis=1).astype(BF16)
    carry_ref[...] = jnp.full(carry_ref.shape, smallest, F32)


def _sb_attention(q, kt_diag, v_diag, kt_past, v_past, *, tq, td, tk, causal_past, cache_window=None):
    b, l, w = q.shape
    n_g, hw = kt_diag.shape[1], kt_diag.shape[3]
    assert tq % td == 0 and (not causal_past or tq % tk == 0)
    dh = w // SB_HEADS
    if cache_window is None:
        p = v_past.shape[1]
        n_past_blocks = kt_past.shape[2]
        past_specs = [pl.BlockSpec((1, 1, n_past_blocks, hw, tk), lambda bi, hp, qi: (bi, hp, 0, 0, 0),
                                   pipeline_mode=pl.Buffered(1)),
                      pl.BlockSpec((1, p, hw), lambda bi, hp, qi: (bi, 0, hp),
                                   pipeline_mode=pl.Buffered(1))]
    else:
        layer, key0, p = cache_window
        assert n_g == 1 and key0 % p == 0 and p % tk == 0
        n_past_blocks = p // tk
        past_specs = [pl.BlockSpec((1, 1, SB_HEADS, dh, p), lambda bi, hp, qi: (layer, bi, 0, 0, key0 // p),
                                   pipeline_mode=pl.Buffered(1))] * 2
    kern = functools.partial(_sb_kernel, tq=tq, td=td, nd=tq // td, tk=tk, heads=hw // dh, dh=dh,
                             causal_past=causal_past, n_past_static=n_past_blocks,
                             past_keys_on_lanes=cache_window is not None)
    return pl.pallas_call(
        kern,
        grid=(b, n_g, l // tq),
        in_specs=[
            pl.BlockSpec((1, tq, hw), lambda bi, hp, qi: (bi, qi, hp)),
            pl.BlockSpec((1, 1, tq // td, hw, td), lambda bi, hp, qi: (bi, hp, qi, 0, 0)),
            pl.BlockSpec((1, tq, hw), lambda bi, hp, qi: (bi, qi, hp)),
        ] + past_specs,
        out_specs=(pl.BlockSpec((1, tq, hw), lambda bi, hp, qi: (bi, qi, hp)),
                   pl.BlockSpec((1, 1, 1, 8, LANES), lambda bi, hp, qi: (bi, hp, qi, 0, 0))),
        out_shape=(jax.ShapeDtypeStruct((b, l, w), BF16),
                   jax.ShapeDtypeStruct((b, n_g, l // tq, 8, LANES), F32)),
        compiler_params=_params("parallel", "parallel", "arbitrary"),
        name="stick_breaking_attention",
    )(q, kt_diag, v_diag, kt_past, v_past)


def _gdn_kernel(u_ref, z_ref, ba_ref, bat_ref, hist_ref, s0_ref, cw_ref,
                alog_l_ref, dt_l_ref, alog_s_ref, dt_s_ref, hg_ref,
                o_ref, s_ref, cnew_ref, ext_ref, *, rows, chunk, width, n_taps):
    step = pl.program_id(1)
    n_steps = pl.num_programs(1)
    dk = width // GDN_HEADS
    n_sub = rows // chunk
    pad = 8
    n_hist = n_taps - 1

    @pl.when(step == 0)
    def _():
        ext_ref[0:pad, :] = jnp.zeros((pad, ext_ref.shape[1]), F32)
        ext_ref[pad - n_hist:pad, :] = hist_ref[0]
        s_ref[...] = s0_ref[...]

    ext_ref[pad:pad + rows, :] = u_ref[0]
    y = ext_ref[pad:pad + rows, :] * cw_ref[n_hist:n_taps, :]
    for i in reversed(range(n_hist)):
        y = y + ext_ref[pad - n_hist + i:pad - n_hist + i + rows, :] * cw_ref[i:i + 1, :]
    new_tail = ext_ref[rows:rows + pad, :]
    ext_ref[0:pad, :] = new_tail

    @pl.when(step == n_steps - 1)
    def _():
        cnew_ref[0] = new_tail[pad - n_hist:pad, :]

    qkv = y * jax.nn.sigmoid(y)

    heads = range(GDN_HEADS)
    qs, ks, vs = [], [], []
    for h in heads:
        q = qkv[:, h * dk:(h + 1) * dk]
        k = qkv[:, width + h * dk:width + (h + 1) * dk]
        qs.append(q * lax.rsqrt(jnp.sum(q * q, axis=-1, keepdims=True) + NORM_EPS) * float(dk ** -0.5))
        ks.append(k * lax.rsqrt(jnp.sum(k * k, axis=-1, keepdims=True) + NORM_EPS))
        vs.append(qkv[:, 2 * width + h * dk:2 * width + (h + 1) * dk])

    r_i = lax.broadcasted_iota(jnp.int32, (chunk, chunk), 0)
    c_i = lax.broadcasted_iota(jnp.int32, (chunk, chunk), 1)
    incl = r_i >= c_i
    strict = r_i > c_i
    eye = (r_i == c_i).astype(F32)
    rr = lax.broadcasted_iota(jnp.int32, (rows, rows), 0)
    cc = lax.broadcasted_iota(jnp.int32, (rows, rows), 1)
    same_chunk = (rr // chunk) == (cc // chunk)
    lower_incl = (same_chunk & (rr >= cc)).astype(BF16)
    upper_incl = (same_chunk & (rr <= cc)).astype(BF16)

    def softplus(t):
        return jnp.maximum(t, 0.0) + jnp.log(1.0 + jnp.exp(-jnp.abs(t)))

    ba = ba_ref[0]
    g_cols = -jnp.exp(alog_l_ref[...]) * softplus(ba + dt_l_ref[...])
    gcum_cols = _dot_exactlhs_f32(lower_incl, g_cols)
    beta_cols = jax.nn.sigmoid(ba)
    bat = bat_ref[0]
    g_rows = -jnp.exp(alog_s_ref[...]) * softplus(bat + dt_s_ref[...])
    gcum_rows = _dot_f32_exactrhs(g_rows, upper_incl)

    pairs = [(ci, h) for ci in range(n_sub) for h in heads]

    pre = {}
    for ci, h in pairs:
        rs = slice(ci * chunk, (ci + 1) * chunk)
        gc_col = gcum_cols[rs, GDN_HEADS + h:GDN_HEADS + h + 1]
        gc_row = gcum_rows[GDN_HEADS + h:GDN_HEADS + h + 1, rs]
        gc_last = gc_row[:, chunk - 1:chunk]
        beta = beta_cols[rs, h:h + 1]
        gamma = jnp.where(incl, jnp.exp(gc_col - gc_row), 0.0)
        decay_in = jnp.exp(gc_col)
        q, k, v = qs[h][rs], ks[h][rs], vs[h][rs]
        kb = k * beta
        kq = _dot_nt(jnp.concatenate([kb, q], axis=0).astype(BF16), k.astype(BF16))
        pre[ci, h] = dict(
            n=jnp.where(strict, -(kq[:chunk] * gamma), 0.0),
            qk=(kq[chunk:] * gamma).astype(BF16),
            rhs=jnp.concatenate([v * beta, kb * decay_in], axis=1).astype(BF16),
            q_dec=(q * decay_in).astype(BF16),
            k_end=(k * jnp.exp(gc_last - gc_col)).astype(BF16),
            chunk_decay=jnp.exp(gc_last))

    n_rounds = max(1, (chunk - 1).bit_length())
    m_pow = {p: pre[p]["n"] for p in pairs}
    t_inv = {p: eye + pre[p]["n"] for p in pairs}
    for i in range(n_rounds):
        last = i == n_rounds - 1
        for p in pairs:
            m_bf = m_pow[p].astype(BF16)
            if i == 0:
                m_pow[p] = _dot(m_bf, m_bf)
            elif last:
                t_inv[p] = t_inv[p] + _dot(t_inv[p].astype(BF16), m_bf)
            else:
                both = _dot(jnp.concatenate([m_bf, t_inv[p].astype(BF16)], axis=0), m_bf)
                m_pow[p] = both[:chunk]
                t_inv[p] = t_inv[p] + both[chunk:]

    sol = {p: _dot(t_inv[p].astype(BF16), pre[p]["rhs"]) for p in pairs}

    state = [s_ref[0, h] for h in heads]
    o_rows = [[None] * n_sub for _ in heads]
    for ci in range(n_sub):
        ws = [_dot(jnp.concatenate([sol[ci, h][:, dk:].astype(BF16), pre[ci, h]["q_dec"]], axis=0),
                   state[h].astype(BF16)) for h in heads]
        v_new = [(sol[ci, h][:, :dk] - ws[h][:chunk]).astype(BF16) for h in heads]
        for h in heads:
            o_rows[h][ci] = ws[h][chunk:] + _dot(pre[ci, h]["qk"], v_new[h])
        state = [state[h] * pre[ci, h]["chunk_decay"] + _dot_tn(pre[ci, h]["k_end"], v_new[h])
                 for h in heads]
    for h in heads:
        s_ref[0, h] = state[h]

    z = z_ref[0]
    outs = []
    for h in heads:
        o_h = o_rows[h][0] if n_sub == 1 else jnp.concatenate(o_rows[h], axis=0)
        z_h = z[:, h * dk:(h + 1) * dk]
        outs.append(_rms(o_h, hg_ref[...]) * (z_h * jax.nn.sigmoid(z_h)))
    o_ref[0] = jnp.concatenate(outs, axis=1).astype(BF16)


def _gdn(u, z, ba, bat, hist, s0, conv_w, alog_l, dt_l, alog_s, dt_s, head_gain, *, chunk):
    b, l, conv_dim = u.shape
    width = conv_dim // 3
    dk = width // GDN_HEADS
    n_taps = conv_w.shape[0]
    rows = min(GDN_ROWS, l)
    kern = functools.partial(_gdn_kernel, rows=rows, chunk=chunk, width=width, n_taps=n_taps)

    def row_block(wd):
        return pl.BlockSpec((1, rows, wd), lambda bi, si: (bi, si, 0))

    state_spec = pl.BlockSpec((1, GDN_HEADS, dk, dk), lambda bi, si: (bi, 0, 0, 0))
    hist_spec = pl.BlockSpec((1, n_taps - 1, conv_dim), lambda bi, si: (bi, 0, 0))
    return pl.pallas_call(
        kern,
        grid=(b, l // rows),
        in_specs=[row_block(conv_dim), row_block(width), row_block(LANES),
                  pl.BlockSpec((1, 8, rows), lambda bi, si: (bi, 0, si)),
                  hist_spec, state_spec, _const_spec(conv_w.shape),
                  _const_spec((1, LANES)), _const_spec((1, LANES)),
                  _const_spec((8, 1)), _const_spec((8, 1)), _const_spec((1, dk))],
        out_specs=(row_block(width), state_spec, hist_spec),
        out_shape=(jax.ShapeDtypeStruct((b, l, width), BF16),
                   jax.ShapeDtypeStruct(s0.shape, F32),
                   jax.ShapeDtypeStruct(hist.shape, F32)),
        scratch_shapes=[pltpu.VMEM((rows + 8, conv_dim), F32)],
        compiler_params=_params("parallel", "arbitrary"),
        name="gated_delta_rule",
    )(u, z, ba, bat, hist, s0, conv_w, alog_l, dt_l, alog_s, dt_s, head_gain)


def _merge_kernel(x_ref, osb_ref, ogdn_ref, gin_ref, gout_ref, wgate_ref, wsb_ref, wgdn_ref, wout_ref, o_ref):
    x = x_ref[...]
    d = x.shape[1]
    h = _rms(x, gin_ref[...]).astype(BF16)
    gate_sb = jax.nn.sigmoid(_dot(h, wgate_ref[:, :d]))
    merged = gate_sb * _dot(osb_ref[...], wsb_ref[...])
    gate_gdn = jax.nn.sigmoid(_dot(h, wgate_ref[:, d:]))
    merged = merged + gate_gdn * _dot(ogdn_ref[...], wgdn_ref[...])
    m = _dot(merged.astype(BF16), wout_ref[...])
    o_ref[...] = x + _rms(m, gout_ref[...])


def _merge(x2, osb2, ogdn2, g_in, g_out, wgate, wsb, wgdn, wout):
    n, d = x2.shape
    tm = min(ROW_TILE, n)

    def rows(wd):
        return pl.BlockSpec((tm, wd), lambda i: (i, 0))

    return pl.pallas_call(
        _merge_kernel,
        grid=(n // tm,),
        in_specs=[rows(d), rows(osb2.shape[1]), rows(ogdn2.shape[1]),
                  _const_spec((1, d)), _const_spec((1, d)),
                  _const_spec(wgate.shape), _const_spec(wsb.shape), _const_spec(wgdn.shape),
                  _const_spec(wout.shape)],
        out_specs=rows(d),
        out_shape=jax.ShapeDtypeStruct((n, d), F32),
        compiler_params=_params("parallel"),
        name="branch_merge",
    )(x2, osb2, ogdn2, g_in, g_out, wgate, wsb, wgdn, wout)


def _layer_weights(l, norm_gains, w_in_bf, conv_w, gdn_a_log, gdn_dt_bias, gdn_norm_gain,
                   w_branch_sb, w_branch_gdn, w_out):
    d = w_in_bf.shape[1]
    sb_w = w_branch_sb.shape[1]
    gdn_w = w_branch_gdn.shape[1]
    conv_dim = conv_w.shape[2]
    n_a = 3 * sb_w + conv_dim + gdn_w
    wi = w_in_bf[l]
    wba = jnp.zeros((d, LANES), BF16).at[:, :2 * GDN_HEADS].set(wi[:, n_a:n_a + 2 * GDN_HEADS])

    def lane_vec(p):
        return jnp.zeros((1, LANES), F32).at[0, GDN_HEADS:2 * GDN_HEADS].set(p)

    def sublane_vec(p):
        return jnp.zeros((8, 1), F32).at[GDN_HEADS:2 * GDN_HEADS, 0].set(p)

    return dict(
        gains=[norm_gains[l, i][None, :] for i in range(6)],
        wa=wi[:, :n_a],
        wba=wba,
        wgate=wi[:, n_a + 2 * GDN_HEADS:],
        wsb=w_branch_sb[l].astype(BF16), wgdn=w_branch_gdn[l].astype(BF16), wout=w_out[l].astype(BF16),
        conv_w=conv_w[l],
        alog_l=lane_vec(gdn_a_log[l]), dt_l=lane_vec(gdn_dt_bias[l]),
        alog_s=sublane_vec(gdn_a_log[l]), dt_s=sublane_vec(gdn_dt_bias[l]),
        head_gain=gdn_norm_gain[l][None, :],
        dims=(sb_w, conv_dim, gdn_w),
    )


def _run_group(x, weights, ffn_w, past_k, past_v, conv_hist, s0, chunk):
    b, l, d = x.shape
    n = b * l
    s_list, c_list = [], []
    kv_stacks = ()
    for li, w in enumerate(weights):
        sb_w, conv_dim, gdn_w = w["dims"]
        dh = sb_w // SB_HEADS
        tq = min(SB_QUERY_TILE, l)
        td = min(SB_KEY_TILE, l)
        g = w["gains"]
        x2 = _ffn(x.reshape(n, d), g[0], g[1], ffn_w[0], ffn_w[1], li)
        q, k_stack, v_stack, kt, vb, u, z, ba, bat = _inproj(
            x2.reshape(b, l, d), g[2], w["wa"], w["wba"], kv_stacks,
            layer=li, depth=len(weights), sb_w=sb_w, conv_dim=conv_dim, gdn_w=gdn_w, sb_tile=td)
        kv_stacks = (k_stack, v_stack)
        if past_k is None:
            o_sb, _ = _sb_attention(q, kt, vb, kt, vb, tq=tq, td=td, tk=td, causal_past=True)
        else:
            p = past_k.shape[2]
            tk = min(SB_KEY_TILE, p)

            kt_cache = jnp.transpose(past_k, (0, 1, 3, 4, 2))
            vt_cache = jnp.transpose(past_v, (0, 1, 3, 4, 2))

            def attend(key0, n_keys):
                return _sb_attention(q, kt, vb, kt_cache, vt_cache, tq=tq, td=td, tk=tk,
                                     causal_past=False, cache_window=(li, key0, n_keys))

            o_sb, carry_min = attend(p - tk, tk)
            if p > tk:
                o_sb = lax.cond(jnp.min(carry_min) >= SB_ZERO_WEIGHT_LOG, lambda: o_sb,
                                lambda: attend(0, p)[0])
        o_gdn, s_fin, conv_new = _gdn(u, z, ba, bat, conv_hist[li], s0[li], w["conv_w"],
                                      w["alog_l"], w["dt_l"], w["alog_s"], w["dt_s"], w["head_gain"],
                                      chunk=chunk)
        x2 = _merge(x2, o_sb.reshape(n, sb_w), o_gdn.reshape(n, gdn_w), g[2], g[3],
                    w["wgate"], w["wsb"], w["wgdn"], w["wout"])
        x2 = _ffn(x2, g[4], g[5], ffn_w[2], ffn_w[3], li)
        x = x2.reshape(b, l, d)
        s_list.append(s_fin)
        c_list.append(conv_new)
    k_all, v_all = (t.reshape(len(weights), b, l, SB_HEADS, dh) for t in kv_stacks)
    return x, k_all, v_all, jnp.stack(s_list), jnp.stack(c_list)


def kernel(x_prompt, x_sample, cache_sb_k, cache_sb_v, state_gdn, state_conv, norm_gains,
           w_ffn1_up, w_ffn1_down, w_in, conv_w, gdn_a_log, gdn_dt_bias, gdn_norm_gain,
           w_branch_sb, w_branch_gdn, w_out, w_ffn2_up, w_ffn2_down):
    depth = w_in.shape[0]
    ffn_w = tuple(_to_bf16(w) for w in (w_ffn1_up, w_ffn1_down, w_ffn2_up, w_ffn2_down))
    w_in_bf = w_in.astype(BF16)
    weights = [_layer_weights(l, norm_gains, w_in_bf, conv_w, gdn_a_log, gdn_dt_bias, gdn_norm_gain,
                              w_branch_sb, w_branch_gdn, w_out) for l in range(depth)]
    bp = x_prompt.shape[0]
    chunk_prompt = 64
    zero_conv = jnp.zeros((depth, bp) + state_conv.shape[2:], state_conv.dtype)
    zero_state = jnp.zeros((depth, bp) + state_gdn.shape[2:], state_gdn.dtype)
    y_p, pk, pv, ps, pc = _run_group(x_prompt, weights, ffn_w, None, None, zero_conv, zero_state, chunk_prompt)
    y_s, sk, sv, ss, sc = _run_group(x_sample, weights, ffn_w, cache_sb_k, cache_sb_v, state_conv, state_gdn,
                                     x_sample.shape[1])
    return (y_p, y_s, pk, pv, ps, pc, sk, sv, ss, sc)
```

```python
import functools

import jax
import jax.numpy as jnp
from jax import lax
from jax.experimental import pallas as pl
from jax.experimental.pallas import tpu as pltpu

F32 = jnp.float32
BF16 = jnp.bfloat16

NORM_EPS = 1e-6
LOG2E = 1.4426950408889634
SB_HEADS = 8
GDN_HEADS = 4
LANES = 128
VMEM_LIMIT_BYTES = 56 * 1024 * 1024

ROW_TILE = 512
CAST_ROWS = 256
FF_CHUNK = 256
GDN_ROWS = 512
SB_KEY_TILE = 256
SB_QUERY_TILE = 256
SB_GROUP_LANES = 512
SB_ZERO_WEIGHT_LOG = 110.0
SB_MASKED_SCORE = -1e30
SB_NO_PAST_CARRY = 1e30


def _params(*sem):
    return pltpu.CompilerParams(dimension_semantics=sem, vmem_limit_bytes=VMEM_LIMIT_BYTES)


def _const_spec(shape):
    zeros = (0,) * len(shape)
    return pl.BlockSpec(shape, lambda *_: zeros)


def _rms(x, gain):
    ms = jnp.mean(x * x, axis=-1, keepdims=True)
    return x * lax.rsqrt(ms + NORM_EPS) * gain


def _dot(a, b):
    return jnp.dot(a, b, preferred_element_type=F32)


def _dot_nt(a, b):
    return lax.dot_general(a, b, (((1,), (1,)), ((), ())), preferred_element_type=F32)


def _dot_tn(a, b):
    return lax.dot_general(a, b, (((0,), (0,)), ((), ())), preferred_element_type=F32)


def _split3(x):
    hi = x.astype(BF16)
    r = x - hi.astype(F32)
    mid = r.astype(BF16)
    lo = (r - mid.astype(F32)).astype(BF16)
    return hi, mid, lo


def _dot_f32_exactrhs(a, b_bf16):
    hi, mid, lo = _split3(a)
    return _dot(hi, b_bf16) + _dot(mid, b_bf16) + _dot(lo, b_bf16)


def _dot_exactlhs_f32(a_bf16, b):
    hi, mid, lo = _split3(b)
    return _dot(a_bf16, hi) + _dot(a_bf16, mid) + _dot(a_bf16, lo)


def _ffn_kernel(x_ref, gin_ref, gout_ref, wup_ref, wd_ref, o_ref, act_ref):
    x = x_ref[...]
    h = _rms(x, gin_ref[...]).astype(BF16)
    d_ff = wd_ref.shape[1]
    for c in range(d_ff // FF_CHUNK):
        sl = slice(c * FF_CHUNK, (c + 1) * FF_CHUNK)
        g = _dot(h, wup_ref[0, :, sl])
        u = _dot(h, wup_ref[0, :, d_ff + c * FF_CHUNK:d_ff + (c + 1) * FF_CHUNK])
        act_ref[:, sl] = (g * jax.nn.sigmoid(g) * u).astype(BF16)
    y = _dot(act_ref[...], wd_ref[0])
    o_ref[...] = x + 0.5 * _rms(y, gout_ref[...])


def _ffn(x2, g_in, g_out, w_up, w_down, layer):
    n, d = x2.shape
    d_ff = w_down.shape[1]
    tm = min(ROW_TILE, n)
    row = pl.BlockSpec((tm, d), lambda i: (i, 0))
    return pl.pallas_call(
        _ffn_kernel,
        grid=(n // tm,),
        in_specs=[row, _const_spec((1, d)), _const_spec((1, d)),
                  pl.BlockSpec((1, d, 2 * d_ff), lambda i: (layer, 0, 0)),
                  pl.BlockSpec((1, d_ff, d), lambda i: (layer, 0, 0))],
        out_specs=row,
        out_shape=jax.ShapeDtypeStruct((n, d), F32),
        scratch_shapes=[pltpu.VMEM((tm, d_ff), BF16)],
        compiler_params=_params("parallel"),
        name="ffn_half_step",
    )(x2, g_in, g_out, w_up, w_down)


def _cast_kernel(w_ref, o_ref):
    o_ref[...] = w_ref[...].astype(o_ref.dtype)


def _to_bf16(w):
    depth, rows, cols = w.shape
    tr = min(rows, CAST_ROWS)
    assert rows % tr == 0
    spec = pl.BlockSpec((1, tr, cols), lambda li, ri: (li, ri, 0))
    return pl.pallas_call(
        _cast_kernel,
        grid=(depth, rows // tr),
        in_specs=[spec],
        out_specs=spec,
        out_shape=jax.ShapeDtypeStruct(w.shape, BF16),
        compiler_params=_params("parallel", "parallel"),
        name="weights_to_bf16",
    )(w)


def _inproj_kernel(x_ref, g_ref, wa_ref, wba_ref, *refs, sb_w, conv_dim, gdn_w, sb_tile, q_scale):
    q_ref, k_ref, v_ref, kt_ref, vb_ref, u_ref, z_ref, ba_ref, bat_ref = refs[-9:]
    h = _rms(x_ref[0], g_ref[...]).astype(BF16)
    tm = h.shape[0]

    def proj(lo, width):
        return _dot(h, wa_ref[:, lo:lo + width])

    q_ref[0] = (proj(0, sb_w) * q_scale).astype(BF16)
    k = proj(sb_w, sb_w)
    k_ref[0, 0] = k
    for later in range(1, k_ref.shape[0]):
        k_ref[later, 0] = jnp.zeros_like(k)
        v_ref[later, 0] = jnp.zeros_like(k)
    kt = k.T.astype(BF16)
    for hp in range(sb_w // SB_GROUP_LANES):
        for s in range(tm // sb_tile):
            kt_ref[0, hp, s] = kt[hp * SB_GROUP_LANES:(hp + 1) * SB_GROUP_LANES,
                                  s * sb_tile:(s + 1) * sb_tile]
    v = proj(2 * sb_w, sb_w)
    v_ref[0, 0] = v
    vb_ref[0] = v.astype(BF16)
    u_ref[0] = proj(3 * sb_w, conv_dim)
    z_ref[0] = proj(3 * sb_w + conv_dim, gdn_w)
    ba = _dot(h, wba_ref[...])
    ba_ref[0] = ba
    bat_ref[0] = ba.T[:8]


def _inproj(x, gain, wa, wba, kv_stacks, *, layer, depth, sb_w, conv_dim, gdn_w, sb_tile):
    b, l, d = x.shape
    tm = min(ROW_TILE, l)
    n_hp = sb_w // SB_GROUP_LANES
    kern = functools.partial(_inproj_kernel, sb_w=sb_w, conv_dim=conv_dim, gdn_w=gdn_w,
                             sb_tile=sb_tile,
                             q_scale=float((sb_w // SB_HEADS) ** -0.5))

    def rows(width):
        return pl.BlockSpec((1, tm, width), lambda bi, ti: (bi, ti, 0))

    out_shape = (
        jax.ShapeDtypeStruct((b, l, sb_w), BF16),
        jax.ShapeDtypeStruct((depth, b, l, sb_w), F32),
        jax.ShapeDtypeStruct((depth, b, l, sb_w), F32),
        jax.ShapeDtypeStruct((b, n_hp, l // sb_tile, SB_GROUP_LANES, sb_tile), BF16),
        jax.ShapeDtypeStruct((b, l, sb_w), BF16),
        jax.ShapeDtypeStruct((b, l, conv_dim), F32),
        jax.ShapeDtypeStruct((b, l, gdn_w), F32),
        jax.ShapeDtypeStruct((b, l, LANES), F32),
        jax.ShapeDtypeStruct((b, 8, l), F32),
    )
    if kv_stacks:
        layer_rows = pl.BlockSpec((1, 1, tm, sb_w), lambda bi, ti: (layer, bi, ti, 0))
    else:
        assert layer == 0
        layer_rows = pl.BlockSpec((depth, 1, tm, sb_w), lambda bi, ti: (0, bi, ti, 0))
    out_specs = (
        rows(sb_w), layer_rows, layer_rows,
        pl.BlockSpec((1, n_hp, tm // sb_tile, SB_GROUP_LANES, sb_tile), lambda bi, ti: (bi, 0, ti, 0, 0)),
        rows(sb_w), rows(conv_dim), rows(gdn_w), rows(LANES),
        pl.BlockSpec((1, 8, tm), lambda bi, ti: (bi, 0, ti)),
    )
    return pl.pallas_call(
        kern,
        grid=(b, l // tm),
        in_specs=[rows(d), _const_spec((1, d)), _const_spec(wa.shape), _const_spec(wba.shape)]
        + [pl.BlockSpec(memory_space=pl.ANY)] * len(kv_stacks),
        out_specs=out_specs,
        out_shape=out_shape,
        input_output_aliases={4 + i: 1 + i for i in range(len(kv_stacks))},
        compiler_params=_params("parallel", "parallel"),
        name="in_projection",
    )(x, gain, wa, wba, *kv_stacks)


def _sb_kernel(q_ref, ktd_ref, vd_ref, ktp_ref, vp_ref, o_ref, carry_ref, *, tq, td, nd, tk, heads, dh,
               causal_past, n_past_static, past_keys_on_lanes):
    qi = pl.program_id(2)

    def later_key_matrix(n):
        r = lax.broadcasted_iota(jnp.int32, (n, n), 0)
        c = lax.broadcasted_iota(jnp.int32, (n, n), 1)
        return (r > c).astype(BF16)

    def sweep(blocks, st):
        pairs = [(blk, hh) for blk in blocks for hh in range(heads)]
        scores = []
        for blk, hh in pairs:
            lanes = slice(hh * dh, (hh + 1) * dh)
            z = _dot(q_all[:, lanes], blk["kt"][lanes, :])
            if blk["visible"] is not None:
                z = jnp.where(blk["visible"], z, SB_MASKED_SCORE)
            nlk = jnp.maximum(z, 0.0) + jnp.log(1.0 + jnp.exp2(jnp.abs(z) * (-LOG2E)))
            scores.append((z - nlk, nlk.astype(BF16), jnp.sum(nlk, axis=1, keepdims=True)))
        csums = [_dot(nlk_b, blk["u"]) for (blk, _), (_, nlk_b, _) in zip(pairs, scores)]
        st = list(st)
        for (blk, hh), (log_beta, _, row_sum), csum in zip(pairs, scores, csums):
            lanes = slice(hh * dh, (hh + 1) * dh)
            carry, acc = st[2 * hh], st[2 * hh + 1]
            if carry is not None and blk.get("carry_bias") is not None:
                carry = carry + blk["carry_bias"]
            log_a = log_beta - csum if carry is None else log_beta - csum - carry
            a = jnp.exp(log_a).astype(BF16)
            pv = _dot_nt(a, blk["vt"][lanes, :]) if "vt" in blk else _dot(a, blk["v"][:, lanes])
            st[2 * hh + 1] = pv if acc is None else acc + pv
            st[2 * hh] = row_sum if carry is None else carry + row_sum
        return st

    q_all = q_ref[0]
    state = [None] * (2 * heads)

    u_diag = later_key_matrix(td)
    u_past = u_diag if tk == td else later_key_matrix(tk)
    rows_i = lax.broadcasted_iota(jnp.int32, (tq, td), 0)
    cols_i = lax.broadcasted_iota(jnp.int32, (tq, td), 1)
    n_past = qi * (tq // tk) if causal_past else n_past_static

    def past_block(j, carry_bias=None):
        row0 = pl.multiple_of(j * tk, tk)
        blk = dict(visible=None, u=u_past, carry_bias=carry_bias)
        if past_keys_on_lanes:
            blk["kt"] = ktp_ref[0, 0, :, :, pl.ds(row0, tk)].reshape(heads * dh, tk).astype(BF16)
            blk["vt"] = vp_ref[0, 0, :, :, pl.ds(row0, tk)].reshape(heads * dh, tk).astype(BF16)
        else:
            blk["kt"] = ktp_ref[0, 0, j]
            blk["v"] = vp_ref[0, pl.ds(row0, tk), :]
        return blk

    first = [dict(kt=ktd_ref[0, 0, sd], v=vd_ref[0, sd * td:(sd + 1) * td, :],
                  visible=cols_i + sd * td < rows_i, u=u_diag) for sd in reversed(range(nd))]
    no_past_bias = jnp.where(n_past > 0, 0.0, SB_NO_PAST_CARRY) if causal_past else None
    first.append(past_block(jnp.maximum(n_past - 1, 0), no_past_bias))
    state = sweep(first, state)

    def min_carry(st):
        m = st[0]
        for hh in range(1, heads):
            m = jnp.minimum(m, st[2 * hh])
        return jnp.min(m)

    def cond(loop):
        i, smallest, _ = loop
        return jnp.logical_and(i < n_past, smallest < SB_ZERO_WEIGHT_LOG)

    def body(loop):
        i, _, st = loop
        st = sweep([past_block(n_past - 1 - i)], st)
        return i + 1, min_carry(st), tuple(st)

    _, smallest, st = lax.while_loop(cond, body, (jnp.int32(1), min_carry(state), tuple(state)))
    o_ref[0] = jnp.concatenate([st[2 * hh + 1] for hh in range(heads)], axis=1).astype(BF16)
    carry_ref[...] = jnp.full(carry_ref.shape, smallest, F32)


def _sb_attention(q, kt_diag, v_diag, kt_past, v_past, *, tq, td, tk, causal_past, cache_window=None):
    b, l, w = q.shape
    n_g, hw = kt_diag.shape[1], kt_diag.shape[3]
    assert tq % td == 0 and (not causal_past or tq % tk == 0)
    dh = w // SB_HEADS
    if cache_window is None:
        p = v_past.shape[1]
        n_past_blocks = kt_past.shape[2]
        past_specs = [pl.BlockSpec((1, 1, n_past_blocks, hw, tk), lambda bi, hp, qi: (bi, hp, 0, 0, 0),
                                   pipeline_mode=pl.Buffered(1)),
                      pl.BlockSpec((1, p, hw), lambda bi, hp, qi: (bi, 0, hp),
                                   pipeline_mode=pl.Buffered(1))]
    else:
        layer, key0, p = cache_window
        assert n_g == 1 and key0 % p == 0 and p % tk == 0
        n_past_blocks = p // tk
        past_specs = [pl.BlockSpec((1, 1, SB_HEADS, dh, p), lambda bi, hp, qi: (layer, bi, 0, 0, key0 // p),
                                   pipeline_mode=pl.Buffered(1))] * 2
    kern = functools.partial(_sb_kernel, tq=tq, td=td, nd=tq // td, tk=tk, heads=hw // dh, dh=dh,
                             causal_past=causal_past, n_past_static=n_past_blocks,
                             past_keys_on_lanes=cache_window is not None)
    return pl.pallas_call(
        kern,
        grid=(b, n_g, l // tq),
        in_specs=[
            pl.BlockSpec((1, tq, hw), lambda bi, hp, qi: (bi, qi, hp)),
            pl.BlockSpec((1, 1, tq // td, hw, td), lambda bi, hp, qi: (bi, hp, qi, 0, 0)),
            pl.BlockSpec((1, tq, hw), lambda bi, hp, qi: (bi, qi, hp)),
        ] + past_specs,
        out_specs=(pl.BlockSpec((1, tq, hw), lambda bi, hp, qi: (bi, qi, hp)),
                   pl.BlockSpec((1, 1, 1, 8, LANES), lambda bi, hp, qi: (bi, hp, qi, 0, 0))),
        out_shape=(jax.ShapeDtypeStruct((b, l, w), BF16),
                   jax.ShapeDtypeStruct((b, n_g, l // tq, 8, LANES), F32)),
        compiler_params=_params("parallel", "parallel", "arbitrary"),
        name="stick_breaking_attention",
    )(q, kt_diag, v_diag, kt_past, v_past)


def _gdn_kernel(u_ref, z_ref, ba_ref, bat_ref, hist_ref, s0_ref, cw_ref,
                alog_l_ref, dt_l_ref, alog_s_ref, dt_s_ref, hg_ref,
                o_ref, s_ref, cnew_ref, ext_ref, *, rows, chunk, width, n_taps):
    step = pl.program_id(1)
    n_steps = pl.num_programs(1)
    dk = width // GDN_HEADS
    n_sub = rows // chunk
    pad = 8
    n_hist = n_taps - 1

    @pl.when(step == 0)
    def _():
        ext_ref[0:pad, :] = jnp.zeros((pad, ext_ref.shape[1]), F32)
        ext_ref[pad - n_hist:pad, :] = hist_ref[0]
        s_ref[...] = s0_ref[...]

    ext_ref[pad:pad + rows, :] = u_ref[0]
    y = ext_ref[pad:pad + rows, :] * cw_ref[n_hist:n_taps, :]
    for i in reversed(range(n_hist)):
        y = y + ext_ref[pad - n_hist + i:pad - n_hist + i + rows, :] * cw_ref[i:i + 1, :]
    new_tail = ext_ref[rows:rows + pad, :]
    ext_ref[0:pad, :] = new_tail

    @pl.when(step == n_steps - 1)
    def _():
        cnew_ref[0] = new_tail[pad - n_hist:pad, :]

    qkv = y * jax.nn.sigmoid(y)

    heads = range(GDN_HEADS)
    qs, ks, vs = [], [], []
    for h in heads:
        q = qkv[:, h * dk:(h + 1) * dk]
        k = qkv[:, width + h * dk:width + (h + 1) * dk]
        qs.append(q * lax.rsqrt(jnp.sum(q * q, axis=-1, keepdims=True) + NORM_EPS) * float(dk ** -0.5))
        ks.append(k * lax.rsqrt(jnp.sum(k * k, axis=-1, keepdims=True) + NORM_EPS))
        vs.append(qkv[:, 2 * width + h * dk:2 * width + (h + 1) * dk])

    r_i = lax.broadcasted_iota(jnp.int32, (chunk, chunk), 0)
    c_i = lax.broadcasted_iota(jnp.int32, (chunk, chunk), 1)
    incl = r_i >= c_i
    strict = r_i > c_i
    eye = (r_i == c_i).astype(F32)
    rr = lax.broadcasted_iota(jnp.int32, (rows, rows), 0)
    cc = lax.broadcasted_iota(jnp.int32, (rows, rows), 1)
    same_chunk = (rr // chunk) == (cc // chunk)
    lower_incl = (same_chunk & (rr >= cc)).astype(BF16)
    upper_incl = (same_chunk & (rr <= cc)).astype(BF16)

    def softplus(t):
        return jnp.maximum(t, 0.0) + jnp.log(1.0 + jnp.exp(-jnp.abs(t)))

    ba = ba_ref[0]
    g_cols = -jnp.exp(alog_l_ref[...]) * softplus(ba + dt_l_ref[...])
    gcum_cols = _dot_exactlhs_f32(lower_incl, g_cols)
    beta_cols = jax.nn.sigmoid(ba)
    bat = bat_ref[0]
    g_rows = -jnp.exp(alog_s_ref[...]) * softplus(bat + dt_s_ref[...])
    gcum_rows = _dot_f32_exactrhs(g_rows, upper_incl)

    pairs = [(ci, h) for ci in range(n_sub) for h in heads]

    pre = {}
    for ci, h in pairs:
        rs = slice(ci * chunk, (ci + 1) * chunk)
        gc_col = gcum_cols[rs, GDN_HEADS + h:GDN_HEADS + h + 1]
        gc_row = gcum_rows[GDN_HEADS + h:GDN_HEADS + h + 1, rs]
        gc_last = gc_row[:, chunk - 1:chunk]
        beta = beta_cols[rs, h:h + 1]
        gamma = jnp.where(incl, jnp.exp(gc_col - gc_row), 0.0)
        decay_in = jnp.exp(gc_col)
        q, k, v = qs[h][rs], ks[h][rs], vs[h][rs]
        kb = k * beta
        kq = _dot_nt(jnp.concatenate([kb, q], axis=0).astype(BF16), k.astype(BF16))
        pre[ci, h] = dict(
            n=jnp.where(strict, -(kq[:chunk] * gamma), 0.0),
            qk=(kq[chunk:] * gamma).astype(BF16),
            rhs=jnp.concatenate([v * beta, kb * decay_in], axis=1).astype(BF16),
            q_dec=(q * decay_in).astype(BF16),
            k_end=(k * jnp.exp(gc_last - gc_col)).astype(BF16),
            chunk_decay=jnp.exp(gc_last))

    n_rounds = max(1, (chunk - 1).bit_length())
    m_pow = {p: pre[p]["n"] for p in pairs}
    t_inv = {p: eye + pre[p]["n"] for p in pairs}
    for i in range(n_rounds):
        last = i == n_rounds - 1
        for p in pairs:
            m_bf = m_pow[p].astype(BF16)
            if i == 0:
                m_pow[p] = _dot(m_bf, m_bf)
            elif last:
                t_inv[p] = t_inv[p] + _dot(t_inv[p].astype(BF16), m_bf)
            else:
                both = _dot(jnp.concatenate([m_bf, t_inv[p].astype(BF16)], axis=0), m_bf)
                m_pow[p] = both[:chunk]
                t_inv[p] = t_inv[p] + both[chunk:]

    sol = {p: _dot(t_inv[p].astype(BF16), pre[p]["rhs"]) for p in pairs}

    state = [s_ref[0, h] for h in heads]
    o_rows = [[None] * n_sub for _ in heads]
    for ci in range(n_sub):
        ws = [_dot(jnp.concatenate([sol[ci, h][:, dk:].astype(BF16), pre[ci, h]["q_dec"]], axis=0),
                   state[h].astype(BF16)) for h in heads]
        v_new = [(sol[ci, h][:, :dk] - ws[h][:chunk]).astype(BF16) for h in heads]
        for h in heads:
            o_rows[h][ci] = ws[h][chunk:] + _dot(pre[ci, h]["qk"], v_new[h])
        state = [state[h] * pre[ci, h]["chunk_decay"] + _dot_tn(pre[ci, h]["k_end"], v_new[h])
                 for h in heads]
    for h in heads:
        s_ref[0, h] = state[h]

    z = z_ref[0]
    outs = []
    for h in heads:
        o_h = o_rows[h][0] if n_sub == 1 else jnp.concatenate(o_rows[h], axis=0)
        z_h = z[:, h * dk:(h + 1) * dk]
        outs.append(_rms(o_h, hg_ref[...]) * (z_h * jax.nn.sigmoid(z_h)))
    o_ref[0] = jnp.concatenate(outs, axis=1).astype(BF16)


def _gdn(u, z, ba, bat, hist, s0, conv_w, alog_l, dt_l, alog_s, dt_s, head_gain, *, chunk):
    b, l, conv_dim = u.shape
    width = conv_dim // 3
    dk = width // GDN_HEADS
    n_taps = conv_w.shape[0]
    rows = min(GDN_ROWS, l)
    kern = functools.partial(_gdn_kernel, rows=rows, chunk=chunk, width=width, n_taps=n_taps)

    def row_block(wd):
        return pl.BlockSpec((1, rows, wd), lambda bi, si: (bi, si, 0))

    state_spec = pl.BlockSpec((1, GDN_HEADS, dk, dk), lambda bi, si: (bi, 0, 0, 0))
    hist_spec = pl.BlockSpec((1, n_taps - 1, conv_dim), lambda bi, si: (bi, 0, 0))
    return pl.pallas_call(
        kern,
        grid=(b, l // rows),
        in_specs=[row_block(conv_dim), row_block(width), row_block(LANES),
                  pl.BlockSpec((1, 8, rows), lambda bi, si: (bi, 0, si)),
                  hist_spec, state_spec, _const_spec(conv_w.shape),
                  _const_spec((1, LANES)), _const_spec((1, LANES)),
                  _const_spec((8, 1)), _const_spec((8, 1)), _const_spec((1, dk))],
        out_specs=(row_block(width), state_spec, hist_spec),
        out_shape=(jax.ShapeDtypeStruct((b, l, width), BF16),
                   jax.ShapeDtypeStruct(s0.shape, F32),
                   jax.ShapeDtypeStruct(hist.shape, F32)),
        scratch_shapes=[pltpu.VMEM((rows + 8, conv_dim), F32)],
        compiler_params=_params("parallel", "arbitrary"),
        name="gated_delta_rule",
    )(u, z, ba, bat, hist, s0, conv_w, alog_l, dt_l, alog_s, dt_s, head_gain)


def _merge_ffn_kernel(x_ref, osb_ref, ogdn_ref, gin_ref, gout_ref, wgate_ref, wsb_ref, wgdn_ref, wout_ref,
                      fgin_ref, fgout_ref, wup_ref, wd_ref, o_ref, act_ref):
    x = x_ref[...]
    d = x.shape[1]
    h = _rms(x, gin_ref[...]).astype(BF16)
    merged = jax.nn.sigmoid(_dot(h, wgate_ref[:, :d])) * _dot(osb_ref[...], wsb_ref[...])
    merged = merged + jax.nn.sigmoid(_dot(h, wgate_ref[:, d:])) * _dot(ogdn_ref[...], wgdn_ref[...])
    x = x + _rms(_dot(merged.astype(BF16), wout_ref[...]), gout_ref[...])
    h = _rms(x, fgin_ref[...]).astype(BF16)
    d_ff = wd_ref.shape[1]
    for c in range(d_ff // FF_CHUNK):
        sl = slice(c * FF_CHUNK, (c + 1) * FF_CHUNK)
        g = _dot(h, wup_ref[0, :, sl])
        u = _dot(h, wup_ref[0, :, d_ff + c * FF_CHUNK:d_ff + (c + 1) * FF_CHUNK])
        act_ref[:, sl] = (g * jax.nn.sigmoid(g) * u).astype(BF16)
    o_ref[...] = x + 0.5 * _rms(_dot(act_ref[...], wd_ref[0]), fgout_ref[...])


def _merge_ffn(x2, osb2, ogdn2, g_in, g_out, wgate, wsb, wgdn, wout, f_in, f_out, w_up, w_down, layer):
    n, d = x2.shape
    d_ff = w_down.shape[1]
    tm = min(ROW_TILE, n)

    def rows(wd):
        return pl.BlockSpec((tm, wd), lambda i: (i, 0))

    def fixed(shape, index=None):
        index = index or (0,) * len(shape)
        return pl.BlockSpec(shape, lambda i: index, pipeline_mode=pl.Buffered(1))

    return pl.pallas_call(
        _merge_ffn_kernel,
        grid=(n // tm,),
        in_specs=[rows(d), rows(osb2.shape[1]), rows(ogdn2.shape[1]),
                  _const_spec((1, d)), _const_spec((1, d)),
                  fixed(wgate.shape), fixed(wsb.shape), fixed(wgdn.shape), fixed(wout.shape),
                  _const_spec((1, d)), _const_spec((1, d)),
                  fixed((1, d, 2 * d_ff), (layer, 0, 0)), fixed((1, d_ff, d), (layer, 0, 0))],
        out_specs=rows(d),
        out_shape=jax.ShapeDtypeStruct((n, d), F32),
        scratch_shapes=[pltpu.VMEM((tm, d_ff), BF16)],
        compiler_params=_params("parallel"),
        name="merge_and_ffn_half_step",
    )(x2, osb2, ogdn2, g_in, g_out, wgate, wsb, wgdn, wout, f_in, f_out, w_up, w_down)


def _layer_weights(l, norm_gains, w_in_bf, conv_w, gdn_a_log, gdn_dt_bias, gdn_norm_gain,
                   w_branch_sb, w_branch_gdn, w_out):
    d = w_in_bf.shape[1]
    sb_w = w_branch_sb.shape[1]
    gdn_w = w_branch_gdn.shape[1]
    conv_dim = conv_w.shape[2]
    n_a = 3 * sb_w + conv_dim + gdn_w
    wi = w_in_bf[l]
    wba = jnp.zeros((d, LANES), BF16).at[:, :2 * GDN_HEADS].set(wi[:, n_a:n_a + 2 * GDN_HEADS])

    def lane_vec(p):
        return jnp.zeros((1, LANES), F32).at[0, GDN_HEADS:2 * GDN_HEADS].set(p)

    def sublane_vec(p):
        return jnp.zeros((8, 1), F32).at[GDN_HEADS:2 * GDN_HEADS, 0].set(p)

    return dict(
        gains=[norm_gains[l, i][None, :] for i in range(6)],
        wa=wi[:, :n_a],
        wba=wba,
        wgate=wi[:, n_a + 2 * GDN_HEADS:],
        wsb=w_branch_sb[l].astype(BF16), wgdn=w_branch_gdn[l].astype(BF16), wout=w_out[l].astype(BF16),
        conv_w=conv_w[l],
        alog_l=lane_vec(gdn_a_log[l]), dt_l=lane_vec(gdn_dt_bias[l]),
        alog_s=sublane_vec(gdn_a_log[l]), dt_s=sublane_vec(gdn_dt_bias[l]),
        head_gain=gdn_norm_gain[l][None, :],
        dims=(sb_w, conv_dim, gdn_w),
    )


def _run_group(x, weights, ffn_w, past_k, past_v, conv_hist, s0, chunk):
    b, l, d = x.shape
    n = b * l
    s_list, c_list = [], []
    kv_stacks = ()
    for li, w in enumerate(weights):
        sb_w, conv_dim, gdn_w = w["dims"]
        dh = sb_w // SB_HEADS
        tq = min(SB_QUERY_TILE, l)
        td = min(SB_KEY_TILE, l)
        g = w["gains"]
        x2 = _ffn(x.reshape(n, d), g[0], g[1], ffn_w[0], ffn_w[1], li)
        q, k_stack, v_stack, kt, vb, u, z, ba, bat = _inproj(
            x2.reshape(b, l, d), g[2], w["wa"], w["wba"], kv_stacks,
            layer=li, depth=len(weights), sb_w=sb_w, conv_dim=conv_dim, gdn_w=gdn_w, sb_tile=td)
        kv_stacks = (k_stack, v_stack)
        if past_k is None:
            o_sb, _ = _sb_attention(q, kt, vb, kt, vb, tq=tq, td=td, tk=td, causal_past=True)
        else:
            p = past_k.shape[2]
            tk = min(SB_KEY_TILE, p)

            kt_cache = jnp.transpose(past_k, (0, 1, 3, 4, 2))
            vt_cache = jnp.transpose(past_v, (0, 1, 3, 4, 2))

            def attend(key0, n_keys):
                return _sb_attention(q, kt, vb, kt_cache, vt_cache, tq=tq, td=td, tk=tk,
                                     causal_past=False, cache_window=(li, key0, n_keys))

            o_sb, carry_min = attend(p - tk, tk)
            if p > tk:
                o_sb = lax.cond(jnp.min(carry_min) >= SB_ZERO_WEIGHT_LOG, lambda: o_sb,
                                lambda: attend(0, p)[0])
        o_gdn, s_fin, conv_new = _gdn(u, z, ba, bat, conv_hist[li], s0[li], w["conv_w"],
                                      w["alog_l"], w["dt_l"], w["alog_s"], w["dt_s"], w["head_gain"],
                                      chunk=chunk)
        x2 = _merge_ffn(x2, o_sb.reshape(n, sb_w), o_gdn.reshape(n, gdn_w), g[2], g[3],
                        w["wgate"], w["wsb"], w["wgdn"], w["wout"], g[4], g[5], ffn_w[2], ffn_w[3], li)
        x = x2.reshape(b, l, d)
        s_list.append(s_fin)
        c_list.append(conv_new)
    k_all, v_all = (t.reshape(len(weights), b, l, SB_HEADS, dh) for t in kv_stacks)
    return x, k_all, v_all, jnp.stack(s_list), jnp.stack(c_list)


def kernel(x_prompt, x_sample, cache_sb_k, cache_sb_v, state_gdn, state_conv, norm_gains,
           w_ffn1_up, w_ffn1_down, w_in, conv_w, gdn_a_log, gdn_dt_bias, gdn_norm_gain,
           w_branch_sb, w_branch_gdn, w_out, w_ffn2_up, w_ffn2_down):
    depth = w_in.shape[0]
    ffn_w = tuple(_to_bf16(w) for w in (w_ffn1_up, w_ffn1_down, w_ffn2_up, w_ffn2_down))
    w_in_bf = w_in.astype(BF16)
    weights = [_layer_weights(l, norm_gains, w_in_bf, conv_w, gdn_a_log, gdn_dt_bias, gdn_norm_gain,
                              w_branch_sb, w_branch_gdn, w_out) for l in range(depth)]
    bp = x_prompt.shape[0]
    chunk_prompt = 64
    zero_conv = jnp.zeros((depth, bp) + state_conv.shape[2:], state_conv.dtype)
    zero_state = jnp.zeros((depth, bp) + state_gdn.shape[2:], state_gdn.dtype)
    y_p, pk, pv, ps, pc = _run_group(x_prompt, weights, ffn_w, None, None, zero_conv, zero_state, chunk_prompt)
    y_s, sk, sv, ss, sc = _run_group(x_sample, weights, ffn_w, cache_sb_k, cache_sb_v, state_conv, state_gdn,
                                     x_sample.shape[1])
    return (y_p, y_s, pk, pv, ps, pc, sk, sv, ss, sc)
```
